```python
import math
import jax, jax.numpy as jnp
from jax import lax
import numpy as np

D_MODEL = 1024
BATCH = 8
SEQ = 2048
DEPTH = 2

CTX_LEN = 256
GRID_W = 64
HD = 64
BLOCK = 128
WINDOW = 128
ROPE_THETA = 10000.0
EPS = 1e-6
NEG_INF = -1e30
A_HEADS = 8
A_KV = 2
B_HEADS = 8
B_KV = 2
C_HEADS = 4
D_HEADS = 8
Q_LORA = 512
KV_LORA = 256
QK_NOPE = 64
QK_ROPE = 32
V_HEAD = 64
N_BRANCH = 4
BRANCH_W = 512
Q_SIZES = (A_HEADS * HD, B_HEADS * HD, 2 * C_HEADS * HD, Q_LORA)
KV_SIZES = (A_KV * HD, A_KV * HD, B_KV * HD, B_KV * HD, 2 * C_HEADS * HD, 2 * C_HEADS * HD, KV_LORA, QK_ROPE)
Q_COLS = A_HEADS * HD + B_HEADS * HD + 2 * C_HEADS * HD + Q_LORA
KV_COLS = 2 * A_KV * HD + 2 * B_KV * HD + 4 * C_HEADS * HD + KV_LORA + QK_ROPE
GATE_COLS = N_BRANCH * D_MODEL
IN_COLS = Q_COLS + KV_COLS + GATE_COLS
D_FF = 2816
N_EXPERTS = 8
TOP_K = 2
EXPERT_FF = 3584
MOE_BLOCK = 128
N_DENSE = (DEPTH + 1) // 2
N_MOE = DEPTH // 2

kernel_name = 'hybrid_gated_branch_diffusion_block'


def _offsets(sizes):
    return [int(v) for v in np.cumsum(sizes)[:-1]]


def rms(x, g):
    xf = x.astype(jnp.float32)
    y = xf * lax.rsqrt(jnp.mean(xf * xf, axis=-1, keepdims=True) + EPS)
    return (y * g.astype(jnp.float32)).astype(x.dtype)


def adaln(x, g, shift, scale):
    return rms(x, g) * (1 + scale) + shift


def modulation(cond, w, b):
    m = jax.nn.silu(cond) @ w + b
    return jnp.split(m, 6, axis=-1)


def axial_rope(rows, rot_dim):
    t = jnp.arange(rows * GRID_W)
    n = rot_dim // 4
    inv = jnp.power(ROPE_THETA, -jnp.arange(n, dtype=jnp.float32) / n)
    ang = jnp.concatenate([(t // GRID_W).astype(jnp.float32)[:, None] * inv,
                           (t % GRID_W).astype(jnp.float32)[:, None] * inv], axis=-1)
    return jnp.cos(ang), jnp.sin(ang)


def apply_rope(x, rope):
    if rope is None:
        return x
    cos, sin = rope
    half = x.shape[-1] // 2
    c = cos[:, None, :].astype(x.dtype)
    s = sin[:, None, :].astype(x.dtype)
    x1, x2 = x[..., :half], x[..., half:]
    return jnp.concatenate([x1 * c - x2 * s, x1 * s + x2 * c], axis=-1)


def build_q(pq, P, rope_h, rope_r):
    B, L, _ = pq.shape
    qa, qb, qc, cq = jnp.split(pq, _offsets(Q_SIZES), axis=-1)
    qa = apply_rope(rms(qa.reshape(B, L, A_HEADS, HD), P['a_qn']), rope_h).reshape(B, L, A_KV, A_HEADS // A_KV, HD)
    qb = apply_rope(rms(qb.reshape(B, L, B_HEADS, HD), P['b_qn']), rope_h).reshape(B, L, B_KV, B_HEADS // B_KV, HD)
    qc = apply_rope(rms(qc.reshape(B, L, 2 * C_HEADS, HD), P['c_qn']), rope_h).reshape(B, L, C_HEADS, 2, HD)
    qd = (rms(cq, P['d_q_norm']) @ P['d_w_uq']).reshape(B, L, D_HEADS, QK_NOPE + QK_ROPE)
    qd = jnp.concatenate([rms(qd[..., :QK_NOPE], P['d_qn_nope']),
                          apply_rope(rms(qd[..., QK_NOPE:], P['d_qn_rope']), rope_r)], axis=-1)
    return {'a': qa, 'b': qb, 'c1': qc[:, :, :, 0], 'c2': qc[:, :, :, 1], 'd': qd[:, :, :, None, :]}


def build_kv(pkv, P, rope_h, rope_r):
    B, S, _ = pkv.shape
    ka, va, kb, vb, kc, vc, ckv, kr = jnp.split(pkv, _offsets(KV_SIZES), axis=-1)
    ka = apply_rope(rms(ka.reshape(B, S, A_KV, HD), P['a_kn']), rope_h)
    kb = apply_rope(rms(kb.reshape(B, S, B_KV, HD), P['b_kn']), rope_h)
    kc = apply_rope(rms(kc.reshape(B, S, 2 * C_HEADS, HD), P['c_kn']), rope_h).reshape(B, S, C_HEADS, 2, HD)
    kvd = (rms(ckv, P['d_kv_norm']) @ P['d_w_ukv']).reshape(B, S, D_HEADS, QK_NOPE + V_HEAD)
    kd_rope = apply_rope(rms(kr.reshape(B, S, 1, QK_ROPE), P['d_kn_rope']), rope_r)
    kd = jnp.concatenate([rms(kvd[..., :QK_NOPE], P['d_kn_nope']),
                          jnp.broadcast_to(kd_rope, (B, S, D_HEADS, QK_ROPE))], axis=-1)
    return {'ka': ka, 'va': va.reshape(B, S, A_KV, HD),
            'kb': kb, 'vb': vb.reshape(B, S, B_KV, HD),
            'kc1': kc[:, :, :, 0], 'kc2': kc[:, :, :, 1], 'vc': vc.reshape(B, S, C_HEADS, 2 * HD),
            'kd': kd, 'vd': kvd[..., QK_NOPE:]}


def dense_attn(q, k, v, scale, sink=None):
    B, L, Hk, G, d = q.shape
    nb = L // BLOCK
    qb = jnp.moveaxis(q.reshape(B, nb, BLOCK, Hk, G, d), 1, 0)

    def one(qblk):
        s = jnp.einsum('bqhgd,bkhd->bhgqk', qblk, k, preferred_element_type=jnp.float32) * scale
        if sink is not None:
            s_sink = jnp.broadcast_to(sink.astype(jnp.float32).reshape(1, Hk, G, 1, 1), s.shape[:-1] + (1,))
            p = jax.nn.softmax(jnp.concatenate([s, s_sink], axis=-1), axis=-1)[..., :-1]
        else:
            p = jax.nn.softmax(s, axis=-1)
        return jnp.einsum('bhgqk,bkhd->bqhgd', p.astype(v.dtype), v)

    out = lax.map(one, qb)
    return jnp.moveaxis(out, 0, 1).reshape(B, L, Hk, G, v.shape[-1])


def window_attn(q, k, v, k_ctx, v_ctx, sink, scale):
    B, L, Hk, G, d = q.shape
    nb = L // BLOCK
    pad = ((0, 0), (BLOCK, BLOCK), (0, 0), (0, 0))
    kp = jnp.pad(k, pad).reshape(B, nb + 2, BLOCK, Hk, d)
    vp = jnp.pad(v, pad).reshape(B, nb + 2, BLOCK, Hk, v.shape[-1])
    kw = jnp.concatenate([kp[:, :-2], kp[:, 1:-1], kp[:, 2:]], axis=2)
    vw = jnp.concatenate([vp[:, :-2], vp[:, 1:-1], vp[:, 2:]], axis=2)
    qb = q.reshape(B, nb, BLOCK, Hk, G, d)
    s_loc = jnp.einsum('bnqhgd,bnkhd->bnhgqk', qb, kw, preferred_element_type=jnp.float32) * scale
    qi = jnp.arange(BLOCK)[:, None]
    kj = jnp.arange(3 * BLOCK)[None, :]
    kpos = (jnp.arange(nb)[:, None, None] - 1) * BLOCK + kj
    valid = (kj >= qi + BLOCK - WINDOW) & (kj <= qi + BLOCK + WINDOW) & (kpos >= 0) & (kpos < L)
    s_loc = jnp.where(valid[None, :, None, None], s_loc, NEG_INF)
    s_ctx = jnp.einsum('bnqhgd,bkhd->bnhgqk', qb, k_ctx, preferred_element_type=jnp.float32) * scale
    s_sink = jnp.broadcast_to(sink.astype(jnp.float32).reshape(1, 1, Hk, G, 1, 1), s_loc.shape[:-1] + (1,))
    p = jax.nn.softmax(jnp.concatenate([s_loc, s_ctx, s_sink], axis=-1), axis=-1)
    nl = 3 * BLOCK
    nc = k_ctx.shape[1]
    out = (jnp.einsum('bnhgqk,bnkhd->bnqhgd', p[..., :nl].astype(v.dtype), vw)
           + jnp.einsum('bnhgqk,bkhd->bnqhgd', p[..., nl:nl + nc].astype(v.dtype), v_ctx))
    return out.reshape(B, L, Hk, G, v.shape[-1])


def diff_attn(q1, q2, k1, k2, v, lam, scale):
    B, L, H, d = q1.shape
    nb = L // BLOCK
    qs = jnp.moveaxis(jnp.stack([q1, q2], axis=2).reshape(B, nb, BLOCK, 2, H, d), 1, 0)

    def one(qblk):
        s1 = jnp.einsum('bqhd,bkhd->bhqk', qblk[:, :, 0], k1, preferred_element_type=jnp.float32) * scale
        s2 = jnp.einsum('bqhd,bkhd->bhqk', qblk[:, :, 1], k2, preferred_element_type=jnp.float32) * scale
        p = jax.nn.softmax(s1, axis=-1) - lam * jax.nn.softmax(s2, axis=-1)
        return jnp.einsum('bhqk,bkhe->bqhe', p.astype(v.dtype), v)

    out = lax.map(one, qs)
    return jnp.moveaxis(out, 0, 1).reshape(B, L, H, v.shape[-1])


def run_mixers(q, kv, kv_ctx, P, lam, lam_init):
    sc = 1.0 / math.sqrt(HD)
    if kv is None:
        keys = kv_ctx
        oa = dense_attn(q['a'], keys['ka'], keys['va'], sc, sink=P['a_sink'])
    else:
        oa = window_attn(q['a'], kv['ka'], kv['va'], kv_ctx['ka'], kv_ctx['va'], P['a_sink'], sc)
        keys = {n: jnp.concatenate([kv[n], kv_ctx[n]], axis=1)
                for n in ('kb', 'vb', 'kc1', 'kc2', 'vc', 'kd', 'vd')}
    ob = dense_attn(q['b'], keys['kb'], keys['vb'], sc)
    oc = diff_attn(q['c1'], q['c2'], keys['kc1'], keys['kc2'], keys['vc'], lam, sc)
    oc = rms(oc, P['c_subln']) * (1.0 - lam_init)
    od = dense_attn(q['d'], keys['kd'], keys['vd'], 1.0 / math.sqrt(QK_NOPE + QK_ROPE))
    B, L = oa.shape[0], oa.shape[1]
    return jnp.stack([oa.reshape(B, L, BRANCH_W), ob.reshape(B, L, BRANCH_W),
                      oc.reshape(B, L, BRANCH_W), od.reshape(B, L, BRANCH_W)], axis=2)


def merge(br, pg, P):
    y = jnp.einsum('blnw,nwd->blnd', br, P['w_br'])
    g = jax.nn.sigmoid(pg.reshape(y.shape))
    return jnp.sum(g * y, axis=2) @ P['w_out']


def swiglu(h, wg, wu, wd):
    return (jax.nn.silu(h @ wg) * (h @ wu)) @ wd


def moe_swiglu(h, w_router, w1, w3, w2):
    N, Dm = h.shape
    logits = (h @ w_router).astype(jnp.float32)
    top_v, top_i = lax.top_k(logits, TOP_K)
    wts = jax.nn.softmax(top_v, axis=-1)
    n_assign = N * TOP_K
    e_flat = top_i.reshape(-1)
    tok_flat = jnp.repeat(jnp.arange(N), TOP_K)
    order = jnp.argsort(e_flat)
    e_s, tok_s, w_s = e_flat[order], tok_flat[order], wts.reshape(-1)[order]
    counts = jnp.bincount(e_flat, length=N_EXPERTS)
    starts = jnp.cumsum(counts) - counts
    padded = (counts + MOE_BLOCK - 1) // MOE_BLOCK * MOE_BLOCK
    pstarts = jnp.cumsum(padded) - padded
    pends = pstarts + padded
    dest = pstarts[e_s] + (jnp.arange(n_assign) - starts[e_s])
    n_blocks = -(-n_assign // MOE_BLOCK) + N_EXPERTS
    xbuf = jnp.zeros((n_blocks * MOE_BLOCK, Dm), h.dtype).at[dest].set(h[tok_s])
    blk_e = jnp.minimum(jnp.searchsorted(pends, jnp.arange(n_blocks) * MOE_BLOCK, side='right'), N_EXPERTS - 1)

    def expert_block(args):
        xb, e = args
        return swiglu(xb, w1[e], w3[e], w2[e])

    ybuf = lax.map(expert_block, (xbuf.reshape(n_blocks, MOE_BLOCK, Dm), blk_e))
    y = ybuf.reshape(n_blocks * MOE_BLOCK, Dm)[dest] * w_s[:, None].astype(h.dtype)
    return jnp.zeros_like(h).at[tok_s].add(y)


def setup_inputs(seed: int = 0) -> dict:
    key = jax.random.key(seed)
    ks = jax.random.split(key, 40)
    cnt = [0]

    def nrm(shape, s):
        k = ks[cnt[0]]
        cnt[0] += 1
        return jax.random.normal(k, shape, jnp.float32) * s

    def gain(shape):
        return 1.0 + nrm(shape, 0.02)

    D = D_MODEL
    return {
        'x': nrm((BATCH, SEQ, D), 1.0),
        'c': nrm((BATCH, D), 1.0),
        'ctx': nrm((BATCH, CTX_LEN, D), 1.0),
        'c_ctx': nrm((D,), 1.0),
        'w_mod': nrm((DEPTH, D, 6 * D), 0.5 * D ** -0.5),
        'b_mod': nrm((DEPTH, 6 * D), 0.02),
        'mix_norm': gain((DEPTH, D)),
        'ffn_norm': gain((DEPTH, D)),
        'w_in': nrm((DEPTH, D, IN_COLS), D ** -0.5),
        'a_qn': gain((DEPTH, HD)),
        'a_kn': gain((DEPTH, HD)),
        'a_sink': nrm((DEPTH, A_HEADS), 0.5),
        'b_qn': gain((DEPTH, HD)),
        'b_kn': gain((DEPTH, HD)),
        'c_qn': gain((DEPTH, HD)),
        'c_kn': gain((DEPTH, HD)),
        'c_lq1': nrm((DEPTH, HD), 0.1),
        'c_lk1': nrm((DEPTH, HD), 0.1),
        'c_lq2': nrm((DEPTH, HD), 0.1),
        'c_lk2': nrm((DEPTH, HD), 0.1),
        'c_subln': gain((DEPTH, 2 * HD)),
        'd_q_norm': gain((DEPTH, Q_LORA)),
        'd_kv_norm': gain((DEPTH, KV_LORA)),
        'd_w_uq': nrm((DEPTH, Q_LORA, D_HEADS * (QK_NOPE + QK_ROPE)), Q_LORA ** -0.5),
        'd_w_ukv': nrm((DEPTH, KV_LORA, D_HEADS * (QK_NOPE + V_HEAD)), KV_LORA ** -0.5),
        'd_qn_nope': gain((DEPTH, QK_NOPE)),
        'd_kn_nope': gain((DEPTH, QK_NOPE)),
        'd_qn_rope': gain((DEPTH, QK_ROPE)),
        'd_kn_rope': gain((DEPTH, QK_ROPE)),
        'w_br': nrm((DEPTH, N_BRANCH, BRANCH_W, D), BRANCH_W ** -0.5),
        'w_out': nrm((DEPTH, D, D), D ** -0.5),
        'ff_w_gate': nrm((N_DENSE, D, D_FF), D ** -0.5),
        'ff_w_up': nrm((N_DENSE, D, D_FF), D ** -0.5),
        'ff_w_down': nrm((N_DENSE, D_FF, D), D_FF ** -0.5),
        'moe_router': nrm((N_MOE, D, N_EXPERTS), D ** -0.5),
        'moe_w1': nrm((N_MOE, N_EXPERTS, D, EXPERT_FF), D ** -0.5),
        'moe_w3': nrm((N_MOE, N_EXPERTS, D, EXPERT_FF), D ** -0.5),
        'moe_w2': nrm((N_MOE, N_EXPERTS, EXPERT_FF, D), EXPERT_FF ** -0.5),
    }


def reference(x, c, ctx, c_ctx, w_mod, b_mod, mix_norm, ffn_norm, w_in,
              a_qn, a_kn, a_sink, b_qn, b_kn, c_qn, c_kn, c_lq1, c_lk1, c_lq2, c_lk2, c_subln,
              d_q_norm, d_kv_norm, d_w_uq, d_w_ukv, d_qn_nope, d_kn_nope, d_qn_rope, d_kn_rope,
              w_br, w_out, ff_w_gate, ff_w_up, ff_w_down, moe_router, moe_w1, moe_w3, moe_w2):
    B, L, _ = x.shape
    rows = L // GRID_W
    rope_h = axial_rope(rows, HD)
    rope_r = axial_rope(rows, QK_ROPE)
    lat, cx = x, ctx
    kv_lo, kv_hi = Q_COLS, Q_COLS + KV_COLS
    for l in range(DEPTH):
        last = l == DEPTH - 1
        P = {'a_qn': a_qn[l], 'a_kn': a_kn[l], 'a_sink': a_sink[l], 'b_qn': b_qn[l], 'b_kn': b_kn[l],
             'c_qn': c_qn[l], 'c_kn': c_kn[l], 'c_subln': c_subln[l],
             'd_q_norm': d_q_norm[l], 'd_kv_norm': d_kv_norm[l], 'd_w_uq': d_w_uq[l], 'd_w_ukv': d_w_ukv[l],
             'd_qn_nope': d_qn_nope[l], 'd_kn_nope': d_kn_nope[l], 'd_qn_rope': d_qn_rope[l],
             'd_kn_rope': d_kn_rope[l], 'w_br': w_br[l], 'w_out': w_out[l]}
        lam_init = 0.8 - 0.6 * math.exp(-0.3 * l)
        lam = (jnp.exp(jnp.sum(c_lq1[l].astype(jnp.float32) * c_lk1[l].astype(jnp.float32)))
               - jnp.exp(jnp.sum(c_lq2[l].astype(jnp.float32) * c_lk2[l].astype(jnp.float32))) + lam_init)
        m_lat = [t[:, None, :] for t in modulation(c, w_mod[l], b_mod[l])]
        m_ctx = modulation(c_ctx, w_mod[l], b_mod[l])
        p_lat = adaln(lat, mix_norm[l], m_lat[0], m_lat[1]) @ w_in[l]
        h_ctx = adaln(cx, mix_norm[l], m_ctx[0], m_ctx[1])
        if last:
            p_ctx_kv = h_ctx @ w_in[l][:, kv_lo:kv_hi]
        else:
            p_ctx = h_ctx @ w_in[l]
            p_ctx_kv = p_ctx[..., kv_lo:kv_hi]
        kv_ctx = build_kv(p_ctx_kv, P, None, None)
        kv_lat = build_kv(p_lat[..., kv_lo:kv_hi], P, rope_h, rope_r)
        q_lat = build_q(p_lat[..., :kv_lo], P, rope_h, rope_r)
        br_lat = run_mixers(q_lat, kv_lat, kv_ctx, P, lam, lam_init)
        lat = lat + m_lat[2] * merge(br_lat, p_lat[..., kv_hi:], P)
        if not last:
            q_ctx = build_q(p_ctx[..., :kv_lo], P, None, None)
            br_ctx = run_mixers(q_ctx, None, kv_ctx, P, lam, lam_init)
            cx = cx + m_ctx[2] * merge(br_ctx, p_ctx[..., kv_hi:], P)
        i = l // 2
        if l % 2 == 0:
            ffn = lambda h: swiglu(h, ff_w_gate[i], ff_w_up[i], ff_w_down[i])
        else:
            ffn = lambda h: moe_swiglu(h.reshape(-1, h.shape[-1]), moe_router[i], moe_w1[i],
                                       moe_w3[i], moe_w2[i]).reshape(h.shape)
        lat = lat + m_lat[5] * ffn(adaln(lat, ffn_norm[l], m_lat[3], m_lat[4]))
        if not last:
            cx = cx + m_ctx[5] * ffn(adaln(cx, ffn_norm[l], m_ctx[3], m_ctx[4]))
    return lat
```

```python
import functools
import math

import numpy as np
import jax
import jax.numpy as jnp
from jax import lax
from jax.experimental import pallas as pl
from jax.experimental.pallas import tpu as pltpu

F32 = jnp.float32
BF16 = jnp.bfloat16

D_MODEL = 1024
DEPTH = 2
CTX_LEN = 256
GRID_W = 64
HD = 64
BLOCK = 128
WINDOW = 128
ROPE_THETA = 10000.0
EPS = 1e-6
NEG_INF = -1e30
A_HEADS = 8
C_HEADS = 4
D_HEADS = 8
Q_LORA = 512
KV_LORA = 256
QK_NOPE = 64
QK_ROPE = 32
V_HEAD = 64
BRANCH_W = 512
D_FF = 2816
N_EXPERTS = 8
TOP_K = 2
EXPERT_FF = 3584

LANES = 128
VMEM_LIMIT = 56 * 1024 * 1024

Q_COLS = 2048
KV_PAD_COLS = 1920
GATE_COLS = 4 * D_MODEL
W_COLS = Q_COLS + KV_PAD_COLS + GATE_COLS
HEAD_PERM = (0, 4, 1, 5, 2, 6, 3, 7)

TM = 256
KV_ROWS = 2304
MOE_TM = 512
MOE_TF = 512
FFN_TM = 512
FFN_TF = 256


def _cparams(sem):
    return pltpu.CompilerParams(dimension_semantics=sem, vmem_limit_bytes=VMEM_LIMIT)


def _resident(shape):
    nd = len(shape)
    return pl.BlockSpec(shape, lambda *_: (0,) * nd, pipeline_mode=pl.Buffered(1))


def _sigmoid(x):
    return 1.0 / (1.0 + jnp.exp(-x))


def _dot(a, b):
    return jnp.dot(a, b, preferred_element_type=F32)


def _dot_nt(a, b):
    return lax.dot_general(a, b, (((1,), (1,)), ((), ())), preferred_element_type=F32)


def _mod_kernel(c_ref, w_ref, b_ref, o_ref):
    cond = c_ref[...]
    s = cond * _sigmoid(cond)
    o_ref[0] = jnp.dot(s, w_ref[0], precision=lax.Precision.HIGHEST,
                       preferred_element_type=F32) + b_ref[0]


def _modulation(cond, w_mod, b_mod):
    depth = w_mod.shape[0]
    nct = 6 * D_MODEL // 1024
    return pl.pallas_call(
        _mod_kernel,
        out_shape=jax.ShapeDtypeStruct((depth, 16, 6 * D_MODEL), F32),
        grid=(depth, nct),
        in_specs=[pl.BlockSpec((16, D_MODEL), lambda l, j: (0, 0)),
                  pl.BlockSpec((1, D_MODEL, 1024), lambda l, j: (l, 0, j)),
                  pl.BlockSpec((1, 1, 1024), lambda l, j: (l, 0, j))],
        out_specs=pl.BlockSpec((1, 16, 1024), lambda l, j: (l, 0, j)),
        compiler_params=_cparams(("arbitrary", "arbitrary")),
        name="modulation",
    )(cond, w_mod, b_mod.reshape(depth, 1, 6 * D_MODEL))


def _adaln(x, g, shift, scale):
    ms = jnp.mean(x * x, axis=-1, keepdims=True)
    y = x * lax.rsqrt(ms + EPS) * g
    return y * (1.0 + scale) + shift


def _group_mean_sq(r, mavg):
    sq = (r * r).astype(BF16)
    w = r.shape[1]
    parts = []
    for c in range(0, w, 2 * LANES):
        cw = min(2 * LANES, w - c)
        parts.append(_dot(sq[:, c:c + cw], mavg[:cw, :cw]))
    return parts[0] if len(parts) == 1 else jnp.concatenate(parts, axis=1)


def _rope_slab(y, cos, sin, half):
    lane = lax.broadcasted_iota(jnp.int32, y.shape, 1)
    first = (lane % (2 * half)) < half
    sw = jnp.where(first, pltpu.roll(y, LANES - half, 1), pltpu.roll(y, half, 1))
    return y * cos + sw * sin


def _norm_rope(r, mavg, gain, cos, sin, half):
    ms = _group_mean_sq(r, mavg)
    outs = []
    for c in range(0, r.shape[1], LANES):
        y = r[:, c:c + LANES] * lax.rsqrt(ms[:, c:c + LANES] + EPS) * gain
        if cos is not None:
            y = _rope_slab(y, cos, sin, half)
        outs.append(y)
    return outs[0] if len(outs) == 1 else jnp.concatenate(outs, axis=1)


def _proj_kernel(with_q, x_ref, mod_ref, vec_ref, gains_ref, rope_ref, mavg_ref,
                 w_ref, wuq_ref, wukv_ref, *refs):
    if with_q:
        (qa_ref, qb_ref, qc_ref, qd_ref, g_ref) = refs[-5:]
        kv_refs = refs[-13:-5]
        kvo, go = Q_COLS, Q_COLS + KV_PAD_COLS
    else:
        kv_refs = refs[-8:]
        kvo = 0
    ka_ref, va_ref, kb_ref, vb_ref, kc_ref, vc_ref, kd_ref, vd_ref = kv_refs

    m = mod_ref[0]
    hb = _adaln(x_ref[...], vec_ref[0:1, :], m[0:1], m[1:2]).astype(BF16)

    def mm(lo, hi):
        return _dot(hb, w_ref[:, lo:hi])

    m64 = mavg_ref[0]
    md = mavg_ref[1]
    c64, s64, cd, sd = rope_ref[0], rope_ref[1], rope_ref[2], rope_ref[3]
    gains = gains_ref[...]

    def gain(i):
        return gains[i:i + 1, :]

    if with_q:
        qa_ref[...] = _norm_rope(mm(0, 512), m64, gain(0), c64, s64, 32).astype(BF16)
        qb_ref[...] = _norm_rope(mm(512, 1024), m64, gain(1), c64, s64, 32).astype(BF16)
        qc_ref[...] = _norm_rope(mm(1024, 1536), m64, gain(2), c64, s64, 32).astype(BF16)
        cq = mm(1536, 2048)
        cqn = cq * lax.rsqrt(jnp.mean(cq * cq, axis=-1, keepdims=True) + EPS) * vec_ref[1:2, 0:Q_LORA]
        qd = _dot(cqn.astype(BF16), wuq_ref[...])
        qd_ref[...] = _norm_rope(qd, md, gain(6), cd, sd, 16).astype(BF16)
        for n in range(4):
            gl = mm(go + n * D_MODEL, go + (n + 1) * D_MODEL)
            g_ref[:, n * D_MODEL:(n + 1) * D_MODEL] = _sigmoid(gl).astype(BF16)

    kvab = mm(kvo, kvo + 512)
    ka_ref[...] = _norm_rope(kvab[:, 0:128], m64, gain(3), c64, s64, 32).astype(BF16)
    va_ref[...] = kvab[:, 128:256].astype(BF16)
    kb_ref[...] = _norm_rope(kvab[:, 256:384], m64, gain(4), c64, s64, 32).astype(BF16)
    vb_ref[...] = kvab[:, 384:512].astype(BF16)
    kc_ref[...] = _norm_rope(mm(kvo + 512, kvo + 1024), m64, gain(5), c64, s64, 32).astype(BF16)
    vc_ref[...] = mm(kvo + 1024, kvo + 1536).astype(BF16)
    ckv = mm(kvo + 1536, kvo + 1792)
    ckvn = (ckv * lax.rsqrt(jnp.mean(ckv * ckv, axis=-1, keepdims=True) + EPS)
            * vec_ref[1:2, Q_LORA:Q_LORA + KV_LORA])
    kvd = _dot(ckvn.astype(BF16), wukv_ref[...])
    kr = _norm_rope(mm(kvo + 1792, kvo + 1920), md, gain(8), cd, sd, 16)
    kn = _norm_rope(kvd[:, 0:D_HEADS * LANES], md, gain(7), None, None, 0)
    kd_ref[...] = (kn + jnp.concatenate([kr] * D_HEADS, axis=1)).astype(BF16)
    vd_ref[...] = kvd[:, D_HEADS * LANES:].astype(BF16)


_KV_WIDTHS = (128, 128, 128, 128, 512, 512, D_HEADS * LANES, D_HEADS * V_HEAD)


def _project(x, mod, vec, gains, rope, mavg, w, wuq, wukv, *, seq, row_blocks, row_off,
             with_q, kv_in=None):
    n = x.shape[0]
    nb = n // seq
    tps = seq // TM
    grid = (n // TM,)

    def tok(i):
        return (i, 0)

    def kvrow(i):
        return ((i // tps) * row_blocks + row_off + i % tps, 0)

    in_specs = [
        pl.BlockSpec((TM, D_MODEL), tok),
        pl.BlockSpec((1, 6, D_MODEL), lambda i: (i // tps, 0, 0)),
        _resident(vec.shape),
        _resident(gains.shape),
        pl.BlockSpec((4, TM, LANES), lambda i: (0, i % tps, 0)),
        _resident(mavg.shape),
        _resident(w.shape),
        _resident(wuq.shape),
        _resident(wukv.shape),
    ]
    args = [x, mod, vec, gains, rope, mavg, w, wuq, wukv]
    kv_shapes = [jax.ShapeDtypeStruct((nb * KV_ROWS, wd), BF16) for wd in _KV_WIDTHS]
    kv_specs = [pl.BlockSpec((TM, wd), kvrow) for wd in _KV_WIDTHS]
    aliases = {}
    if kv_in is not None:
        for j, a in enumerate(kv_in):
            aliases[len(args)] = j
            args.append(a)
            in_specs.append(pl.BlockSpec(memory_space=pl.ANY))
    out_shapes = list(kv_shapes)
    out_specs = list(kv_specs)
    if with_q:
        for wd in (512, 512, 512, D_HEADS * LANES, GATE_COLS):
            out_shapes.append(jax.ShapeDtypeStruct((n, wd), BF16))
            out_specs.append(pl.BlockSpec((TM, wd), tok))
    return pl.pallas_call(
        functools.partial(_proj_kernel, with_q),
        out_shape=out_shapes, grid=grid, in_specs=in_specs, out_specs=out_specs,
        input_output_aliases=aliases,
        compiler_params=_cparams(("arbitrary",)),
        name="project_q" if with_q else "project_kv",
    )(*args)


def _softmax_pv(s, v, extra_logit=None):
    m = jnp.max(s, axis=-1, keepdims=True)
    if extra_logit is not None:
        m = jnp.maximum(m, extra_logit)
    p = jnp.exp(s - m)
    l = jnp.sum(p, axis=-1, keepdims=True)
    if extra_logit is not None:
        l = l + jnp.exp(extra_logit - m)
    return _dot(p.astype(BF16), v) / l


def _split_heads(qs):
    lane = lax.broadcasted_iota(jnp.int32, qs.shape, 1)
    zero = jnp.zeros_like(qs)
    return jnp.concatenate([jnp.where(lane < HD, qs, zero), jnp.where(lane >= HD, qs, zero)], axis=0)


def _merge_halves(o, tq):
    lane = lax.broadcasted_iota(jnp.int32, (tq, LANES), 1)
    return jnp.where(lane < HD, o[:tq], o[tq:])


def _attn_b_kernel(q_ref, k_ref, v_ref, o_ref):
    tq = q_ref.shape[0]
    k = k_ref[...]
    v = v_ref[...]
    for j in range(4):
        qq = _split_heads(q_ref[:, j * LANES:(j + 1) * LANES])
        o = _softmax_pv(_dot_nt(qq, k), v)
        o_ref[:, j * LANES:(j + 1) * LANES] = _merge_halves(o, tq).astype(BF16)


def _attn_a_kernel(local, seq, sink_ref, q_ref, k_ref, v_ref, o_ref):
    tq = q_ref.shape[0]
    nkeys = k_ref.shape[0]
    if local:
        n = pl.program_id(1)
        q0 = n * tq
        start = pl.multiple_of(jnp.clip(q0 - BLOCK, 0, seq - 3 * BLOCK), BLOCK)
        k = jnp.concatenate([k_ref[pl.ds(start, 3 * BLOCK), :], k_ref[seq:nkeys, :]], axis=0)
        v = jnp.concatenate([v_ref[pl.ds(start, 3 * BLOCK), :], v_ref[seq:nkeys, :]], axis=0)
        nk = k.shape[0]
        row = lax.broadcasted_iota(jnp.int32, (2 * tq, nk), 0)
        col = lax.broadcasted_iota(jnp.int32, (2 * tq, nk), 1)
        qpos = q0 + jnp.where(row >= tq, row - tq, row)
        dist = (start + col) - qpos
        valid = (col >= 3 * BLOCK) | ((dist >= -WINDOW) & (dist <= WINDOW))
    else:
        k = k_ref[...]
        v = v_ref[...]
    rows = lax.broadcasted_iota(jnp.int32, (2 * tq, 1), 0)
    for j in range(4):
        qq = _split_heads(q_ref[:, j * LANES:(j + 1) * LANES])
        s = _dot_nt(qq, k)
        if local:
            s = jnp.where(valid, s, NEG_INF)
        sink = jnp.where(rows < tq, sink_ref[2 * j], sink_ref[2 * j + 1])
        o = _softmax_pv(s, v, extra_logit=sink)
        o_ref[:, j * LANES:(j + 1) * LANES] = _merge_halves(o, tq).astype(BF16)


def _attn_c_kernel(lam_init, lq1_ref, lk1_ref, lq2_ref, lk2_ref, subln_ref, q_ref, k_ref, v_ref, o_ref):
    tq = q_ref.shape[0]
    lam = (jnp.exp(jnp.sum(lq1_ref[...] * lk1_ref[...], axis=-1, keepdims=True))
           - jnp.exp(jnp.sum(lq2_ref[...] * lk2_ref[...], axis=-1, keepdims=True)) + lam_init)
    for c in range(C_HEADS):
        qq = _split_heads(q_ref[:, c * LANES:(c + 1) * LANES])
        o = _softmax_pv(_dot_nt(qq, k_ref[:, c * LANES:(c + 1) * LANES]),
                        v_ref[:, c * LANES:(c + 1) * LANES])
        oc = o[:tq] - lam * o[tq:]
        oc = oc * lax.rsqrt(jnp.mean(oc * oc, axis=-1, keepdims=True) + EPS) * subln_ref[...]
        o_ref[:, c * LANES:(c + 1) * LANES] = (oc * (1.0 - lam_init)).astype(BF16)


def _attn_d_kernel(q_ref, k_ref, v_ref, o_ref):
    tq = q_ref.shape[0]
    for hp in range(D_HEADS // 2):
        v = v_ref[:, hp * LANES:(hp + 1) * LANES]
        halves = []
        for h in (2 * hp, 2 * hp + 1):
            s = _dot_nt(q_ref[:, h * LANES:(h + 1) * LANES], k_ref[:, h * LANES:(h + 1) * LANES])
            halves.append(_softmax_pv(s, v))
        o = jnp.concatenate(halves, axis=0)
        o_ref[:, hp * LANES:(hp + 1) * LANES] = _merge_halves(o, tq).astype(BF16)


def _attention(body, q, k, v, extra, extra_specs, *, seq_q, tq, ctx_only, name):
    n = q.shape[0]
    nb = n // seq_q
    nq = seq_q // tq
    if ctx_only:
        kblk = KV_ROWS // CTX_LEN
        kspec = lambda w: pl.BlockSpec((CTX_LEN, w), lambda b, i: (b * kblk + kblk - 1, 0))
    else:
        kspec = lambda w: pl.BlockSpec((KV_ROWS, w), lambda b, i: (b, 0))
    return pl.pallas_call(
        body,
        out_shape=jax.ShapeDtypeStruct((n, BRANCH_W), BF16),
        grid=(nb, nq),
        in_specs=list(extra_specs) + [
            pl.BlockSpec((tq, q.shape[1]), lambda b, i: (b * nq + i, 0)),
            kspec(k.shape[1]), kspec(v.shape[1])],
        out_specs=pl.BlockSpec((tq, BRANCH_W), lambda b, i: (b * nq + i, 0)),
        compiler_params=_cparams(("arbitrary", "arbitrary")),
        name=name,
    )(*extra, q, k, v)


def _mixers(qs, kvs, pl_params, *, seq_q, ctx_only, tag):
    qa, qb, qc, qd = qs
    ka, va, kb, vb, kc, vc, kd, vd = kvs
    seq_lat = KV_ROWS - CTX_LEN
    smem = pl.BlockSpec(memory_space=pltpu.SMEM)
    row = lambda w: pl.BlockSpec((1, w), lambda b, i: (0, 0))
    tq = min(256, seq_q)
    oa = _attention(functools.partial(_attn_a_kernel, not ctx_only, seq_lat), qa, ka, va,
                    [pl_params["sink"]], [smem], seq_q=seq_q, tq=BLOCK, ctx_only=ctx_only,
                    name="attn_a_" + tag)
    ob = _attention(_attn_b_kernel, qb, kb, vb, [], [], seq_q=seq_q, tq=tq, ctx_only=ctx_only,
                    name="attn_b_" + tag)
    oc = _attention(functools.partial(_attn_c_kernel, pl_params["lam_init"]), qc, kc, vc,
                    [pl_params["lq1"], pl_params["lk1"], pl_params["lq2"], pl_params["lk2"],
                     pl_params["subln"]],
                    [row(HD)] * 4 + [row(2 * HD)], seq_q=seq_q, tq=tq, ctx_only=ctx_only,
                    name="attn_c_" + tag)
    od = _attention(_attn_d_kernel, qd, kd, vd, [], [], seq_q=seq_q, tq=tq, ctx_only=ctx_only,
                    name="attn_d_" + tag)
    return oa, ob, oc, od


def _merge_kernel(moe, oa_ref, ob_ref, oc_ref, od_ref, g_ref, x_ref, mod_ref, norm_ref,
                  wbr_ref, wout_ref, *refs):
    if moe:
        wr_ref, xo_ref, h_ref, lg_ref = refs
    else:
        xo_ref, h_ref = refs
    m = mod_ref[0]
    y = None
    for n, o_ref in enumerate((oa_ref, ob_ref, oc_ref, od_ref)):
        yn = g_ref[:, n * D_MODEL:(n + 1) * D_MODEL].astype(F32) * _dot(o_ref[...], wbr_ref[n])
        y = yn if y is None else y + yn
    z = _dot(y.astype(BF16), wout_ref[...])
    xn = x_ref[...] + m[2:3] * z
    xo_ref[...] = xn
    h = _adaln(xn, norm_ref[...], m[3:4], m[4:5])
    if moe:
        h_ref[...] = h
        lg_ref[...] = lax.dot_general(wr_ref[...], h, (((1,), (1,)), ((), ())),
                                      precision=lax.Precision.HIGHEST, preferred_element_type=F32)
    else:
        h_ref[...] = h.astype(BF16)


def _merge(branches, g, x, mod, norm, wbr, wout, router_t, *, seq):
    n = x.shape[0]
    tps = seq // TM
    moe = router_t is not None
    tok = lambda i: (i, 0)
    in_specs = [pl.BlockSpec((TM, BRANCH_W), tok)] * 4 + [
        pl.BlockSpec((TM, GATE_COLS), tok),
        pl.BlockSpec((TM, D_MODEL), tok),
        pl.BlockSpec((1, 6, D_MODEL), lambda i: (i // tps, 0, 0)),
        _resident(norm.shape), _resident(wbr.shape), _resident(wout.shape)]
    args = list(branches) + [g, x, mod, norm, wbr, wout]
    out_shape = [jax.ShapeDtypeStruct((n, D_MODEL), F32),
                 jax.ShapeDtypeStruct((n, D_MODEL), F32 if moe else BF16)]
    out_specs = [pl.BlockSpec((TM, D_MODEL), tok), pl.BlockSpec((TM, D_MODEL), tok)]
    if moe:
        in_specs.append(_resident(router_t.shape))
        args.append(router_t)
        out_shape.append(jax.ShapeDtypeStruct((N_EXPERTS, n), F32))
        out_specs.append(pl.BlockSpec((N_EXPERTS, TM), lambda i: (0, i)))
    return pl.pallas_call(
        functools.partial(_merge_kernel, moe),
        out_shape=out_shape, grid=(n // TM,), in_specs=in_specs, out_specs=out_specs,
        compiler_params=_cparams(("arbitrary",)),
        name="merge_moe" if moe else "merge_dense",
    )(*args)


def _ffn_kernel(h_ref, x_ref, mod_ref, wg_ref, wu_ref, wd_ref, o_ref, hid_ref):
    h = h_ref[...]
    for c in range(0, D_FF, FFN_TF):
        a = _dot(h, wg_ref[:, c:c + FFN_TF])
        u = _dot(h, wu_ref[:, c:c + FFN_TF])
        hid_ref[:, c:c + FFN_TF] = (a * _sigmoid(a) * u).astype(BF16)
    m = mod_ref[0]
    o_ref[...] = x_ref[...] + m[5:6] * _dot(hid_ref[...], wd_ref[...])


def _dense_ffn(h, x, mod, wg, wu, wd, *, seq):
    n = x.shape[0]
    tm = min(FFN_TM, seq)
    tps = seq // tm
    tok = lambda i: (i, 0)
    return pl.pallas_call(
        _ffn_kernel,
        out_shape=jax.ShapeDtypeStruct((n, D_MODEL), F32),
        grid=(n // tm,),
        in_specs=[pl.BlockSpec((tm, D_MODEL), tok), pl.BlockSpec((tm, D_MODEL), tok),
                  pl.BlockSpec((1, 6, D_MODEL), lambda i: (i // tps, 0, 0)),
                  _resident(wg.shape), _resident(wu.shape), _resident(wd.shape)],
        out_specs=pl.BlockSpec((tm, D_MODEL), tok),
        scratch_shapes=[pltpu.VMEM((tm, D_FF), BF16)],
        compiler_params=_cparams(("arbitrary",)),
        name="dense_ffn",
    )(h, x, mod, wg, wu, wd)


def _route(logits_t):
    n = logits_t.shape[1]
    top_v, top_i = lax.top_k(logits_t.T, TOP_K)
    wts = jax.nn.softmax(top_v, axis=-1)
    member = (top_i[:, :, None] == jnp.arange(N_EXPERTS)[None, None, :]).any(axis=1).astype(jnp.int32)
    pos = jnp.cumsum(member, axis=0) - member
    counts = jnp.sum(member, axis=0)
    padded = (counts + MOE_TM - 1) // MOE_TM * MOE_TM
    pends = jnp.cumsum(padded)
    pstarts = pends - padded
    dest = (pstarts[top_i] + jnp.take_along_axis(pos, top_i, axis=1)).astype(jnp.int32)
    nblk_max = n * TOP_K // MOE_TM + N_EXPERTS
    nblk = (pends[-1] // MOE_TM).astype(jnp.int32)
    blk = jnp.minimum(jnp.arange(nblk_max), nblk - 1) * MOE_TM
    blk_e = jnp.minimum(jnp.searchsorted(pends, blk, side="right"), N_EXPERTS - 1).astype(jnp.int32)
    return dest, wts, blk_e, nblk.reshape(1), nblk_max


def _dispatch_kernel(tg, dest_ref, h_hbm, xin_hbm, xbuf_hbm, sem):
    del xin_hbm
    base = pl.program_id(0) * tg

    def copies(r):
        src = h_hbm.at[pl.ds(base + r, 1)]
        return (pltpu.make_async_copy(src, xbuf_hbm.at[pl.ds(dest_ref[0, 0, r], 1)], sem),
                pltpu.make_async_copy(src, xbuf_hbm.at[pl.ds(dest_ref[0, 0, tg + r], 1)], sem))

    def start(r, carry):
        for cp in copies(r):
            cp.start()
        return carry

    def wait(r, carry):
        for cp in copies(r):
            cp.wait()
        return carry

    lax.fori_loop(0, tg, start, 0)
    lax.fori_loop(0, tg, wait, 0)


def _dispatch(h, dest, nrows):
    n = h.shape[0]
    tg = 512
    nt = n // tg
    dest_t = jnp.transpose(dest.reshape(nt, tg, TOP_K), (0, 2, 1)).reshape(nt, 1, TOP_K * tg)
    xbuf0 = jnp.zeros((nrows, D_MODEL), F32)
    return pl.pallas_call(
        functools.partial(_dispatch_kernel, tg),
        out_shape=jax.ShapeDtypeStruct((nrows, D_MODEL), F32),
        grid=(nt,),
        in_specs=[pl.BlockSpec((1, 1, TOP_K * tg), lambda i: (i, 0, 0), memory_space=pltpu.SMEM),
                  pl.BlockSpec(memory_space=pl.ANY), pl.BlockSpec(memory_space=pl.ANY)],
        out_specs=pl.BlockSpec(memory_space=pl.ANY),
        scratch_shapes=[pltpu.SemaphoreType.DMA(())],
        input_output_aliases={2: 0},
        compiler_params=_cparams(("arbitrary",)),
        name="moe_dispatch",
    )(dest_t, h, xbuf0)


def _moe_kernel(nf, blk_e_ref, nblk_ref, x_ref, w1_ref, w3_ref, w2_ref, o_ref, xb_ref, acc_ref):
    del blk_e_ref
    i = pl.program_id(0)
    j = pl.program_id(1)

    @pl.when(i < nblk_ref[0])
    def _():
        @pl.when(j == 0)
        def _():
            xb_ref[...] = x_ref[...].astype(BF16)
            acc_ref[...] = jnp.zeros_like(acc_ref)

        xb = xb_ref[...]
        a = _dot(xb, w1_ref[0])
        u = _dot(xb, w3_ref[0])
        acc_ref[...] += _dot((a * _sigmoid(a) * u).astype(BF16), w2_ref[0])

        @pl.when(j == nf - 1)
        def _():
            o_ref[...] = acc_ref[...]

    @pl.when((i >= nblk_ref[0]) & (j == nf - 1))
    def _():
        o_ref[...] = jnp.zeros_like(o_ref)


def _moe_ffn(xbuf, blk_e, nblk, nblk_max, w1, w3, w2):
    nf = EXPERT_FF // MOE_TF

    def rows(i, j, be, nb):
        return (i, 0)

    def jj(i, j, nb):
        return jnp.where(i < nb[0], j, nf - 1)

    grid_spec = pltpu.PrefetchScalarGridSpec(
        num_scalar_prefetch=2, grid=(nblk_max, nf),
        in_specs=[pl.BlockSpec((MOE_TM, D_MODEL), rows),
                  pl.BlockSpec((1, D_MODEL, MOE_TF), lambda i, j, be, nb: (be[i], 0, jj(i, j, nb))),
                  pl.BlockSpec((1, D_MODEL, MOE_TF), lambda i, j, be, nb: (be[i], 0, jj(i, j, nb))),
                  pl.BlockSpec((1, MOE_TF, D_MODEL), lambda i, j, be, nb: (be[i], jj(i, j, nb), 0))],
        out_specs=pl.BlockSpec((MOE_TM, D_MODEL), rows),
        scratch_shapes=[pltpu.VMEM((MOE_TM, D_MODEL), BF16), pltpu.VMEM((MOE_TM, D_MODEL), F32)])
    return pl.pallas_call(
        functools.partial(_moe_kernel, nf),
        out_shape=jax.ShapeDtypeStruct(xbuf.shape, F32),
        grid_spec=grid_spec,
        compiler_params=_cparams(("arbitrary", "arbitrary")),
        name="moe_experts",
    )(blk_e, nblk, xbuf, w1, w3, w2)


def _combine_kernel(tc, dest_ref, wts_ref, x_ref, mod_ref, y_hbm, o_ref, yg_ref, sem):
    def copies(r):
        return (pltpu.make_async_copy(y_hbm.at[pl.ds(dest_ref[0, 0, r], 1)],
                                      yg_ref.at[0, pl.ds(r, 1)], sem),
                pltpu.make_async_copy(y_hbm.at[pl.ds(dest_ref[0, 0, tc + r], 1)],
                                      yg_ref.at[1, pl.ds(r, 1)], sem))

    def start(r, carry):
        for cp in copies(r):
            cp.start()
        return carry

    def wait(r, carry):
        for cp in copies(r):
            cp.wait()
        return carry

    lax.fori_loop(0, tc, start, 0)
    lax.fori_loop(0, tc, wait, 0)
    m = mod_ref[0]
    w = wts_ref[...]
    y = w[:, 0:1] * yg_ref[0] + w[:, 1:2] * yg_ref[1]
    o_ref[...] = x_ref[...] + m[5:6] * y


def _combine(ybuf, dest, wts, x, mod, *, seq):
    n = x.shape[0]
    tc = 256
    nt = n // tc
    tps = seq // tc
    dest_t = jnp.transpose(dest.reshape(nt, tc, TOP_K), (0, 2, 1)).reshape(nt, 1, TOP_K * tc)
    tok = lambda i: (i, 0)
    return pl.pallas_call(
        functools.partial(_combine_kernel, tc),
        out_shape=jax.ShapeDtypeStruct((n, D_MODEL), F32),
        grid=(nt,),
        in_specs=[pl.BlockSpec((1, 1, TOP_K * tc), lambda i: (i, 0, 0), memory_space=pltpu.SMEM),
                  pl.BlockSpec((tc, TOP_K), tok),
                  pl.BlockSpec((tc, D_MODEL), tok),
                  pl.BlockSpec((1, 6, D_MODEL), lambda i: (i // tps, 0, 0)),
                  pl.BlockSpec(memory_space=pl.ANY)],
        out_specs=pl.BlockSpec((tc, D_MODEL), tok),
        scratch_shapes=[pltpu.VMEM((TOP_K, tc, D_MODEL), F32), pltpu.SemaphoreType.DMA(())],
        compiler_params=_cparams(("arbitrary",)),
        name="moe_combine",
    )(dest_t, wts, x, mod, ybuf)


def _head_cols(perm):
    return np.concatenate([np.arange(p * HD, (p + 1) * HD) for p in perm])


def _prep_w_in(w):
    hp = _head_cols(HEAD_PERM)
    kv_lo = Q_COLS
    kr_lo = kv_lo + KV_PAD_COLS - LANES
    parts = [w[:, 0:512][:, hp], w[:, 512:1024][:, hp], w[:, 1024:Q_COLS],
             w[:, kv_lo:kr_lo],
             jnp.pad(w[:, kr_lo:kr_lo + QK_ROPE], ((0, 0), (QK_NOPE, LANES - QK_NOPE - QK_ROPE))),
             w[:, kr_lo + QK_ROPE:]]
    return jnp.concatenate(parts, axis=1).astype(BF16)


def _prep_layer(l, p):
    hp = _head_cols(HEAD_PERM)
    zeros = lambda k: jnp.zeros((k,), F32)
    sc = 1.0 / math.sqrt(HD)
    sc_d = 1.0 / math.sqrt(QK_NOPE + QK_ROPE)
    wuq = p["d_w_uq"][l].reshape(Q_LORA, D_HEADS, QK_NOPE + QK_ROPE)
    wuq = jnp.pad(wuq, ((0, 0), (0, 0), (0, LANES - QK_NOPE - QK_ROPE))).reshape(Q_LORA, D_HEADS * LANES)
    wukv = p["d_w_ukv"][l].reshape(KV_LORA, D_HEADS, QK_NOPE + V_HEAD)
    wuk = jnp.pad(wukv[:, :, :QK_NOPE], ((0, 0), (0, 0), (0, LANES - QK_NOPE))).reshape(KV_LORA, D_HEADS * LANES)
    wuv = wukv[:, :, QK_NOPE:].reshape(KV_LORA, D_HEADS * V_HEAD)
    gains = jnp.stack([
        jnp.tile(p["a_qn"][l], 2) * sc, jnp.tile(p["b_qn"][l], 2) * sc, jnp.tile(p["c_qn"][l], 2) * sc,
        jnp.tile(p["a_kn"][l], 2), jnp.tile(p["b_kn"][l], 2), jnp.tile(p["c_kn"][l], 2),
        jnp.concatenate([p["d_qn_nope"][l], p["d_qn_rope"][l], zeros(32)]) * sc_d,
        jnp.concatenate([p["d_kn_nope"][l], zeros(64)]),
        jnp.concatenate([zeros(64), p["d_kn_rope"][l], zeros(32)]),
    ] + [zeros(LANES)] * 7).astype(F32)
    vec = jnp.stack([p["mix_norm"][l],
                     jnp.concatenate([p["d_q_norm"][l], p["d_kv_norm"][l], zeros(D_MODEL - Q_LORA - KV_LORA)])]
                    + [zeros(D_MODEL)] * 6).astype(F32)
    wbr = p["w_br"][l]
    wbr = jnp.stack([wbr[0][hp], wbr[1][hp], wbr[2], wbr[3]]).astype(BF16)
    return {
        "w_in": _prep_w_in(p["w_in"][l]),
        "wuq": wuq.astype(BF16),
        "wukv": jnp.concatenate([wuk, wuv], axis=1).astype(BF16),
        "gains": gains, "vec": vec, "wbr": wbr, "wout": p["w_out"][l].astype(BF16),
        "ffn_norm": p["ffn_norm"][l].reshape(1, D_MODEL),
        "sink": p["a_sink"][l][np.asarray(HEAD_PERM)].astype(F32),
        "lq1": p["c_lq1"][l].reshape(1, HD), "lk1": p["c_lk1"][l].reshape(1, HD),
        "lq2": p["c_lq2"][l].reshape(1, HD), "lk2": p["c_lk2"][l].reshape(1, HD),
        "subln": p["c_subln"][l].reshape(1, 2 * HD),
        "lam_init": 0.8 - 0.6 * math.exp(-0.3 * l),
    }


def _group_avg_mats():
    m64 = np.kron(np.eye(4), np.full((HD, HD), 1.0 / HD))
    one = np.zeros((LANES, LANES))
    one[:QK_NOPE, :QK_NOPE] = 1.0 / QK_NOPE
    one[QK_NOPE:QK_NOPE + QK_ROPE, QK_NOPE:QK_NOPE + QK_ROPE] = 1.0 / QK_ROPE
    md = np.kron(np.eye(2), one)
    return jnp.asarray(np.stack([m64, md]), BF16)


def _rope_tables(seq, rotary):
    if not rotary:
        one = jnp.ones((seq, LANES), F32)
        zero = jnp.zeros((seq, LANES), F32)
        return jnp.stack([one, zero, one, zero])
    t = jnp.arange(seq)

    def angles(rot_dim):
        nfreq = rot_dim // 4
        inv = jnp.power(ROPE_THETA, -jnp.arange(nfreq, dtype=F32) / nfreq)
        return jnp.concatenate([(t // GRID_W).astype(F32)[:, None] * inv,
                                (t % GRID_W).astype(F32)[:, None] * inv], axis=-1)

    ah = angles(HD)
    c64 = jnp.tile(jnp.cos(ah), (1, 4))
    s64 = jnp.tile(jnp.concatenate([-jnp.sin(ah), jnp.sin(ah)], axis=1), (1, 2))
    ar = angles(QK_ROPE)
    one = jnp.ones((seq, 1), F32)
    cd = jnp.concatenate([one * jnp.ones((1, QK_NOPE), F32), jnp.cos(ar), jnp.cos(ar),
                          one * jnp.ones((1, LANES - QK_NOPE - QK_ROPE), F32)], axis=1)
    sd = jnp.concatenate([jnp.zeros((seq, QK_NOPE), F32), -jnp.sin(ar), jnp.sin(ar),
                          jnp.zeros((seq, LANES - QK_NOPE - QK_ROPE), F32)], axis=1)
    return jnp.stack([c64, s64, cd, sd])


def kernel(x, c, ctx, c_ctx, w_mod, b_mod, mix_norm, ffn_norm, w_in, a_qn, a_kn, a_sink, b_qn, b_kn, c_qn, c_kn, c_lq1, c_lk1, c_lq2, c_lk2, c_subln, d_q_norm, d_kv_norm, d_w_uq, d_w_ukv, d_qn_nope, d_kn_nope, d_qn_rope, d_kn_rope, w_br, w_out, ff_w_gate, ff_w_up, ff_w_down, moe_router, moe_w1, moe_w3, moe_w2):
    p = dict(mix_norm=mix_norm, ffn_norm=ffn_norm, w_in=w_in, a_qn=a_qn, a_kn=a_kn, a_sink=a_sink,
             b_qn=b_qn, b_kn=b_kn, c_qn=c_qn, c_kn=c_kn, c_lq1=c_lq1, c_lk1=c_lk1, c_lq2=c_lq2,
             c_lk2=c_lk2, c_subln=c_subln, d_q_norm=d_q_norm, d_kv_norm=d_kv_norm, d_w_uq=d_w_uq,
             d_w_ukv=d_w_ukv, d_qn_nope=d_qn_nope, d_kn_nope=d_kn_nope, d_qn_rope=d_qn_rope,
             d_kn_rope=d_kn_rope, w_br=w_br, w_out=w_out)
    nb, seq, _ = x.shape
    assert seq + CTX_LEN == KV_ROWS and ctx.shape[1] == CTX_LEN and nb <= 15
    depth = w_mod.shape[0]
    lat = x.reshape(nb * seq, D_MODEL)
    cx = ctx.reshape(nb * CTX_LEN, D_MODEL)

    cond = jnp.zeros((16, D_MODEL), F32).at[:nb].set(c).at[nb].set(c_ctx)
    mod_all = _modulation(cond, w_mod, b_mod).reshape(depth, 16, 6, D_MODEL)
    rope_lat = _rope_tables(seq, True)
    rope_ctx = _rope_tables(CTX_LEN, False)
    mavg = _group_avg_mats()
    lat_blocks = seq // TM

    for l in range(depth):
        last = l == depth - 1
        lp = _prep_layer(l, p)
        mod_lat = mod_all[l, :nb]
        mod_ctx = jnp.broadcast_to(mod_all[l, nb:nb + 1], (nb, 6, D_MODEL))
        common = (lp["vec"], lp["gains"])
        wts = (lp["wuq"], lp["wukv"])
        kvs = [jnp.zeros((nb * KV_ROWS, wd), BF16) for wd in _KV_WIDTHS]
        if last:
            kvs = _project(cx, mod_ctx, *common, rope_ctx, mavg,
                           lp["w_in"][:, Q_COLS:Q_COLS + KV_PAD_COLS], *wts, seq=CTX_LEN,
                           row_blocks=KV_ROWS // TM, row_off=lat_blocks, with_q=False, kv_in=kvs)
        else:
            outs = _project(cx, mod_ctx, *common, rope_ctx, mavg, lp["w_in"], *wts, seq=CTX_LEN,
                            row_blocks=KV_ROWS // TM, row_off=lat_blocks, with_q=True, kv_in=kvs)
            kvs, q_ctx, g_ctx = outs[:8], outs[8:12], outs[12]
        outs = _project(lat, mod_lat, *common, rope_lat, mavg, lp["w_in"], *wts, seq=seq,
                        row_blocks=KV_ROWS // TM, row_off=0, with_q=True, kv_in=kvs)
        kvs, q_lat, g_lat = outs[:8], outs[8:12], outs[12]
        br_lat = _mixers(q_lat, kvs, lp, seq_q=seq, ctx_only=False, tag="lat")
        moe = l % 2 == 1
        i = l // 2
        router_t = moe_router[i].T if moe else None
        res = _merge(br_lat, g_lat, lat, mod_lat, lp["ffn_norm"], lp["wbr"], lp["wout"], router_t, seq=seq)
        if not last:
            br_ctx = _mixers(q_ctx, kvs, lp, seq_q=CTX_LEN, ctx_only=True, tag="ctx")
            res_ctx = _merge(br_ctx, g_ctx, cx, mod_ctx, lp["ffn_norm"], lp["wbr"], lp["wout"],
                             router_t, seq=CTX_LEN)
        if not moe:
            wg, wu, wd = (ff_w_gate[i].astype(BF16), ff_w_up[i].astype(BF16), ff_w_down[i].astype(BF16))
            lat = _dense_ffn(res[1], res[0], mod_lat, wg, wu, wd, seq=seq)
            if not last:
                cx = _dense_ffn(res_ctx[1], res_ctx[0], mod_ctx, wg, wu, wd, seq=CTX_LEN)
        else:
            w1, w3, w2 = moe_w1[i].astype(BF16), moe_w3[i].astype(BF16), moe_w2[i].astype(BF16)

            def routed(res_t, mod_t, seq_t):
                xr, hr, lg = res_t
                dest, wts_r, blk_e, nblk, nblk_max = _route(lg)
                xbuf = _dispatch(hr, dest, nblk_max * MOE_TM)
                ybuf = _moe_ffn(xbuf, blk_e, nblk, nblk_max, w1, w3, w2)
                return _combine(ybuf, dest, wts_r, xr, mod_t, seq=seq_t)

            lat = routed(res, mod_lat, seq)
            if not last:
                cx = routed(res_ctx, mod_ctx, CTX_LEN)
    return lat.reshape(nb, seq, D_MODEL)
```

```python
import functools
import math

import numpy as np
import jax
import jax.numpy as jnp
from jax import lax
from jax.experimental import pallas as pl
from jax.experimental.pallas import tpu as pltpu

F32 = jnp.float32
BF16 = jnp.bfloat16

D_MODEL = 1024
DEPTH = 2
CTX_LEN = 256
GRID_W = 64
HD = 64
BLOCK = 128
WINDOW = 128
ROPE_THETA = 10000.0
EPS = 1e-6
NEG_INF = -1e30
A_HEADS = 8
C_HEADS = 4
D_HEADS = 8
Q_LORA = 512
KV_LORA = 256
QK_NOPE = 64
QK_ROPE = 32
V_HEAD = 64
BRANCH_W = 512
D_FF = 2816
N_EXPERTS = 8
TOP_K = 2
EXPERT_FF = 3584

LANES = 128
VMEM_LIMIT = 56 * 1024 * 1024
LOG2E = math.log2(math.e)

Q_COLS = 2048
KV_PAD_COLS = 1920
GATE_COLS = 4 * D_MODEL
W_COLS = Q_COLS + KV_PAD_COLS + GATE_COLS
HEAD_PERM = (0, 4, 1, 5, 2, 6, 3, 7)

TM = 256
KV_ROWS = 2304
MOE_TM = 512
MOE_TF = 512
FFN_TM = 512
FFN_TF = 256


def _cparams(sem):
    return pltpu.CompilerParams(dimension_semantics=sem, vmem_limit_bytes=VMEM_LIMIT)


def _resident(shape):
    nd = len(shape)
    return pl.BlockSpec(shape, lambda *_: (0,) * nd, pipeline_mode=pl.Buffered(1))


def _sigmoid(x):
    return 1.0 / (1.0 + jnp.exp(-x))


def _dot(a, b):
    return jnp.dot(a, b, preferred_element_type=F32)


def _dot_nt(a, b):
    return lax.dot_general(a, b, (((1,), (1,)), ((), ())), preferred_element_type=F32)


def _mod_kernel(c_ref, w_ref, b_ref, o_ref):
    cond = c_ref[...]
    s = cond * _sigmoid(cond)
    o_ref[0] = jnp.dot(s, w_ref[0], precision=lax.Precision.HIGHEST,
                       preferred_element_type=F32) + b_ref[0]


def _modulation(cond, w_mod, b_mod):
    depth = w_mod.shape[0]
    nct = 6 * D_MODEL // 1024
    return pl.pallas_call(
        _mod_kernel,
        out_shape=jax.ShapeDtypeStruct((depth, 16, 6 * D_MODEL), F32),
        grid=(depth, nct),
        in_specs=[pl.BlockSpec((16, D_MODEL), lambda l, j: (0, 0)),
                  pl.BlockSpec((1, D_MODEL, 1024), lambda l, j: (l, 0, j)),
                  pl.BlockSpec((1, 1, 1024), lambda l, j: (l, 0, j))],
        out_specs=pl.BlockSpec((1, 16, 1024), lambda l, j: (l, 0, j)),
        compiler_params=_cparams(("arbitrary", "arbitrary")),
        name="modulation",
    )(cond, w_mod, b_mod.reshape(depth, 1, 6 * D_MODEL))


def _adaln(x, g, shift, scale):
    ms = jnp.mean(x * x, axis=-1, keepdims=True)
    y = x * lax.rsqrt(ms + EPS) * g
    return y * (1.0 + scale) + shift


def _group_mean_sq(r, mavg):
    sq = (r * r).astype(BF16)
    w = r.shape[1]
    parts = []
    for c in range(0, w, 2 * LANES):
        cw = min(2 * LANES, w - c)
        parts.append(_dot(sq[:, c:c + cw], mavg[:cw, :cw]))
    return parts[0] if len(parts) == 1 else jnp.concatenate(parts, axis=1)


def _rope_slab(y, cos, sin, half):
    lane = lax.broadcasted_iota(jnp.int32, y.shape, 1)
    first = (lane % (2 * half)) < half
    sw = jnp.where(first, pltpu.roll(y, LANES - half, 1), pltpu.roll(y, half, 1))
    return y * cos + sw * sin


def _norm_rope(r, mavg, gain, cos, sin, half):
    ms = _group_mean_sq(r, mavg)
    outs = []
    for c in range(0, r.shape[1], LANES):
        y = r[:, c:c + LANES] * lax.rsqrt(ms[:, c:c + LANES] + EPS) * gain
        if cos is not None:
            y = _rope_slab(y, cos, sin, half)
        outs.append(y)
    return outs[0] if len(outs) == 1 else jnp.concatenate(outs, axis=1)


def _proj_kernel(with_q, x_ref, mod_ref, vec_ref, gains_ref, rope_ref, mavg_ref,
                 w_ref, wuq_ref, wukv_ref, *refs):
    if with_q:
        (qa_ref, qb_ref, qc_ref, qd_ref, g_ref) = refs[-5:]
        kv_refs = refs[-13:-5]
        kvo, go = Q_COLS, Q_COLS + KV_PAD_COLS
    else:
        kv_refs = refs[-8:]
        kvo = 0
    ka_ref, va_ref, kb_ref, vb_ref, kc_ref, vc_ref, kd_ref, vd_ref = kv_refs

    m = mod_ref[0]
    hb = _adaln(x_ref[...], vec_ref[0:1, :], m[0:1], m[1:2]).astype(BF16)

    def mm(lo, hi):
        return _dot(hb, w_ref[:, lo:hi])

    m64 = mavg_ref[0]
    md = mavg_ref[1]
    c64, s64, cd, sd = rope_ref[0], rope_ref[1], rope_ref[2], rope_ref[3]
    gains = gains_ref[...]

    def gain(i):
        return gains[i:i + 1, :]

    if with_q:
        qa_ref[...] = _norm_rope(mm(0, 512), m64, gain(0), c64, s64, 32).astype(BF16)
        qb_ref[...] = _norm_rope(mm(512, 1024), m64, gain(1), c64, s64, 32).astype(BF16)
        qc_ref[...] = _norm_rope(mm(1024, 1536), m64, gain(2), c64, s64, 32).astype(BF16)
        cq = mm(1536, 2048)
        cqn = cq * lax.rsqrt(jnp.mean(cq * cq, axis=-1, keepdims=True) + EPS) * vec_ref[1:2, 0:Q_LORA]
        qd = _dot(cqn.astype(BF16), wuq_ref[...])
        qd_ref[...] = _norm_rope(qd, md, gain(6), cd, sd, 16).astype(BF16)
        for n in range(4):
            gl = mm(go + n * D_MODEL, go + (n + 1) * D_MODEL)
            g_ref[:, n * D_MODEL:(n + 1) * D_MODEL] = _sigmoid(gl).astype(BF16)

    kvab = mm(kvo, kvo + 512)
    ka_ref[...] = _norm_rope(kvab[:, 0:128], m64, gain(3), c64, s64, 32).astype(BF16)
    va_ref[...] = kvab[:, 128:256].astype(BF16)
    kb_ref[...] = _norm_rope(kvab[:, 256:384], m64, gain(4), c64, s64, 32).astype(BF16)
    vb_ref[...] = kvab[:, 384:512].astype(BF16)
    kc_ref[...] = _norm_rope(mm(kvo + 512, kvo + 1024), m64, gain(5), c64, s64, 32).astype(BF16)
    vc_ref[...] = mm(kvo + 1024, kvo + 1536).astype(BF16)
    ckv = mm(kvo + 1536, kvo + 1792)
    ckvn = (ckv * lax.rsqrt(jnp.mean(ckv * ckv, axis=-1, keepdims=True) + EPS)
            * vec_ref[1:2, Q_LORA:Q_LORA + KV_LORA])
    kvd = _dot(ckvn.astype(BF16), wukv_ref[...])
    kr = _norm_rope(mm(kvo + 1792, kvo + 1920), md, gain(8), cd, sd, 16)
    kn = _norm_rope(kvd[:, 0:D_HEADS * LANES], md, gain(7), None, None, 0)
    kd_ref[...] = (kn + jnp.concatenate([kr] * D_HEADS, axis=1)).astype(BF16)
    vd_ref[...] = kvd[:, D_HEADS * LANES:].astype(BF16)


_KV_WIDTHS = (128, 128, 128, 128, 512, 512, D_HEADS * LANES, D_HEADS * V_HEAD)


def _project(x, mod, vec, gains, rope, mavg, w, wuq, wukv, *, seq, row_blocks, row_off,
             with_q, kv_in=None):
    n = x.shape[0]
    nb = n // seq
    tps = seq // TM
    grid = (n // TM,)

    def tok(i):
        return (i, 0)

    def kvrow(i):
        return ((i // tps) * row_blocks + row_off + i % tps, 0)

    in_specs = [
        pl.BlockSpec((TM, D_MODEL), tok),
        pl.BlockSpec((1, 6, D_MODEL), lambda i: (i // tps, 0, 0)),
        _resident(vec.shape),
        _resident(gains.shape),
        pl.BlockSpec((4, TM, LANES), lambda i: (0, i % tps, 0)),
        _resident(mavg.shape),
        _resident(w.shape),
        _resident(wuq.shape),
        _resident(wukv.shape),
    ]
    args = [x, mod, vec, gains, rope, mavg, w, wuq, wukv]
    kv_shapes = [jax.ShapeDtypeStruct((nb * KV_ROWS, wd), BF16) for wd in _KV_WIDTHS]
    kv_specs = [pl.BlockSpec((TM, wd), kvrow) for wd in _KV_WIDTHS]
    aliases = {}
    if kv_in is not None:
        for j, a in enumerate(kv_in):
            aliases[len(args)] = j
            args.append(a)
            in_specs.append(pl.BlockSpec(memory_space=pl.ANY))
    out_shapes = list(kv_shapes)
    out_specs = list(kv_specs)
    if with_q:
        for wd in (512, 512, 512, D_HEADS * LANES, GATE_COLS):
            out_shapes.append(jax.ShapeDtypeStruct((n, wd), BF16))
            out_specs.append(pl.BlockSpec((TM, wd), tok))
    return pl.pallas_call(
        functools.partial(_proj_kernel, with_q),
        out_shape=out_shapes, grid=grid, in_specs=in_specs, out_specs=out_specs,
        input_output_aliases=aliases,
        compiler_params=_cparams(("arbitrary",)),
        name="project_q" if with_q else "project_kv",
    )(*args)


def _softmax_pv(s, v, extra_logit=None):
    m = jnp.max(s, axis=-1, keepdims=True)
    if extra_logit is not None:
        m = jnp.maximum(m, extra_logit)
    p = jnp.exp2(s - m)
    l = jnp.sum(p, axis=-1, keepdims=True)
    if extra_logit is not None:
        l = l + jnp.exp2(extra_logit - m)
    return _dot(p.astype(BF16), v) / l


def _split_heads(qs):
    lane = lax.broadcasted_iota(jnp.int32, qs.shape, 1)
    zero = jnp.zeros_like(qs)
    return jnp.concatenate([jnp.where(lane < HD, qs, zero), jnp.where(lane >= HD, qs, zero)], axis=0)


def _merge_halves(o, tq):
    lane = lax.broadcasted_iota(jnp.int32, (tq, LANES), 1)
    return jnp.where(lane < HD, o[:tq], o[tq:])


def _attn_b_kernel(q_ref, k_ref, v_ref, o_ref):
    tq = q_ref.shape[0]
    k = k_ref[...]
    v = v_ref[...]
    for j in range(4):
        qq = _split_heads(q_ref[:, j * LANES:(j + 1) * LANES])
        o = _softmax_pv(_dot_nt(qq, k), v)
        o_ref[:, j * LANES:(j + 1) * LANES] = _merge_halves(o, tq).astype(BF16)


def _attn_a_kernel(local, seq, sink_ref, q_ref, k_ref, v_ref, o_ref):
    tq = q_ref.shape[0]
    nkeys = k_ref.shape[0]
    if local:
        n = pl.program_id(1)
        q0 = n * tq
        start = pl.multiple_of(jnp.clip(q0 - BLOCK, 0, seq - 3 * BLOCK), BLOCK)
        k = jnp.concatenate([k_ref[pl.ds(start, 3 * BLOCK), :], k_ref[seq:nkeys, :]], axis=0)
        v = jnp.concatenate([v_ref[pl.ds(start, 3 * BLOCK), :], v_ref[seq:nkeys, :]], axis=0)
        nk = k.shape[0]
        row = lax.broadcasted_iota(jnp.int32, (2 * tq, nk), 0)
        col = lax.broadcasted_iota(jnp.int32, (2 * tq, nk), 1)
        qpos = q0 + jnp.where(row >= tq, row - tq, row)
        dist = (start + col) - qpos
        valid = (col >= 3 * BLOCK) | ((dist >= -WINDOW) & (dist <= WINDOW))
    else:
        k = k_ref[...]
        v = v_ref[...]
    rows = lax.broadcasted_iota(jnp.int32, (2 * tq, 1), 0)
    for j in range(4):
        qq = _split_heads(q_ref[:, j * LANES:(j + 1) * LANES])
        s = _dot_nt(qq, k)
        if local:
            s = jnp.where(valid, s, NEG_INF)
        sink = jnp.where(rows < tq, sink_ref[2 * j], sink_ref[2 * j + 1]) * LOG2E
        o = _softmax_pv(s, v, extra_logit=sink)
        o_ref[:, j * LANES:(j + 1) * LANES] = _merge_halves(o, tq).astype(BF16)


def _attn_c_kernel(lam_init, lq1_ref, lk1_ref, lq2_ref, lk2_ref, subln_ref, q_ref, k_ref, v_ref, o_ref):
    tq = q_ref.shape[0]
    lam = (jnp.exp(jnp.sum(lq1_ref[...] * lk1_ref[...], axis=-1, keepdims=True))
           - jnp.exp(jnp.sum(lq2_ref[...] * lk2_ref[...], axis=-1, keepdims=True)) + lam_init)
    for c in range(C_HEADS):
        qq = _split_heads(q_ref[:, c * LANES:(c + 1) * LANES])
        o = _softmax_pv(_dot_nt(qq, k_ref[:, c * LANES:(c + 1) * LANES]),
                        v_ref[:, c * LANES:(c + 1) * LANES])
        oc = o[:tq] - lam * o[tq:]
        oc = oc * lax.rsqrt(jnp.mean(oc * oc, axis=-1, keepdims=True) + EPS) * subln_ref[...]
        o_ref[:, c * LANES:(c + 1) * LANES] = (oc * (1.0 - lam_init)).astype(BF16)


def _attn_d_kernel(q_ref, k_ref, v_ref, o_ref):
    tq = q_ref.shape[0]
    for hp in range(D_HEADS // 2):
        v = v_ref[:, hp * LANES:(hp + 1) * LANES]
        halves = []
        for h in (2 * hp, 2 * hp + 1):
            s = _dot_nt(q_ref[:, h * LANES:(h + 1) * LANES], k_ref[:, h * LANES:(h + 1) * LANES])
            halves.append(_softmax_pv(s, v))
        o = jnp.concatenate(halves, axis=0)
        o_ref[:, hp * LANES:(hp + 1) * LANES] = _merge_halves(o, tq).astype(BF16)


def _attention(body, q, k, v, extra, extra_specs, *, seq_q, tq, ctx_only, name):
    n = q.shape[0]
    nb = n // seq_q
    nq = seq_q // tq
    if ctx_only:
        kblk = KV_ROWS // CTX_LEN
        kspec = lambda w: pl.BlockSpec((CTX_LEN, w), lambda b, i: (b * kblk + kblk - 1, 0))
    else:
        kspec = lambda w: pl.BlockSpec((KV_ROWS, w), lambda b, i: (b, 0))
    return pl.pallas_call(
        body,
        out_shape=jax.ShapeDtypeStruct((n, BRANCH_W), BF16),
        grid=(nb, nq),
        in_specs=list(extra_specs) + [
            pl.BlockSpec((tq, q.shape[1]), lambda b, i: (b * nq + i, 0)),
            kspec(k.shape[1]), kspec(v.shape[1])],
        out_specs=pl.BlockSpec((tq, BRANCH_W), lambda b, i: (b * nq + i, 0)),
        compiler_params=_cparams(("arbitrary", "arbitrary")),
        name=name,
    )(*extra, q, k, v)


def _mixers(qs, kvs, pl_params, *, seq_q, ctx_only, tag):
    qa, qb, qc, qd = qs
    ka, va, kb, vb, kc, vc, kd, vd = kvs
    seq_lat = KV_ROWS - CTX_LEN
    smem = pl.BlockSpec(memory_space=pltpu.SMEM)
    row = lambda w: pl.BlockSpec((1, w), lambda b, i: (0, 0))
    tq = min(256, seq_q)
    oa = _attention(functools.partial(_attn_a_kernel, not ctx_only, seq_lat), qa, ka, va,
                    [pl_params["sink"]], [smem], seq_q=seq_q, tq=BLOCK, ctx_only=ctx_only,
                    name="attn_a_" + tag)
    ob = _attention(_attn_b_kernel, qb, kb, vb, [], [], seq_q=seq_q, tq=tq, ctx_only=ctx_only,
                    name="attn_b_" + tag)
    oc = _attention(functools.partial(_attn_c_kernel, pl_params["lam_init"]), qc, kc, vc,
                    [pl_params["lq1"], pl_params["lk1"], pl_params["lq2"], pl_params["lk2"],
                     pl_params["subln"]],
                    [row(HD)] * 4 + [row(2 * HD)], seq_q=seq_q, tq=tq, ctx_only=ctx_only,
                    name="attn_c_" + tag)
    od = _attention(_attn_d_kernel, qd, kd, vd, [], [], seq_q=seq_q, tq=tq, ctx_only=ctx_only,
                    name="attn_d_" + tag)
    return oa, ob, oc, od


def _merge_kernel(moe, oa_ref, ob_ref, oc_ref, od_ref, g_ref, x_ref, mod_ref, norm_ref,
                  wbr_ref, wout_ref, *refs):
    if moe:
        wr_ref, xo_ref, h_ref, lg_ref = refs
    else:
        xo_ref, h_ref = refs
    m = mod_ref[0]
    y = None
    for n, o_ref in enumerate((oa_ref, ob_ref, oc_ref, od_ref)):
        yn = g_ref[:, n * D_MODEL:(n + 1) * D_MODEL].astype(F32) * _dot(o_ref[...], wbr_ref[n])
        y = yn if y is None else y + yn
    z = _dot(y.astype(BF16), wout_ref[...])
    xn = x_ref[...] + m[2:3] * z
    xo_ref[...] = xn
    h = _adaln(xn, norm_ref[...], m[3:4], m[4:5])
    if moe:
        h_ref[...] = h
        h_hi = h.astype(BF16)
        h_lo = (h - h_hi.astype(F32)).astype(BF16)
        lg_ref[...] = (_dot_nt(wr_ref[0], h_hi) + _dot_nt(wr_ref[1], h_hi)) + _dot_nt(wr_ref[0], h_lo)
    else:
        h_ref[...] = h.astype(BF16)


def _merge(branches, g, x, mod, norm, wbr, wout, router_t, *, seq):
    n = x.shape[0]
    tps = seq // TM
    moe = router_t is not None
    tok = lambda i: (i, 0)
    in_specs = [pl.BlockSpec((TM, BRANCH_W), tok)] * 4 + [
        pl.BlockSpec((TM, GATE_COLS), tok),
        pl.BlockSpec((TM, D_MODEL), tok),
        pl.BlockSpec((1, 6, D_MODEL), lambda i: (i // tps, 0, 0)),
        _resident(norm.shape), _resident(wbr.shape), _resident(wout.shape)]
    args = list(branches) + [g, x, mod, norm, wbr, wout]
    out_shape = [jax.ShapeDtypeStruct((n, D_MODEL), F32),
                 jax.ShapeDtypeStruct((n, D_MODEL), F32 if moe else BF16)]
    out_specs = [pl.BlockSpec((TM, D_MODEL), tok), pl.BlockSpec((TM, D_MODEL), tok)]
    if moe:
        in_specs.append(_resident(router_t.shape))
        args.append(router_t)
        out_shape.append(jax.ShapeDtypeStruct((N_EXPERTS, n), F32))
        out_specs.append(pl.BlockSpec((N_EXPERTS, TM), lambda i: (0, i)))
    return pl.pallas_call(
        functools.partial(_merge_kernel, moe),
        out_shape=out_shape, grid=(n // TM,), in_specs=in_specs, out_specs=out_specs,
        compiler_params=_cparams(("arbitrary",)),
        name="merge_moe" if moe else "merge_dense",
    )(*args)


def _ffn_kernel(h_ref, x_ref, mod_ref, wg_ref, wu_ref, wd_ref, o_ref, hid_ref):
    h = h_ref[...]
    for c in range(0, D_FF, FFN_TF):
        a = _dot(h, wg_ref[:, c:c + FFN_TF])
        u = _dot(h, wu_ref[:, c:c + FFN_TF])
        hid_ref[:, c:c + FFN_TF] = (a * _sigmoid(a) * u).astype(BF16)
    m = mod_ref[0]
    o_ref[...] = x_ref[...] + m[5:6] * _dot(hid_ref[...], wd_ref[...])


def _dense_ffn(h, x, mod, wg, wu, wd, *, seq):
    n = x.shape[0]
    tm = min(FFN_TM, seq)
    tps = seq // tm
    tok = lambda i: (i, 0)
    return pl.pallas_call(
        _ffn_kernel,
        out_shape=jax.ShapeDtypeStruct((n, D_MODEL), F32),
        grid=(n // tm,),
        in_specs=[pl.BlockSpec((tm, D_MODEL), tok), pl.BlockSpec((tm, D_MODEL), tok),
                  pl.BlockSpec((1, 6, D_MODEL), lambda i: (i // tps, 0, 0)),
                  _resident(wg.shape), _resident(wu.shape), _resident(wd.shape)],
        out_specs=pl.BlockSpec((tm, D_MODEL), tok),
        scratch_shapes=[pltpu.VMEM((tm, D_FF), BF16)],
        compiler_params=_cparams(("arbitrary",)),
        name="dense_ffn",
    )(h, x, mod, wg, wu, wd)


ROUTE_CHUNK = 512


def _route_kernel(n, lg_ref, tri_ref, dest_ref, wts_ref, meta_ref, mem_ref, pos_ref):
    lg = lg_ref[...]
    eidx = lax.broadcasted_iota(jnp.int32, lg.shape, 0).astype(F32)
    none = float(N_EXPERTS)
    m1 = jnp.max(lg, axis=0, keepdims=True)
    i1 = jnp.min(jnp.where(lg == m1, eidx, none), axis=0, keepdims=True)
    lg2 = jnp.where(eidx == i1, -jnp.inf, lg)
    m2 = jnp.max(lg2, axis=0, keepdims=True)
    i2 = jnp.min(jnp.where(lg2 == m2, eidx, none), axis=0, keepdims=True)
    e = jnp.exp(m2 - m1)
    w1 = 1.0 / (1.0 + e)
    wts_ref[0:1, :] = w1
    wts_ref[1:2, :] = e * w1
    mem_ref[...] = jnp.where(eidx == i1, 1.0, 0.0) + jnp.where(eidx == i2, 1.0, 0.0)

    carry = jnp.zeros((N_EXPERTS, 1), F32)
    tri = tri_ref[...]
    for c in range(0, n, ROUTE_CHUNK):
        mc = mem_ref[:, c:c + ROUTE_CHUNK]
        pos_ref[:, c:c + ROUTE_CHUNK] = carry + _dot(mc.astype(BF16), tri)
        carry = carry + jnp.sum(mc, axis=1, keepdims=True)
    padded = jnp.floor((carry + (MOE_TM - 1.0)) * (1.0 / MOE_TM)) * MOE_TM
    esub = lax.broadcasted_iota(jnp.int32, (N_EXPERTS, 1), 0)
    pstart = jnp.zeros((N_EXPERTS, 1), F32)
    for k in range(N_EXPERTS - 1):
        pstart = pstart + jnp.where(esub > k, padded[k:k + 1, :], 0.0)
    pend = pstart + padded
    slot = pstart + pos_ref[...]
    dest_ref[0:1, :] = jnp.sum(jnp.where(eidx == i1, slot, 0.0), axis=0, keepdims=True).astype(jnp.int32)
    dest_ref[1:2, :] = jnp.sum(jnp.where(eidx == i2, slot, 0.0), axis=0, keepdims=True).astype(jnp.int32)

    nblk = pend[N_EXPERTS - 1:N_EXPERTS, :] * (1.0 / MOE_TM)
    brow = jnp.minimum(lax.broadcasted_iota(jnp.int32, (1, LANES), 1).astype(F32), nblk - 1.0) * MOE_TM
    be = jnp.zeros((1, LANES), F32)
    for k in range(N_EXPERTS):
        be = be + jnp.where(pend[k:k + 1, :] <= brow, 1.0, 0.0)
    meta_ref[...] = jnp.zeros(meta_ref.shape, jnp.int32)
    meta_ref[0:1, :] = jnp.minimum(be, N_EXPERTS - 1.0).astype(jnp.int32)
    meta_ref[1:2, :] = jnp.broadcast_to(nblk, (1, LANES)).astype(jnp.int32)


def _route(logits_t):
    n = logits_t.shape[1]
    nblk_max = n * TOP_K // MOE_TM + N_EXPERTS
    assert nblk_max <= LANES and n % ROUTE_CHUNK == 0
    tri = jnp.asarray(np.triu(np.ones((ROUTE_CHUNK, ROUTE_CHUNK)), 1), BF16)
    whole = lambda shape: pl.BlockSpec(shape, lambda: (0,) * len(shape))
    dest, wts, meta = pl.pallas_call(
        functools.partial(_route_kernel, n),
        out_shape=[jax.ShapeDtypeStruct((TOP_K, n), jnp.int32),
                   jax.ShapeDtypeStruct((TOP_K, n), F32),
                   jax.ShapeDtypeStruct((8, LANES), jnp.int32)],
        in_specs=[whole(logits_t.shape), whole(tri.shape)],
        out_specs=[whole((TOP_K, n)), whole((TOP_K, n)), whole((8, LANES))],
        scratch_shapes=[pltpu.VMEM((N_EXPERTS, n), F32), pltpu.VMEM((N_EXPERTS, n), F32)],
        compiler_params=pltpu.CompilerParams(vmem_limit_bytes=VMEM_LIMIT),
        name="moe_route",
    )(logits_t, tri)
    return dest, wts, meta[0, :nblk_max], meta[1, :1], nblk_max


def _dispatch_kernel(tg, dest_ref, h_ref, xin_hbm, xbuf_hbm, sem):
    del xin_hbm

    def copies(r):
        src = h_ref.at[pl.ds(r, 1)]
        return (pltpu.make_async_copy(src, xbuf_hbm.at[pl.ds(dest_ref[0, 0, r], 1)], sem),
                pltpu.make_async_copy(src, xbuf_hbm.at[pl.ds(dest_ref[0, 0, tg + r], 1)], sem))

    def start(r, carry):
        for cp in copies(r):
            cp.start()
        return carry

    def wait(r, carry):
        for cp in copies(r):
            cp.wait()
        return carry

    lax.fori_loop(0, tg, start, 0)
    lax.fori_loop(0, tg, wait, 0)


def _tile_dest(dest, t):
    n = dest.shape[1]
    return jnp.transpose(dest.reshape(TOP_K, n // t, t), (1, 0, 2)).reshape(n // t, 1, TOP_K * t)


def _dispatch(h, dest, nrows):
    n = h.shape[0]
    tg = 512
    nt = n // tg
    dest_t = _tile_dest(dest, tg)
    xbuf0 = jnp.zeros((nrows, D_MODEL), F32)
    return pl.pallas_call(
        functools.partial(_dispatch_kernel, tg),
        out_shape=jax.ShapeDtypeStruct((nrows, D_MODEL), F32),
        grid=(nt,),
        in_specs=[pl.BlockSpec((1, 1, TOP_K * tg), lambda i: (i, 0, 0), memory_space=pltpu.SMEM),
                  pl.BlockSpec((tg, D_MODEL), lambda i: (i, 0)), pl.BlockSpec(memory_space=pl.ANY)],
        out_specs=pl.BlockSpec(memory_space=pl.ANY),
        scratch_shapes=[pltpu.SemaphoreType.DMA(())],
        input_output_aliases={2: 0},
        compiler_params=_cparams(("arbitrary",)),
        name="moe_dispatch",
    )(dest_t, h, xbuf0)


def _moe_kernel(nf, blk_e_ref, nblk_ref, x_ref, w1_ref, w3_ref, w2_ref, o_ref, xb_ref, acc_ref):
    del blk_e_ref
    i = pl.program_id(0)
    j = pl.program_id(1)

    @pl.when(i < nblk_ref[0])
    def _():
        @pl.when(j == 0)
        def _():
            xb_ref[...] = x_ref[...].astype(BF16)
            acc_ref[...] = jnp.zeros_like(acc_ref)

        xb = xb_ref[...]
        a = _dot(xb, w1_ref[0])
        u = _dot(xb, w3_ref[0])
        acc_ref[...] += _dot((a * _sigmoid(a) * u).astype(BF16), w2_ref[0])

        @pl.when(j == nf - 1)
        def _():
            o_ref[...] = acc_ref[...]

    @pl.when((i >= nblk_ref[0]) & (j == nf - 1))
    def _():
        o_ref[...] = jnp.zeros_like(o_ref)


def _moe_ffn(xbuf, blk_e, nblk, nblk_max, w1, w3, w2):
    nf = EXPERT_FF // MOE_TF

    def rows(i, j, be, nb):
        return (i, 0)

    def jj(i, j, nb):
        return jnp.where(i < nb[0], j, nf - 1)

    grid_spec = pltpu.PrefetchScalarGridSpec(
        num_scalar_prefetch=2, grid=(nblk_max, nf),
        in_specs=[pl.BlockSpec((MOE_TM, D_MODEL), rows),
                  pl.BlockSpec((1, D_MODEL, MOE_TF), lambda i, j, be, nb: (be[i], 0, jj(i, j, nb))),
                  pl.BlockSpec((1, D_MODEL, MOE_TF), lambda i, j, be, nb: (be[i], 0, jj(i, j, nb))),
                  pl.BlockSpec((1, MOE_TF, D_MODEL), lambda i, j, be, nb: (be[i], jj(i, j, nb), 0))],
        out_specs=pl.BlockSpec((MOE_TM, D_MODEL), rows),
        scratch_shapes=[pltpu.VMEM((MOE_TM, D_MODEL), BF16), pltpu.VMEM((MOE_TM, D_MODEL), F32)])
    return pl.pallas_call(
        functools.partial(_moe_kernel, nf),
        out_shape=jax.ShapeDtypeStruct(xbuf.shape, F32),
        grid_spec=grid_spec,
        compiler_params=_cparams(("arbitrary", "arbitrary")),
        name="moe_experts",
    )(blk_e, nblk, xbuf, w1, w3, w2)


def _combine_kernel(tc, dest_ref, wts_ref, x_ref, mod_ref, y_hbm, o_ref, yg_ref, sem):
    def copies(r):
        return (pltpu.make_async_copy(y_hbm.at[pl.ds(dest_ref[0, 0, r], 1)],
                                      yg_ref.at[0, pl.ds(r, 1)], sem),
                pltpu.make_async_copy(y_hbm.at[pl.ds(dest_ref[0, 0, tc + r], 1)],
                                      yg_ref.at[1, pl.ds(r, 1)], sem))

    def start(r, carry):
        for cp in copies(r):
            cp.start()
        return carry

    def wait(r, carry):
        for cp in copies(r):
            cp.wait()
        return carry

    lax.fori_loop(0, tc, start, 0)
    lax.fori_loop(0, tc, wait, 0)
    m = mod_ref[0]
    w = wts_ref[...]
    y = w[:, 0:1] * yg_ref[0] + w[:, 1:2] * yg_ref[1]
    o_ref[...] = x_ref[...] + m[5:6] * y


def _combine(ybuf, dest, wts, x, mod, *, seq):
    n = x.shape[0]
    tc = 256
    nt = n // tc
    tps = seq // tc
    dest_t = _tile_dest(dest, tc)
    wts = wts.T
    tok = lambda i: (i, 0)
    return pl.pallas_call(
        functools.partial(_combine_kernel, tc),
        out_shape=jax.ShapeDtypeStruct((n, D_MODEL), F32),
        grid=(nt,),
        in_specs=[pl.BlockSpec((1, 1, TOP_K * tc), lambda i: (i, 0, 0), memory_space=pltpu.SMEM),
                  pl.BlockSpec((tc, TOP_K), tok),
                  pl.BlockSpec((tc, D_MODEL), tok),
                  pl.BlockSpec((1, 6, D_MODEL), lambda i: (i // tps, 0, 0)),
                  pl.BlockSpec(memory_space=pl.ANY)],
        out_specs=pl.BlockSpec((tc, D_MODEL), tok),
        scratch_shapes=[pltpu.VMEM((TOP_K, tc, D_MODEL), F32), pltpu.SemaphoreType.DMA(())],
        compiler_params=_cparams(("arbitrary",)),
        name="moe_combine",
    )(dest_t, wts, x, mod, ybuf)


def _head_cols(perm):
    return np.concatenate([np.arange(p * HD, (p + 1) * HD) for p in perm])


def _prep_w_in(w):
    hp = _head_cols(HEAD_PERM)
    kv_lo = Q_COLS
    kr_lo = kv_lo + KV_PAD_COLS - LANES
    parts = [w[:, 0:512][:, hp], w[:, 512:1024][:, hp], w[:, 1024:Q_COLS],
             w[:, kv_lo:kr_lo],
             jnp.pad(w[:, kr_lo:kr_lo + QK_ROPE], ((0, 0), (QK_NOPE, LANES - QK_NOPE - QK_ROPE))),
             w[:, kr_lo + QK_ROPE:]]
    return jnp.concatenate(parts, axis=1).astype(BF16)


def _prep_layer(l, p):
    hp = _head_cols(HEAD_PERM)
    zeros = lambda k: jnp.zeros((k,), F32)
    sc = LOG2E / math.sqrt(HD)
    sc_d = LOG2E / math.sqrt(QK_NOPE + QK_ROPE)
    wuq = p["d_w_uq"][l].reshape(Q_LORA, D_HEADS, QK_NOPE + QK_ROPE)
    wuq = jnp.pad(wuq, ((0, 0), (0, 0), (0, LANES - QK_NOPE - QK_ROPE))).reshape(Q_LORA, D_HEADS * LANES)
    wukv = p["d_w_ukv"][l].reshape(KV_LORA, D_HEADS, QK_NOPE + V_HEAD)
    wuk = jnp.pad(wukv[:, :, :QK_NOPE], ((0, 0), (0, 0), (0, LANES - QK_NOPE))).reshape(KV_LORA, D_HEADS * LANES)
    wuv = wukv[:, :, QK_NOPE:].reshape(KV_LORA, D_HEADS * V_HEAD)
    gains = jnp.stack([
        jnp.tile(p["a_qn"][l], 2) * sc, jnp.tile(p["b_qn"][l], 2) * sc, jnp.tile(p["c_qn"][l], 2) * sc,
        jnp.tile(p["a_kn"][l], 2), jnp.tile(p["b_kn"][l], 2), jnp.tile(p["c_kn"][l], 2),
        jnp.concatenate([p["d_qn_nope"][l], p["d_qn_rope"][l], zeros(32)]) * sc_d,
        jnp.concatenate([p["d_kn_nope"][l], zeros(64)]),
        jnp.concatenate([zeros(64), p["d_kn_rope"][l], zeros(32)]),
    ] + [zeros(LANES)] * 7).astype(F32)
    vec = jnp.stack([p["mix_norm"][l],
                     jnp.concatenate([p["d_q_norm"][l], p["d_kv_norm"][l], zeros(D_MODEL - Q_LORA - KV_LORA)])]
                    + [zeros(D_MODEL)] * 6).astype(F32)
    wbr = p["w_br"][l]
    wbr = jnp.stack([wbr[0][hp], wbr[1][hp], wbr[2], wbr[3]]).astype(BF16)
    return {
        "w_in": _prep_w_in(p["w_in"][l]),
        "wuq": wuq.astype(BF16),
        "wukv": jnp.concatenate([wuk, wuv], axis=1).astype(BF16),
        "gains": gains, "vec": vec, "wbr": wbr, "wout": p["w_out"][l].astype(BF16),
        "ffn_norm": p["ffn_norm"][l].reshape(1, D_MODEL),
        "sink": p["a_sink"][l][np.asarray(HEAD_PERM)].astype(F32),
        "lq1": p["c_lq1"][l].reshape(1, HD), "lk1": p["c_lk1"][l].reshape(1, HD),
        "lq2": p["c_lq2"][l].reshape(1, HD), "lk2": p["c_lk2"][l].reshape(1, HD),
        "subln": p["c_subln"][l].reshape(1, 2 * HD),
        "lam_init": 0.8 - 0.6 * math.exp(-0.3 * l),
    }


def _group_avg_mats():
    m64 = np.kron(np.eye(4), np.full((HD, HD), 1.0 / HD))
    one = np.zeros((LANES, LANES))
    one[:QK_NOPE, :QK_NOPE] = 1.0 / QK_NOPE
    one[QK_NOPE:QK_NOPE + QK_ROPE, QK_NOPE:QK_NOPE + QK_ROPE] = 1.0 / QK_ROPE
    md = np.kron(np.eye(2), one)
    return jnp.asarray(np.stack([m64, md]), BF16)


def _rope_tables(seq, rotary):
    if not rotary:
        one = jnp.ones((seq, LANES), F32)
        zero = jnp.zeros((seq, LANES), F32)
        return jnp.stack([one, zero, one, zero])
    t = jnp.arange(seq)

    def angles(rot_dim):
        nfreq = rot_dim // 4
        inv = jnp.power(ROPE_THETA, -jnp.arange(nfreq, dtype=F32) / nfreq)
        return jnp.concatenate([(t // GRID_W).astype(F32)[:, None] * inv,
                                (t % GRID_W).astype(F32)[:, None] * inv], axis=-1)

    ah = angles(HD)
    c64 = jnp.tile(jnp.cos(ah), (1, 4))
    s64 = jnp.tile(jnp.concatenate([-jnp.sin(ah), jnp.sin(ah)], axis=1), (1, 2))
    ar = angles(QK_ROPE)
    one = jnp.ones((seq, 1), F32)
    cd = jnp.concatenate([one * jnp.ones((1, QK_NOPE), F32), jnp.cos(ar), jnp.cos(ar),
                          one * jnp.ones((1, LANES - QK_NOPE - QK_ROPE), F32)], axis=1)
    sd = jnp.concatenate([jnp.zeros((seq, QK_NOPE), F32), -jnp.sin(ar), jnp.sin(ar),
                          jnp.zeros((seq, LANES - QK_NOPE - QK_ROPE), F32)], axis=1)
    return jnp.stack([c64, s64, cd, sd])


def kernel(x, c, ctx, c_ctx, w_mod, b_mod, mix_norm, ffn_norm, w_in, a_qn, a_kn, a_sink, b_qn, b_kn, c_qn, c_kn, c_lq1, c_lk1, c_lq2, c_lk2, c_subln, d_q_norm, d_kv_norm, d_w_uq, d_w_ukv, d_qn_nope, d_kn_nope, d_qn_rope, d_kn_rope, w_br, w_out, ff_w_gate, ff_w_up, ff_w_down, moe_router, moe_w1, moe_w3, moe_w2):
    p = dict(mix_norm=mix_norm, ffn_norm=ffn_norm, w_in=w_in, a_qn=a_qn, a_kn=a_kn, a_sink=a_sink,
             b_qn=b_qn, b_kn=b_kn, c_qn=c_qn, c_kn=c_kn, c_lq1=c_lq1, c_lk1=c_lk1, c_lq2=c_lq2,
             c_lk2=c_lk2, c_subln=c_subln, d_q_norm=d_q_norm, d_kv_norm=d_kv_norm, d_w_uq=d_w_uq,
             d_w_ukv=d_w_ukv, d_qn_nope=d_qn_nope, d_kn_nope=d_kn_nope, d_qn_rope=d_qn_rope,
             d_kn_rope=d_kn_rope, w_br=w_br, w_out=w_out)
    nb, seq, _ = x.shape
    assert seq + CTX_LEN == KV_ROWS and ctx.shape[1] == CTX_LEN and nb <= 15
    depth = w_mod.shape[0]
    lat = x.reshape(nb * seq, D_MODEL)
    cx = ctx.reshape(nb * CTX_LEN, D_MODEL)

    cond = jnp.zeros((16, D_MODEL), F32).at[:nb].set(c).at[nb].set(c_ctx)
    mod_all = _modulation(cond, w_mod, b_mod).reshape(depth, 16, 6, D_MODEL)
    rope_lat = _rope_tables(seq, True)
    rope_ctx = _rope_tables(CTX_LEN, False)
    mavg = _group_avg_mats()
    lat_blocks = seq // TM

    for l in range(depth):
        last = l == depth - 1
        lp = _prep_layer(l, p)
        mod_lat = mod_all[l, :nb]
        mod_ctx = jnp.broadcast_to(mod_all[l, nb:nb + 1], (nb, 6, D_MODEL))
        common = (lp["vec"], lp["gains"])
        wts = (lp["wuq"], lp["wukv"])
        kvs = [jnp.zeros((nb * KV_ROWS, wd), BF16) for wd in _KV_WIDTHS]
        if last:
            kvs = _project(cx, mod_ctx, *common, rope_ctx, mavg,
                           lp["w_in"][:, Q_COLS:Q_COLS + KV_PAD_COLS], *wts, seq=CTX_LEN,
                           row_blocks=KV_ROWS // TM, row_off=lat_blocks, with_q=False, kv_in=kvs)
        else:
            outs = _project(cx, mod_ctx, *common, rope_ctx, mavg, lp["w_in"], *wts, seq=CTX_LEN,
                            row_blocks=KV_ROWS // TM, row_off=lat_blocks, with_q=True, kv_in=kvs)
            kvs, q_ctx, g_ctx = outs[:8], outs[8:12], outs[12]
        outs = _project(lat, mod_lat, *common, rope_lat, mavg, lp["w_in"], *wts, seq=seq,
                        row_blocks=KV_ROWS // TM, row_off=0, with_q=True, kv_in=kvs)
        kvs, q_lat, g_lat = outs[:8], outs[8:12], outs[12]
        br_lat = _mixers(q_lat, kvs, lp, seq_q=seq, ctx_only=False, tag="lat")
        moe = l % 2 == 1
        i = l // 2
        router_t = None
        if moe:
            wr = moe_router[i].T
            wr_hi = wr.astype(BF16)
            router_t = jnp.stack([wr_hi, (wr - wr_hi.astype(F32)).astype(BF16)])
        res = _merge(br_lat, g_lat, lat, mod_lat, lp["ffn_norm"], lp["wbr"], lp["wout"], router_t, seq=seq)
        if not last:
            br_ctx = _mixers(q_ctx, kvs, lp, seq_q=CTX_LEN, ctx_only=True, tag="ctx")
            res_ctx = _merge(br_ctx, g_ctx, cx, mod_ctx, lp["ffn_norm"], lp["wbr"], lp["wout"],
                             router_t, seq=CTX_LEN)
        if not moe:
            wg, wu, wd = (ff_w_gate[i].astype(BF16), ff_w_up[i].astype(BF16), ff_w_down[i].astype(BF16))
            lat = _dense_ffn(res[1], res[0], mod_lat, wg, wu, wd, seq=seq)
            if not last:
                cx = _dense_ffn(res_ctx[1], res_ctx[0], mod_ctx, wg, wu, wd, seq=CTX_LEN)
        else:
            w1, w3, w2 = moe_w1[i].astype(BF16), moe_w3[i].astype(BF16), moe_w2[i].astype(BF16)

            def routed(res_t, mod_t, seq_t):
                xr, hr, lg = res_t
                dest, wts_r, blk_e, nblk, nblk_max = _route(lg)
                xbuf = _dispatch(hr, dest, nblk_max * MOE_TM)
                ybuf = _moe_ffn(xbuf, blk_e, nblk, nblk_max, w1, w3, w2)
                return _combine(ybuf, dest, wts_r, xr, mod_t, seq=seq_t)

            lat = routed(res, mod_lat, seq)
            if not last:
                cx = routed(res_ctx, mod_ctx, CTX_LEN)
    return lat.reshape(nb, seq, D_MODEL)
```

```python
import functools
import math

import numpy as np
import jax
import jax.numpy as jnp
from jax import lax
from jax.experimental import pallas as pl
from jax.experimental.pallas import tpu as pltpu

F32 = jnp.float32
BF16 = jnp.bfloat16

D_MODEL = 1024
DEPTH = 2
CTX_LEN = 256
GRID_W = 64
HD = 64
BLOCK = 128
WINDOW = 128
ROPE_THETA = 10000.0
EPS = 1e-6
NEG_INF = -1e30
A_HEADS = 8
C_HEADS = 4
D_HEADS = 8
Q_LORA = 512
KV_LORA = 256
QK_NOPE = 64
QK_ROPE = 32
V_HEAD = 64
BRANCH_W = 512
D_FF = 2816
N_EXPERTS = 8
TOP_K = 2
EXPERT_FF = 3584

LANES = 128
VMEM_LIMIT = 56 * 1024 * 1024
LOG2E = math.log2(math.e)

Q_COLS = 2048
KV_PAD_COLS = 1920
GATE_COLS = 4 * D_MODEL
W_COLS = Q_COLS + KV_PAD_COLS + GATE_COLS
HEAD_PERM = (0, 4, 1, 5, 2, 6, 3, 7)

TM = 256
MERGE_TM = 512
KV_ROWS = 2304
MOE_TM = 512
MOE_TF = 1792
FFN_TM = 512
FFN_TF = 256


def _cparams(sem):
    return pltpu.CompilerParams(dimension_semantics=sem, vmem_limit_bytes=VMEM_LIMIT)


def _resident(shape):
    nd = len(shape)
    return pl.BlockSpec(shape, lambda *_: (0,) * nd, pipeline_mode=pl.Buffered(1))


def _sigmoid(x):
    return 1.0 / (1.0 + jnp.exp(-x))


def _dot(a, b):
    return jnp.dot(a, b, preferred_element_type=F32)


def _dot_nt(a, b):
    return lax.dot_general(a, b, (((1,), (1,)), ((), ())), preferred_element_type=F32)


def _mod_kernel(c_ref, w_ref, b_ref, o_ref):
    cond = c_ref[...]
    s = cond * _sigmoid(cond)
    o_ref[0] = jnp.dot(s, w_ref[0], precision=lax.Precision.HIGHEST,
                       preferred_element_type=F32) + b_ref[0]


def _modulation(cond, w_mod, b_mod):
    depth = w_mod.shape[0]
    nct = 6 * D_MODEL // 1024
    return pl.pallas_call(
        _mod_kernel,
        out_shape=jax.ShapeDtypeStruct((depth, 16, 6 * D_MODEL), F32),
        grid=(depth, nct),
        in_specs=[pl.BlockSpec((16, D_MODEL), lambda l, j: (0, 0)),
                  pl.BlockSpec((1, D_MODEL, 1024), lambda l, j: (l, 0, j)),
                  pl.BlockSpec((1, 1, 1024), lambda l, j: (l, 0, j))],
        out_specs=pl.BlockSpec((1, 16, 1024), lambda l, j: (l, 0, j)),
        compiler_params=_cparams(("arbitrary", "arbitrary")),
        name="modulation",
    )(cond, w_mod, b_mod.reshape(depth, 1, 6 * D_MODEL))


def _adaln(x, g, shift, scale):
    ms = jnp.mean(x * x, axis=-1, keepdims=True)
    y = x * lax.rsqrt(ms + EPS) * g
    return y * (1.0 + scale) + shift


def _group_mean_sq(r, mavg):
    sq = (r * r).astype(BF16)
    w = r.shape[1]
    parts = []
    for c in range(0, w, 2 * LANES):
        cw = min(2 * LANES, w - c)
        parts.append(_dot(sq[:, c:c + cw], mavg[:cw, :cw]))
    return parts[0] if len(parts) == 1 else jnp.concatenate(parts, axis=1)


def _rope_slab(y, cos, sin, half):
    lane = lax.broadcasted_iota(jnp.int32, y.shape, 1)
    first = (lane % (2 * half)) < half
    sw = jnp.where(first, pltpu.roll(y, LANES - half, 1), pltpu.roll(y, half, 1))
    return y * cos + sw * sin


def _norm_rope(r, mavg, gain, cos, sin, half):
    ms = _group_mean_sq(r, mavg)
    outs = []
    for c in range(0, r.shape[1], LANES):
        y = r[:, c:c + LANES] * lax.rsqrt(ms[:, c:c + LANES] + EPS) * gain
        if cos is not None:
            y = _rope_slab(y, cos, sin, half)
        outs.append(y)
    return outs[0] if len(outs) == 1 else jnp.concatenate(outs, axis=1)


def _proj_kernel(with_q, x_ref, mod_ref, vec_ref, gains_ref, rope_ref, mavg_ref,
                 w_ref, wuq_ref, wukv_ref, *refs):
    if with_q:
        (qa_ref, qb_ref, qc_ref, qd_ref, g_ref) = refs[-5:]
        kv_refs = refs[-13:-5]
        kvo, go = Q_COLS, Q_COLS + KV_PAD_COLS
    else:
        kv_refs = refs[-8:]
        kvo = 0
    ka_ref, va_ref, kb_ref, vb_ref, kc_ref, vc_ref, kd_ref, vd_ref = kv_refs

    m = mod_ref[0]
    hb = _adaln(x_ref[...], vec_ref[0:1, :], m[0:1], m[1:2]).astype(BF16)

    def mm(lo, hi):
        return _dot(hb, w_ref[:, lo:hi])

    m64 = mavg_ref[0]
    md = mavg_ref[1]
    c64, s64, cd, sd = rope_ref[0], rope_ref[1], rope_ref[2], rope_ref[3]
    gains = gains_ref[...]

    def gain(i):
        return gains[i:i + 1, :]

    if with_q:
        qa_ref[...] = _norm_rope(mm(0, 512), m64, gain(0), c64, s64, 32).astype(BF16)
        qb_ref[...] = _norm_rope(mm(512, 1024), m64, gain(1), c64, s64, 32).astype(BF16)
        qc_ref[...] = _norm_rope(mm(1024, 1536), m64, gain(2), c64, s64, 32).astype(BF16)
        cq = mm(1536, 2048)
        cqn = cq * lax.rsqrt(jnp.mean(cq * cq, axis=-1, keepdims=True) + EPS) * vec_ref[1:2, 0:Q_LORA]
        qd = _dot(cqn.astype(BF16), wuq_ref[...])
        qd_ref[...] = _norm_rope(qd, md, gain(6), cd, sd, 16).astype(BF16)
        for n in range(4):
            gl = mm(go + n * D_MODEL, go + (n + 1) * D_MODEL)
            g_ref[:, n * D_MODEL:(n + 1) * D_MODEL] = _sigmoid(gl).astype(BF16)

    kvab = mm(kvo, kvo + 512)
    ka_ref[...] = _norm_rope(kvab[:, 0:128], m64, gain(3), c64, s64, 32).astype(BF16)
    va_ref[...] = kvab[:, 128:256].astype(BF16)
    kb_ref[...] = _norm_rope(kvab[:, 256:384], m64, gain(4), c64, s64, 32).astype(BF16)
    vb_ref[...] = kvab[:, 384:512].astype(BF16)
    kc_ref[...] = _norm_rope(mm(kvo + 512, kvo + 1024), m64, gain(5), c64, s64, 32).astype(BF16)
    vc_ref[...] = mm(kvo + 1024, kvo + 1536).astype(BF16)
    ckv = mm(kvo + 1536, kvo + 1792)
    ckvn = (ckv * lax.rsqrt(jnp.mean(ckv * ckv, axis=-1, keepdims=True) + EPS)
            * vec_ref[1:2, Q_LORA:Q_LORA + KV_LORA])
    kvd = _dot(ckvn.astype(BF16), wukv_ref[...])
    kr = _norm_rope(mm(kvo + 1792, kvo + 1920), md, gain(8), cd, sd, 16)
    kn = _norm_rope(kvd[:, 0:D_HEADS * LANES], md, gain(7), None, None, 0)
    kd_ref[...] = (kn + jnp.concatenate([kr] * D_HEADS, axis=1)).astype(BF16)
    vd_ref[...] = kvd[:, D_HEADS * LANES:].astype(BF16)


_KV_WIDTHS = (128, 128, 128, 128, 512, 512, D_HEADS * LANES, D_HEADS * V_HEAD)


def _project(x, mod, vec, gains, rope, mavg, w, wuq, wukv, *, seq, row_blocks, row_off,
             with_q, kv_in=None):
    n = x.shape[0]
    nb = n // seq
    tps = seq // TM
    grid = (n // TM,)

    def tok(i):
        return (i, 0)

    def kvrow(i):
        return ((i // tps) * row_blocks + row_off + i % tps, 0)

    in_specs = [
        pl.BlockSpec((TM, D_MODEL), tok),
        pl.BlockSpec((1, 6, D_MODEL), lambda i: (i // tps, 0, 0)),
        _resident(vec.shape),
        _resident(gains.shape),
        pl.BlockSpec((4, TM, LANES), lambda i: (0, i % tps, 0)),
        _resident(mavg.shape),
        _resident(w.shape),
        _resident(wuq.shape),
        _resident(wukv.shape),
    ]
    args = [x, mod, vec, gains, rope, mavg, w, wuq, wukv]
    kv_shapes = [jax.ShapeDtypeStruct((nb * KV_ROWS, wd), BF16) for wd in _KV_WIDTHS]
    kv_specs = [pl.BlockSpec((TM, wd), kvrow) for wd in _KV_WIDTHS]
    aliases = {}
    if kv_in is not None:
        for j, a in enumerate(kv_in):
            aliases[len(args)] = j
            args.append(a)
            in_specs.append(pl.BlockSpec(memory_space=pl.ANY))
    out_shapes = list(kv_shapes)
    out_specs = list(kv_specs)
    if with_q:
        for wd in (512, 512, 512, D_HEADS * LANES, GATE_COLS):
            out_shapes.append(jax.ShapeDtypeStruct((n, wd), BF16))
            out_specs.append(pl.BlockSpec((TM, wd), tok))
    return pl.pallas_call(
        functools.partial(_proj_kernel, with_q),
        out_shape=out_shapes, grid=grid, in_specs=in_specs, out_specs=out_specs,
        input_output_aliases=aliases,
        compiler_params=_cparams(("arbitrary",)),
        name="project_q" if with_q else "project_kv",
    )(*args)


def _softmax_pv(q, k, v, mask=None, extra_logit=None):
    s = _dot_nt(q, k)
    if mask is not None:
        s = jnp.where(mask, s, NEG_INF)
    m = jnp.max(s, axis=-1, keepdims=True)
    if extra_logit is not None:
        m = jnp.maximum(m, extra_logit)
    p = jnp.exp2(s - m)
    l = jnp.sum(p, axis=-1, keepdims=True)
    if extra_logit is not None:
        l = l + jnp.exp2(extra_logit - m)
    return _dot(p.astype(BF16), v) / l


def _split_heads(qs):
    lane = lax.broadcasted_iota(jnp.int32, qs.shape, 1)
    zero = jnp.zeros_like(qs)
    return jnp.concatenate([jnp.where(lane < HD, qs, zero), jnp.where(lane >= HD, qs, zero)], axis=0)


def _merge_halves(o, tq):
    lane = lax.broadcasted_iota(jnp.int32, (tq, LANES), 1)
    return jnp.where(lane < HD, o[:tq], o[tq:])


def _mix_b(q_ref, k_ref, v_ref, o_ref):
    tq = q_ref.shape[0]
    for j in range(4):
        qq = _split_heads(q_ref[:, j * LANES:(j + 1) * LANES])
        o = _softmax_pv(qq, k_ref[...], v_ref[...])
        o_ref[:, j * LANES:(j + 1) * LANES] = _merge_halves(o, tq).astype(BF16)


def _mix_a(local, seq, sink_ref, q_ref, k_ref, v_ref, o_ref):
    tq = q_ref.shape[0]
    nkeys = k_ref.shape[0]
    if local:
        span = tq + 2 * WINDOW
        q0 = pl.program_id(1) * tq
        start = pl.multiple_of(jnp.clip(q0 - WINDOW, 0, seq - span), BLOCK)
        k = jnp.concatenate([k_ref[pl.ds(start, span), :], k_ref[seq:nkeys, :]], axis=0)
        v = jnp.concatenate([v_ref[pl.ds(start, span), :], v_ref[seq:nkeys, :]], axis=0)
        nk = k.shape[0]
        row = lax.broadcasted_iota(jnp.int32, (2 * tq, nk), 0)
        col = lax.broadcasted_iota(jnp.int32, (2 * tq, nk), 1)
        qpos = q0 + jnp.where(row >= tq, row - tq, row)
        dist = (start + col) - qpos
        valid = (col >= span) | ((dist >= -WINDOW) & (dist <= WINDOW))
    else:
        k = k_ref[...]
        v = v_ref[...]
        valid = None
    rows = lax.broadcasted_iota(jnp.int32, (2 * tq, 1), 0)
    for j in range(4):
        qq = _split_heads(q_ref[:, j * LANES:(j + 1) * LANES])
        sink = jnp.where(rows < tq, sink_ref[2 * j], sink_ref[2 * j + 1]) * LOG2E
        o = _softmax_pv(qq, k, v, mask=valid, extra_logit=sink)
        o_ref[:, j * LANES:(j + 1) * LANES] = _merge_halves(o, tq).astype(BF16)


def _mix_c(lam_init, lq1_ref, lk1_ref, lq2_ref, lk2_ref, subln_ref, q_ref, k_ref, v_ref, o_ref):
    tq = q_ref.shape[0]
    lam = (jnp.exp(jnp.sum(lq1_ref[...] * lk1_ref[...], axis=-1, keepdims=True))
           - jnp.exp(jnp.sum(lq2_ref[...] * lk2_ref[...], axis=-1, keepdims=True)) + lam_init)
    for c in range(C_HEADS):
        qq = _split_heads(q_ref[:, c * LANES:(c + 1) * LANES])
        o = _softmax_pv(qq, k_ref[:, c * LANES:(c + 1) * LANES],
                        v_ref[:, c * LANES:(c + 1) * LANES])
        oc = o[:tq] - lam * o[tq:]
        oc = oc * lax.rsqrt(jnp.mean(oc * oc, axis=-1, keepdims=True) + EPS) * subln_ref[...]
        o_ref[:, c * LANES:(c + 1) * LANES] = (oc * (1.0 - lam_init)).astype(BF16)


def _mix_d(q_ref, k_ref, v_ref, o_ref):
    tq = q_ref.shape[0]
    for hp in range(D_HEADS // 2):
        v = v_ref[:, hp * LANES:(hp + 1) * LANES]
        halves = [_softmax_pv(q_ref[:, h * LANES:(h + 1) * LANES],
                              k_ref[:, h * LANES:(h + 1) * LANES], v)
                  for h in (2 * hp, 2 * hp + 1)]
        o = jnp.concatenate(halves, axis=0)
        o_ref[:, hp * LANES:(hp + 1) * LANES] = _merge_halves(o, tq).astype(BF16)


ATTN_TQ = 256


def _attention(body, q, k, v, extra, extra_specs, *, seq_q, ctx_only, name):
    n = q.shape[0]
    nb = n // seq_q
    tq = ATTN_TQ
    nq = seq_q // tq
    if ctx_only:
        kblk = KV_ROWS // CTX_LEN
        kspec = lambda w: pl.BlockSpec((CTX_LEN, w), lambda b, i: (b * kblk + kblk - 1, 0))
    else:
        kspec = lambda w: pl.BlockSpec((KV_ROWS, w), lambda b, i: (b, 0))
    tok = lambda b, i: (b * nq + i, 0)
    return pl.pallas_call(
        body,
        out_shape=jax.ShapeDtypeStruct((n, BRANCH_W), BF16),
        grid=(nb, nq),
        in_specs=list(extra_specs) + [pl.BlockSpec((tq, q.shape[1]), tok),
                                      kspec(k.shape[1]), kspec(v.shape[1])],
        out_specs=pl.BlockSpec((tq, BRANCH_W), tok),
        compiler_params=_cparams(("arbitrary", "arbitrary")),
        name=name,
    )(*extra, q, k, v)


def _mixers(qs, kvs, lp, *, seq_q, ctx_only, tag):
    qa, qb, qc, qd = qs
    ka, va, kb, vb, kc, vc, kd, vd = kvs
    smem = pl.BlockSpec(memory_space=pltpu.SMEM)
    row = lambda w: pl.BlockSpec((1, w), lambda b, i: (0, 0))
    common = dict(seq_q=seq_q, ctx_only=ctx_only)
    oa = _attention(functools.partial(_mix_a, not ctx_only, KV_ROWS - CTX_LEN), qa, ka, va,
                    [lp["sink"]], [smem], name="attn_a_" + tag, **common)
    ob = _attention(_mix_b, qb, kb, vb, [], [], name="attn_b_" + tag, **common)
    oc = _attention(functools.partial(_mix_c, lp["lam_init"]), qc, kc, vc,
                    [lp["lq1"], lp["lk1"], lp["lq2"], lp["lk2"], lp["subln"]],
                    [row(HD)] * 4 + [row(2 * HD)], name="attn_c_" + tag, **common)
    od = _attention(_mix_d, qd, kd, vd, [], [], name="attn_d_" + tag, **common)
    return oa, ob, oc, od


def _merge_kernel(moe, oa_ref, ob_ref, oc_ref, od_ref, g_ref, x_ref, mod_ref, norm_ref,
                  wbr_ref, wout_ref, *refs):
    if moe:
        wr_ref, xo_ref, h_ref, lg_ref = refs
    else:
        xo_ref, h_ref = refs
    m = mod_ref[0]
    y = None
    for n, o_ref in enumerate((oa_ref, ob_ref, oc_ref, od_ref)):
        yn = g_ref[:, n * D_MODEL:(n + 1) * D_MODEL].astype(F32) * _dot(o_ref[...], wbr_ref[n])
        y = yn if y is None else y + yn
    z = _dot(y.astype(BF16), wout_ref[...])
    xn = x_ref[...] + m[2:3] * z
    xo_ref[...] = xn
    h = _adaln(xn, norm_ref[...], m[3:4], m[4:5])
    if moe:
        h_ref[...] = h
        h_hi = h.astype(BF16)
        h_lo = (h - h_hi.astype(F32)).astype(BF16)
        lg_ref[...] = (_dot_nt(wr_ref[0], h_hi) + _dot_nt(wr_ref[1], h_hi)) + _dot_nt(wr_ref[0], h_lo)
    else:
        h_ref[...] = h.astype(BF16)


def _merge(branches, g, x, mod, norm, wbr, wout, router_t, *, seq):
    n = x.shape[0]
    tm = min(MERGE_TM, seq)
    tps = seq // tm
    moe = router_t is not None
    tok = lambda i: (i, 0)
    in_specs = [pl.BlockSpec((tm, BRANCH_W), tok)] * 4 + [
        pl.BlockSpec((tm, GATE_COLS), tok),
        pl.BlockSpec((tm, D_MODEL), tok),
        pl.BlockSpec((1, 6, D_MODEL), lambda i: (i // tps, 0, 0)),
        _resident(norm.shape), _resident(wbr.shape), _resident(wout.shape)]
    args = list(branches) + [g, x, mod, norm, wbr, wout]
    out_shape = [jax.ShapeDtypeStruct((n, D_MODEL), F32),
                 jax.ShapeDtypeStruct((n, D_MODEL), F32 if moe else BF16)]
    out_specs = [pl.BlockSpec((tm, D_MODEL), tok), pl.BlockSpec((tm, D_MODEL), tok)]
    if moe:
        in_specs.append(_resident(router_t.shape))
        args.append(router_t)
        out_shape.append(jax.ShapeDtypeStruct((N_EXPERTS, n), F32))
        out_specs.append(pl.BlockSpec((N_EXPERTS, tm), lambda i: (0, i)))
    return pl.pallas_call(
        functools.partial(_merge_kernel, moe),
        out_shape=out_shape, grid=(n // tm,), in_specs=in_specs, out_specs=out_specs,
        compiler_params=_cparams(("arbitrary",)),
        name="merge_moe" if moe else "merge_dense",
    )(*args)


def _ffn_kernel(h_ref, x_ref, mod_ref, wg_ref, wu_ref, wd_ref, o_ref, hid_ref):
    h = h_ref[...]
    for c in range(0, D_FF, FFN_TF):
        a = _dot(h, wg_ref[:, c:c + FFN_TF])
        u = _dot(h, wu_ref[:, c:c + FFN_TF])
        hid_ref[:, c:c + FFN_TF] = (a * _sigmoid(a) * u).astype(BF16)
    m = mod_ref[0]
    o_ref[...] = x_ref[...] + m[5:6] * _dot(hid_ref[...], wd_ref[...])


def _dense_ffn(h, x, mod, wg, wu, wd, *, seq):
    n = x.shape[0]
    tm = min(FFN_TM, seq)
    tps = seq // tm
    tok = lambda i: (i, 0)
    return pl.pallas_call(
        _ffn_kernel,
        out_shape=jax.ShapeDtypeStruct((n, D_MODEL), F32),
        grid=(n // tm,),
        in_specs=[pl.BlockSpec((tm, D_MODEL), tok), pl.BlockSpec((tm, D_MODEL), tok),
                  pl.BlockSpec((1, 6, D_MODEL), lambda i: (i // tps, 0, 0)),
                  _resident(wg.shape), _resident(wu.shape), _resident(wd.shape)],
        out_specs=pl.BlockSpec((tm, D_MODEL), tok),
        scratch_shapes=[pltpu.VMEM((tm, D_FF), BF16)],
        compiler_params=_cparams(("arbitrary",)),
        name="dense_ffn",
    )(h, x, mod, wg, wu, wd)


ROUTE_CHUNK = 512
DMA_UNROLL = 8


def _route_kernel(n, lg_ref, tri_ref, dest_ref, wts_ref, meta_ref, mem_ref, pos_ref):
    lg = lg_ref[...]
    eidx = lax.broadcasted_iota(jnp.int32, lg.shape, 0).astype(F32)
    none = float(N_EXPERTS)
    m1 = jnp.max(lg, axis=0, keepdims=True)
    i1 = jnp.min(jnp.where(lg == m1, eidx, none), axis=0, keepdims=True)
    lg2 = jnp.where(eidx == i1, -jnp.inf, lg)
    m2 = jnp.max(lg2, axis=0, keepdims=True)
    i2 = jnp.min(jnp.where(lg2 == m2, eidx, none), axis=0, keepdims=True)
    e = jnp.exp(m2 - m1)
    w1 = 1.0 / (1.0 + e)
    wts_ref[0:1, :] = w1
    wts_ref[1:2, :] = e * w1
    mem_ref[...] = jnp.where(eidx == i1, 1.0, 0.0) + jnp.where(eidx == i2, 1.0, 0.0)

    carry = jnp.zeros((N_EXPERTS, 1), F32)
    tri = tri_ref[...]
    for c in range(0, n, ROUTE_CHUNK):
        mc = mem_ref[:, c:c + ROUTE_CHUNK]
        pos_ref[:, c:c + ROUTE_CHUNK] = carry + _dot(mc.astype(BF16), tri)
        carry = carry + jnp.sum(mc, axis=1, keepdims=True)
    padded = jnp.floor((carry + (MOE_TM - 1.0)) * (1.0 / MOE_TM)) * MOE_TM
    esub = lax.broadcasted_iota(jnp.int32, (N_EXPERTS, 1), 0)
    pstart = jnp.zeros((N_EXPERTS, 1), F32)
    for k in range(N_EXPERTS - 1):
        pstart = pstart + jnp.where(esub > k, padded[k:k + 1, :], 0.0)
    pend = pstart + padded
    slot = pstart + pos_ref[...]
    dest_ref[0:1, :] = jnp.sum(jnp.where(eidx == i1, slot, 0.0), axis=0, keepdims=True).astype(jnp.int32)
    dest_ref[1:2, :] = jnp.sum(jnp.where(eidx == i2, slot, 0.0), axis=0, keepdims=True).astype(jnp.int32)

    nblk = pend[N_EXPERTS - 1:N_EXPERTS, :] * (1.0 / MOE_TM)
    brow = jnp.minimum(lax.broadcasted_iota(jnp.int32, (1, LANES), 1).astype(F32), nblk - 1.0) * MOE_TM
    be = jnp.zeros((1, LANES), F32)
    for k in range(N_EXPERTS):
        be = be + jnp.where(pend[k:k + 1, :] <= brow, 1.0, 0.0)
    meta_ref[...] = jnp.zeros(meta_ref.shape, jnp.int32)
    meta_ref[0:1, :] = jnp.minimum(be, N_EXPERTS - 1.0).astype(jnp.int32)
    meta_ref[1:2, :] = jnp.broadcast_to(nblk, (1, LANES)).astype(jnp.int32)


def _route(logits_t):
    n = logits_t.shape[1]
    nblk_max = n * TOP_K // MOE_TM + N_EXPERTS
    assert nblk_max <= LANES and n % ROUTE_CHUNK == 0
    tri = jnp.asarray(np.triu(np.ones((ROUTE_CHUNK, ROUTE_CHUNK)), 1), BF16)
    whole = lambda shape: pl.BlockSpec(shape, lambda: (0,) * len(shape))
    dest, wts, meta = pl.pallas_call(
        functools.partial(_route_kernel, n),
        out_shape=[jax.ShapeDtypeStruct((TOP_K, n), jnp.int32),
                   jax.ShapeDtypeStruct((TOP_K, n), F32),
                   jax.ShapeDtypeStruct((8, LANES), jnp.int32)],
        in_specs=[whole(logits_t.shape), whole(tri.shape)],
        out_specs=[whole((TOP_K, n)), whole((TOP_K, n)), whole((8, LANES))],
        scratch_shapes=[pltpu.VMEM((N_EXPERTS, n), F32), pltpu.VMEM((N_EXPERTS, n), F32)],
        compiler_params=pltpu.CompilerParams(vmem_limit_bytes=VMEM_LIMIT),
        name="moe_route",
    )(logits_t, tri)
    return dest, wts, meta[0, :nblk_max], meta[1, :1], nblk_max


def _dispatch_kernel(tg, dest_ref, h_ref, xin_hbm, xbuf_hbm, sem):
    del xin_hbm

    def copies(r):
        src = h_ref.at[pl.ds(r, 1)]
        return (pltpu.make_async_copy(src, xbuf_hbm.at[pl.ds(dest_ref[0, 0, r], 1)], sem),
                pltpu.make_async_copy(src, xbuf_hbm.at[pl.ds(dest_ref[0, 0, tg + r], 1)], sem))

    def start(r, carry):
        for k, cp in enumerate(copies(r)):
            cp.start(priority=k)
        return carry

    def wait(r, carry):
        for cp in copies(r):
            cp.wait()
        return carry

    lax.fori_loop(0, tg, start, 0, unroll=DMA_UNROLL)
    lax.fori_loop(0, tg, wait, 0, unroll=DMA_UNROLL)


def _tile_dest(dest, t):
    n = dest.shape[1]
    return jnp.transpose(dest.reshape(TOP_K, n // t, t), (1, 0, 2)).reshape(n // t, 1, TOP_K * t)


def _dispatch(h, dest, nrows):
    n = h.shape[0]
    tg = 512
    nt = n // tg
    dest_t = _tile_dest(dest, tg)
    xbuf0 = jnp.zeros((nrows, D_MODEL), F32)
    return pl.pallas_call(
        functools.partial(_dispatch_kernel, tg),
        out_shape=jax.ShapeDtypeStruct((nrows, D_MODEL), F32),
        grid=(nt,),
        in_specs=[pl.BlockSpec((1, 1, TOP_K * tg), lambda i: (i, 0, 0), memory_space=pltpu.SMEM),
                  pl.BlockSpec((tg, D_MODEL), lambda i: (i, 0)), pl.BlockSpec(memory_space=pl.ANY)],
        out_specs=pl.BlockSpec(memory_space=pl.ANY),
        scratch_shapes=[pltpu.SemaphoreType.DMA(())],
        input_output_aliases={2: 0},
        compiler_params=_cparams(("arbitrary",)),
        name="moe_dispatch",
    )(dest_t, h, xbuf0)


def _moe_kernel(nf, blk_e_ref, nblk_ref, x_ref, w1_ref, w3_ref, w2_ref, o_ref, xb_ref, acc_ref):
    del blk_e_ref
    i = pl.program_id(0)
    j = pl.program_id(1)

    @pl.when(i < nblk_ref[0])
    def _():
        @pl.when(j == 0)
        def _():
            xb_ref[...] = x_ref[...].astype(BF16)
            acc_ref[...] = jnp.zeros_like(acc_ref)

        xb = xb_ref[...]
        a = _dot(xb, w1_ref[0])
        u = _dot(xb, w3_ref[0])
        acc_ref[...] += _dot((a * _sigmoid(a) * u).astype(BF16), w2_ref[0])

        @pl.when(j == nf - 1)
        def _():
            o_ref[...] = acc_ref[...]

    @pl.when((i >= nblk_ref[0]) & (j == nf - 1))
    def _():
        o_ref[...] = jnp.zeros_like(o_ref)


def _moe_ffn(xbuf, blk_e, nblk, nblk_max, w1, w3, w2):
    nf = EXPERT_FF // MOE_TF

    def rows(i, j, be, nb):
        return (i, 0)

    def jj(i, j, nb):
        return jnp.where(i < nb[0], j, nf - 1)

    grid_spec = pltpu.PrefetchScalarGridSpec(
        num_scalar_prefetch=2, grid=(nblk_max, nf),
        in_specs=[pl.BlockSpec((MOE_TM, D_MODEL), rows),
                  pl.BlockSpec((1, D_MODEL, MOE_TF), lambda i, j, be, nb: (be[i], 0, jj(i, j, nb))),
                  pl.BlockSpec((1, D_MODEL, MOE_TF), lambda i, j, be, nb: (be[i], 0, jj(i, j, nb))),
                  pl.BlockSpec((1, MOE_TF, D_MODEL), lambda i, j, be, nb: (be[i], jj(i, j, nb), 0))],
        out_specs=pl.BlockSpec((MOE_TM, D_MODEL), rows),
        scratch_shapes=[pltpu.VMEM((MOE_TM, D_MODEL), BF16), pltpu.VMEM((MOE_TM, D_MODEL), F32)])
    return pl.pallas_call(
        functools.partial(_moe_kernel, nf),
        out_shape=jax.ShapeDtypeStruct(xbuf.shape, F32),
        grid_spec=grid_spec,
        compiler_params=_cparams(("arbitrary", "arbitrary")),
        name="moe_experts",
    )(blk_e, nblk, xbuf, w1, w3, w2)


def _combine_kernel(tc, dest_ref, wts_ref, x_ref, mod_ref, y_hbm, o_ref, yg_ref, sem):
    def copies(r):
        return (pltpu.make_async_copy(y_hbm.at[pl.ds(dest_ref[0, 0, r], 1)],
                                      yg_ref.at[0, pl.ds(r, 1)], sem),
                pltpu.make_async_copy(y_hbm.at[pl.ds(dest_ref[0, 0, tc + r], 1)],
                                      yg_ref.at[1, pl.ds(r, 1)], sem))

    def start(r, carry):
        for k, cp in enumerate(copies(r)):
            cp.start(priority=k)
        return carry

    def wait(r, carry):
        for cp in copies(r):
            cp.wait()
        return carry

    lax.fori_loop(0, tc, start, 0, unroll=DMA_UNROLL)
    lax.fori_loop(0, tc, wait, 0, unroll=DMA_UNROLL)
    m = mod_ref[0]
    w = wts_ref[...]
    y = w[:, 0:1] * yg_ref[0] + w[:, 1:2] * yg_ref[1]
    o_ref[...] = x_ref[...] + m[5:6] * y


def _combine(ybuf, dest, wts, x, mod, *, seq):
    n = x.shape[0]
    tc = 256
    nt = n // tc
    tps = seq // tc
    dest_t = _tile_dest(dest, tc)
    wts = wts.T
    tok = lambda i: (i, 0)
    return pl.pallas_call(
        functools.partial(_combine_kernel, tc),
        out_shape=jax.ShapeDtypeStruct((n, D_MODEL), F32),
        grid=(nt,),
        in_specs=[pl.BlockSpec((1, 1, TOP_K * tc), lambda i: (i, 0, 0), memory_space=pltpu.SMEM),
                  pl.BlockSpec((tc, TOP_K), tok),
                  pl.BlockSpec((tc, D_MODEL), tok),
                  pl.BlockSpec((1, 6, D_MODEL), lambda i: (i // tps, 0, 0)),
                  pl.BlockSpec(memory_space=pl.ANY)],
        out_specs=pl.BlockSpec((tc, D_MODEL), tok),
        scratch_shapes=[pltpu.VMEM((TOP_K, tc, D_MODEL), F32), pltpu.SemaphoreType.DMA(())],
        compiler_params=_cparams(("arbitrary",)),
        name="moe_combine",
    )(dest_t, wts, x, mod, ybuf)


def _head_cols(perm):
    return np.concatenate([np.arange(p * HD, (p + 1) * HD) for p in perm])


def _prep_w_in(w):
    hp = _head_cols(HEAD_PERM)
    kv_lo = Q_COLS
    kr_lo = kv_lo + KV_PAD_COLS - LANES
    parts = [w[:, 0:512][:, hp], w[:, 512:1024][:, hp], w[:, 1024:Q_COLS],
             w[:, kv_lo:kr_lo],
             jnp.pad(w[:, kr_lo:kr_lo + QK_ROPE], ((0, 0), (QK_NOPE, LANES - QK_NOPE - QK_ROPE))),
             w[:, kr_lo + QK_ROPE:]]
    return jnp.concatenate(parts, axis=1).astype(BF16)


def _prep_layer(l, p):
    hp = _head_cols(HEAD_PERM)
    zeros = lambda k: jnp.zeros((k,), F32)
    sc = LOG2E / math.sqrt(HD)
    sc_d = LOG2E / math.sqrt(QK_NOPE + QK_ROPE)
    wuq = p["d_w_uq"][l].reshape(Q_LORA, D_HEADS, QK_NOPE + QK_ROPE)
    wuq = jnp.pad(wuq, ((0, 0), (0, 0), (0, LANES - QK_NOPE - QK_ROPE))).reshape(Q_LORA, D_HEADS * LANES)
    wukv = p["d_w_ukv"][l].reshape(KV_LORA, D_HEADS, QK_NOPE + V_HEAD)
    wuk = jnp.pad(wukv[:, :, :QK_NOPE], ((0, 0), (0, 0), (0, LANES - QK_NOPE))).reshape(KV_LORA, D_HEADS * LANES)
    wuv = wukv[:, :, QK_NOPE:].reshape(KV_LORA, D_HEADS * V_HEAD)
    gains = jnp.stack([
        jnp.tile(p["a_qn"][l], 2) * sc, jnp.tile(p["b_qn"][l], 2) * sc, jnp.tile(p["c_qn"][l], 2) * sc,
        jnp.tile(p["a_kn"][l], 2), jnp.tile(p["b_kn"][l], 2), jnp.tile(p["c_kn"][l], 2),
        jnp.concatenate([p["d_qn_nope"][l], p["d_qn_rope"][l], zeros(32)]) * sc_d,
        jnp.concatenate([p["d_kn_nope"][l], zeros(64)]),
        jnp.concatenate([zeros(64), p["d_kn_rope"][l], zeros(32)]),
    ] + [zeros(LANES)] * 7).astype(F32)
    vec = jnp.stack([p["mix_norm"][l],
                     jnp.concatenate([p["d_q_norm"][l], p["d_kv_norm"][l], zeros(D_MODEL - Q_LORA - KV_LORA)])]
                    + [zeros(D_MODEL)] * 6).astype(F32)
    wbr = p["w_br"][l]
    wbr = jnp.stack([wbr[0][hp], wbr[1][hp], wbr[2], wbr[3]]).astype(BF16)
    return {
        "w_in": _prep_w_in(p["w_in"][l]),
        "wuq": wuq.astype(BF16),
        "wukv": jnp.concatenate([wuk, wuv], axis=1).astype(BF16),
        "gains": gains, "vec": vec, "wbr": wbr, "wout": p["w_out"][l].astype(BF16),
        "ffn_norm": p["ffn_norm"][l].reshape(1, D_MODEL),
        "sink": p["a_sink"][l][np.asarray(HEAD_PERM)].astype(F32),
        "lq1": p["c_lq1"][l].reshape(1, HD), "lk1": p["c_lk1"][l].reshape(1, HD),
        "lq2": p["c_lq2"][l].reshape(1, HD), "lk2": p["c_lk2"][l].reshape(1, HD),
        "subln": p["c_subln"][l].reshape(1, 2 * HD),
        "lam_init": 0.8 - 0.6 * math.exp(-0.3 * l),
    }


def _group_avg_mats():
    m64 = np.kron(np.eye(4), np.full((HD, HD), 1.0 / HD))
    one = np.zeros((LANES, LANES))
    one[:QK_NOPE, :QK_NOPE] = 1.0 / QK_NOPE
    one[QK_NOPE:QK_NOPE + QK_ROPE, QK_NOPE:QK_NOPE + QK_ROPE] = 1.0 / QK_ROPE
    md = np.kron(np.eye(2), one)
    return jnp.asarray(np.stack([m64, md]), BF16)


def _rope_tables(seq, rotary):
    if not rotary:
        one = jnp.ones((seq, LANES), F32)
        zero = jnp.zeros((seq, LANES), F32)
        return jnp.stack([one, zero, one, zero])
    t = jnp.arange(seq)

    def angles(rot_dim):
        nfreq = rot_dim // 4
        inv = jnp.power(ROPE_THETA, -jnp.arange(nfreq, dtype=F32) / nfreq)
        return jnp.concatenate([(t // GRID_W).astype(F32)[:, None] * inv,
                                (t % GRID_W).astype(F32)[:, None] * inv], axis=-1)

    ah = angles(HD)
    c64 = jnp.tile(jnp.cos(ah), (1, 4))
    s64 = jnp.tile(jnp.concatenate([-jnp.sin(ah), jnp.sin(ah)], axis=1), (1, 2))
    ar = angles(QK_ROPE)
    one = jnp.ones((seq, 1), F32)
    cd = jnp.concatenate([one * jnp.ones((1, QK_NOPE), F32), jnp.cos(ar), jnp.cos(ar),
                          one * jnp.ones((1, LANES - QK_NOPE - QK_ROPE), F32)], axis=1)
    sd = jnp.concatenate([jnp.zeros((seq, QK_NOPE), F32), -jnp.sin(ar), jnp.sin(ar),
                          jnp.zeros((seq, LANES - QK_NOPE - QK_ROPE), F32)], axis=1)
    return jnp.stack([c64, s64, cd, sd])


def kernel(x, c, ctx, c_ctx, w_mod, b_mod, mix_norm, ffn_norm, w_in, a_qn, a_kn, a_sink, b_qn, b_kn, c_qn, c_kn, c_lq1, c_lk1, c_lq2, c_lk2, c_subln, d_q_norm, d_kv_norm, d_w_uq, d_w_ukv, d_qn_nope, d_kn_nope, d_qn_rope, d_kn_rope, w_br, w_out, ff_w_gate, ff_w_up, ff_w_down, moe_router, moe_w1, moe_w3, moe_w2):
    p = dict(mix_norm=mix_norm, ffn_norm=ffn_norm, w_in=w_in, a_qn=a_qn, a_kn=a_kn, a_sink=a_sink,
             b_qn=b_qn, b_kn=b_kn, c_qn=c_qn, c_kn=c_kn, c_lq1=c_lq1, c_lk1=c_lk1, c_lq2=c_lq2,
             c_lk2=c_lk2, c_subln=c_subln, d_q_norm=d_q_norm, d_kv_norm=d_kv_norm, d_w_uq=d_w_uq,
             d_w_ukv=d_w_ukv, d_qn_nope=d_qn_nope, d_kn_nope=d_kn_nope, d_qn_rope=d_qn_rope,
             d_kn_rope=d_kn_rope, w_br=w_br, w_out=w_out)
    nb, seq, _ = x.shape
    assert seq + CTX_LEN == KV_ROWS and ctx.shape[1] == CTX_LEN and nb <= 15
    depth = w_mod.shape[0]
    lat = x.reshape(nb * seq, D_MODEL)
    cx = ctx.reshape(nb * CTX_LEN, D_MODEL)

    cond = jnp.zeros((16, D_MODEL), F32).at[:nb].set(c).at[nb].set(c_ctx)
    mod_all = _modulation(cond, w_mod, b_mod).reshape(depth, 16, 6, D_MODEL)
    rope_lat = _rope_tables(seq, True)
    rope_ctx = _rope_tables(CTX_LEN, False)
    mavg = _group_avg_mats()
    lat_blocks = seq // TM

    for l in range(depth):
        last = l == depth - 1
        lp = _prep_layer(l, p)
        mod_lat = mod_all[l, :nb]
        mod_ctx = jnp.broadcast_to(mod_all[l, nb:nb + 1], (nb, 6, D_MODEL))
        common = (lp["vec"], lp["gains"])
        wts = (lp["wuq"], lp["wukv"])
        kvs = [jnp.zeros((nb * KV_ROWS, wd), BF16) for wd in _KV_WIDTHS]
        if last:
            kvs = _project(cx, mod_ctx, *common, rope_ctx, mavg,
                           lp["w_in"][:, Q_COLS:Q_COLS + KV_PAD_COLS], *wts, seq=CTX_LEN,
                           row_blocks=KV_ROWS // TM, row_off=lat_blocks, with_q=False, kv_in=kvs)
        else:
            outs = _project(cx, mod_ctx, *common, rope_ctx, mavg, lp["w_in"], *wts, seq=CTX_LEN,
                            row_blocks=KV_ROWS // TM, row_off=lat_blocks, with_q=True, kv_in=kvs)
            kvs, q_ctx, g_ctx = outs[:8], outs[8:12], outs[12]
        outs = _project(lat, mod_lat, *common, rope_lat, mavg, lp["w_in"], *wts, seq=seq,
                        row_blocks=KV_ROWS // TM, row_off=0, with_q=True, kv_in=kvs)
        kvs, q_lat, g_lat = outs[:8], outs[8:12], outs[12]
        br_lat = _mixers(q_lat, kvs, lp, seq_q=seq, ctx_only=False, tag="lat")
        moe = l % 2 == 1
        i = l // 2
        router_t = None
        if moe:
            wr = moe_router[i].T
            wr_hi = wr.astype(BF16)
            router_t = jnp.stack([wr_hi, (wr - wr_hi.astype(F32)).astype(BF16)])
        res = _merge(br_lat, g_lat, lat, mod_lat, lp["ffn_norm"], lp["wbr"], lp["wout"], router_t, seq=seq)
        if not last:
            br_ctx = _mixers(q_ctx, kvs, lp, seq_q=CTX_LEN, ctx_only=True, tag="ctx")
            res_ctx = _merge(br_ctx, g_ctx, cx, mod_ctx, lp["ffn_norm"], lp["wbr"], lp["wout"],
                             router_t, seq=CTX_LEN)
        if not moe:
            wg, wu, wd = (ff_w_gate[i].astype(BF16), ff_w_up[i].astype(BF16), ff_w_down[i].astype(BF16))
            lat = _dense_ffn(res[1], res[0], mod_lat, wg, wu, wd, seq=seq)
            if not last:
                cx = _dense_ffn(res_ctx[1], res_ctx[0], mod_ctx, wg, wu, wd, seq=CTX_LEN)
        else:
            w1, w3, w2 = moe_w1[i].astype(BF16), moe_w3[i].astype(BF16), moe_w2[i].astype(BF16)

            def routed(res_t, mod_t, seq_t):
                xr, hr, lg = res_t
                dest, wts_r, blk_e, nblk, nblk_max = _route(lg)
                xbuf = _dispatch(hr, dest, nblk_max * MOE_TM)
                ybuf = _moe_ffn(xbuf, blk_e, nblk, nblk_max, w1, w3, w2)
                return _combine(ybuf, dest, wts_r, xr, mod_t, seq=seq_t)

            lat = routed(res, mod_lat, seq)
            if not last:
                cx = routed(res_ctx, mod_ctx, CTX_LEN)
    return lat.reshape(nb, seq, D_MODEL)
```

```python
import functools
import math

import numpy as np
import jax
import jax.numpy as jnp
from jax import lax
from jax.experimental import pallas as pl
from jax.experimental.pallas import tpu as pltpu

F32 = jnp.float32
BF16 = jnp.bfloat16

D_MODEL = 1024
DEPTH = 2
CTX_LEN = 256
GRID_W = 64
HD = 64
BLOCK = 128
WINDOW = 128
ROPE_THETA = 10000.0
EPS = 1e-6
NEG_INF = -1e30
A_HEADS = 8
C_HEADS = 4
D_HEADS = 8
Q_LORA = 512
KV_LORA = 256
QK_NOPE = 64
QK_ROPE = 32
V_HEAD = 64
BRANCH_W = 512
D_FF = 2816
N_EXPERTS = 8
TOP_K = 2
EXPERT_FF = 3584

LANES = 128
VMEM_LIMIT = 56 * 1024 * 1024
LOG2E = math.log2(math.e)

Q_COLS = 2048
KV_PAD_COLS = 1920
GATE_COLS = 4 * D_MODEL
W_COLS = Q_COLS + KV_PAD_COLS + GATE_COLS
HEAD_PERM = (0, 4, 1, 5, 2, 6, 3, 7)

TM = 256
MERGE_TM = 512
KV_ROWS = 2304
MOE_TM = 512
MOE_TF = 1792
FFN_TM = 512
FFN_TF = 256


def _cparams(sem):
    return pltpu.CompilerParams(dimension_semantics=sem, vmem_limit_bytes=VMEM_LIMIT)


def _resident(shape):
    nd = len(shape)
    return pl.BlockSpec(shape, lambda *_: (0,) * nd, pipeline_mode=pl.Buffered(1))


def _sigmoid(x):
    return 1.0 / (1.0 + jnp.exp(-x))


def _dot(a, b):
    return jnp.dot(a, b, preferred_element_type=F32)


def _dot_nt(a, b):
    return lax.dot_general(a, b, (((1,), (1,)), ((), ())), preferred_element_type=F32)


def _mod_kernel(c_ref, w_ref, b_ref, o_ref):
    cond = c_ref[...]
    s = cond * _sigmoid(cond)
    o_ref[0] = jnp.dot(s, w_ref[0], precision=lax.Precision.HIGHEST,
                       preferred_element_type=F32) + b_ref[0]


def _modulation(cond, w_mod, b_mod):
    depth = w_mod.shape[0]
    nct = 6 * D_MODEL // 1024
    return pl.pallas_call(
        _mod_kernel,
        out_shape=jax.ShapeDtypeStruct((depth, 16, 6 * D_MODEL), F32),
        grid=(depth, nct),
        in_specs=[pl.BlockSpec((16, D_MODEL), lambda l, j: (0, 0)),
                  pl.BlockSpec((1, D_MODEL, 1024), lambda l, j: (l, 0, j)),
                  pl.BlockSpec((1, 1, 1024), lambda l, j: (l, 0, j))],
        out_specs=pl.BlockSpec((1, 16, 1024), lambda l, j: (l, 0, j)),
        compiler_params=_cparams(("arbitrary", "arbitrary")),
        name="modulation",
    )(cond, w_mod, b_mod.reshape(depth, 1, 6 * D_MODEL))


def _adaln(x, g, shift, scale):
    ms = jnp.mean(x * x, axis=-1, keepdims=True)
    y = x * lax.rsqrt(ms + EPS) * g
    return y * (1.0 + scale) + shift


def _group_mean_sq(r, mavg):
    sq = (r * r).astype(BF16)
    w = r.shape[1]
    parts = []
    for c in range(0, w, 2 * LANES):
        cw = min(2 * LANES, w - c)
        parts.append(_dot(sq[:, c:c + cw], mavg[:cw, :cw]))
    return parts[0] if len(parts) == 1 else jnp.concatenate(parts, axis=1)


def _rope_slab(y, cos, sin, half):
    lane = lax.broadcasted_iota(jnp.int32, y.shape, 1)
    first = (lane % (2 * half)) < half
    sw = jnp.where(first, pltpu.roll(y, LANES - half, 1), pltpu.roll(y, half, 1))
    return y * cos + sw * sin


def _norm_rope(r, mavg, gain, cos, sin, half):
    ms = _group_mean_sq(r, mavg)
    outs = []
    for c in range(0, r.shape[1], LANES):
        y = r[:, c:c + LANES] * lax.rsqrt(ms[:, c:c + LANES] + EPS) * gain
        if cos is not None:
            y = _rope_slab(y, cos, sin, half)
        outs.append(y)
    return outs[0] if len(outs) == 1 else jnp.concatenate(outs, axis=1)


def _proj_kernel(with_q, x_ref, mod_ref, vec_ref, gains_ref, rope_ref, mavg_ref,
                 w_ref, wuq_ref, wukv_ref, *refs):
    if with_q:
        (qa_ref, qb_ref, qc_ref, qd_ref, g_ref) = refs[-5:]
        kv_refs = refs[-13:-5]
        kvo, go = Q_COLS, Q_COLS + KV_PAD_COLS
    else:
        kv_refs = refs[-8:]
        kvo = 0
    ka_ref, va_ref, kb_ref, vb_ref, kc_ref, vc_ref, kd_ref, vd_ref = kv_refs

    m = mod_ref[0]
    hb = _adaln(x_ref[...], vec_ref[0:1, :], m[0:1], m[1:2]).astype(BF16)

    def mm(lo, hi):
        return _dot(hb, w_ref[:, lo:hi])

    m64 = mavg_ref[0]
    md = mavg_ref[1]
    c64, s64, cd, sd = rope_ref[0], rope_ref[1], rope_ref[2], rope_ref[3]
    gains = gains_ref[...]

    def gain(i):
        return gains[i:i + 1, :]

    if with_q:
        qa_ref[...] = _norm_rope(mm(0, 512), m64, gain(0), c64, s64, 32).astype(BF16)
        qb_ref[...] = _norm_rope(mm(512, 1024), m64, gain(1), c64, s64, 32).astype(BF16)
        qc_ref[...] = _norm_rope(mm(1024, 1536), m64, gain(2), c64, s64, 32).astype(BF16)
        cq = mm(1536, 2048)
        cqn = cq * lax.rsqrt(jnp.mean(cq * cq, axis=-1, keepdims=True) + EPS) * vec_ref[1:2, 0:Q_LORA]
        qd = _dot(cqn.astype(BF16), wuq_ref[...])
        qd_ref[...] = _norm_rope(qd, md, gain(6), cd, sd, 16).astype(BF16)
        for n in range(4):
            gl = mm(go + n * D_MODEL, go + (n + 1) * D_MODEL)
            g_ref[:, n * D_MODEL:(n + 1) * D_MODEL] = _sigmoid(gl).astype(BF16)

    kvab = mm(kvo, kvo + 512)
    ka_ref[...] = _norm_rope(kvab[:, 0:128], m64, gain(3), c64, s64, 32).astype(BF16)
    va_ref[...] = kvab[:, 128:256].astype(BF16)
    kb_ref[...] = _norm_rope(kvab[:, 256:384], m64, gain(4), c64, s64, 32).astype(BF16)
    vb_ref[...] = kvab[:, 384:512].astype(BF16)
    kc_ref[...] = _norm_rope(mm(kvo + 512, kvo + 1024), m64, gain(5), c64, s64, 32).astype(BF16)
    vc_ref[...] = mm(kvo + 1024, kvo + 1536).astype(BF16)
    ckv = mm(kvo + 1536, kvo + 1792)
    ckvn = (ckv * lax.rsqrt(jnp.mean(ckv * ckv, axis=-1, keepdims=True) + EPS)
            * vec_ref[1:2, Q_LORA:Q_LORA + KV_LORA])
    kvd = _dot(ckvn.astype(BF16), wukv_ref[...])
    kr = _norm_rope(mm(kvo + 1792, kvo + 1920), md, gain(8), cd, sd, 16)
    kn = _norm_rope(kvd[:, 0:D_HEADS * LANES], md, gain(7), None, None, 0)
    kd_ref[...] = (kn + jnp.concatenate([kr] * D_HEADS, axis=1)).astype(BF16)
    vd_ref[...] = kvd[:, D_HEADS * LANES:].astype(BF16)


_KV_WIDTHS = (128, 128, 128, 128, 512, 512, D_HEADS * LANES, D_HEADS * V_HEAD)


def _project(x, mod, vec, gains, rope, mavg, w, wuq, wukv, *, seq, row_blocks, row_off,
             with_q, kv_in=None):
    n = x.shape[0]
    nb = n // seq
    tps = seq // TM
    grid = (n // TM,)

    def tok(i):
        return (i, 0)

    def kvrow(i):
        return ((i // tps) * row_blocks + row_off + i % tps, 0)

    in_specs = [
        pl.BlockSpec((TM, D_MODEL), tok),
        pl.BlockSpec((1, 6, D_MODEL), lambda i: (i // tps, 0, 0)),
        _resident(vec.shape),
        _resident(gains.shape),
        pl.BlockSpec((4, TM, LANES), lambda i: (0, i % tps, 0)),
        _resident(mavg.shape),
        _resident(w.shape),
        _resident(wuq.shape),
        _resident(wukv.shape),
    ]
    args = [x, mod, vec, gains, rope, mavg, w, wuq, wukv]
    kv_shapes = [jax.ShapeDtypeStruct((nb * KV_ROWS, wd), BF16) for wd in _KV_WIDTHS]
    kv_specs = [pl.BlockSpec((TM, wd), kvrow) for wd in _KV_WIDTHS]
    aliases = {}
    if kv_in is not None:
        for j, a in enumerate(kv_in):
            aliases[len(args)] = j
            args.append(a)
            in_specs.append(pl.BlockSpec(memory_space=pl.ANY))
    out_shapes = list(kv_shapes)
    out_specs = list(kv_specs)
    if with_q:
        for wd in (512, 512, 512, D_HEADS * LANES, GATE_COLS):
            out_shapes.append(jax.ShapeDtypeStruct((n, wd), BF16))
            out_specs.append(pl.BlockSpec((TM, wd), tok))
    return pl.pallas_call(
        functools.partial(_proj_kernel, with_q),
        out_shape=out_shapes, grid=grid, in_specs=in_specs, out_specs=out_specs,
        input_output_aliases=aliases,
        compiler_params=_cparams(("arbitrary",)),
        name="project_q" if with_q else "project_kv",
    )(*args)


LOGIT_SAFE = 40.0


def _softmax_pv(shift, q, k, v, mask=None, extra_logit=None):
    s = _dot_nt(q, k)
    if mask is not None:
        s = jnp.where(mask, s, NEG_INF)
    if shift:
        m = jnp.max(s, axis=-1, keepdims=True)
        if extra_logit is not None:
            m = jnp.maximum(m, extra_logit)
            extra_logit = extra_logit - m
        s = s - m
    p = jnp.exp2(s)
    l = jnp.sum(p, axis=-1, keepdims=True)
    if extra_logit is not None:
        l = l + jnp.exp2(extra_logit)
    return _dot(p.astype(BF16), v) / l


def _guarded(body, idx, safe_ref, *refs):
    @pl.when(safe_ref[idx] != 0)
    def _():
        body(False, *refs)

    @pl.when(safe_ref[idx] == 0)
    def _():
        body(True, *refs)


def _split_heads(qs):
    lane = lax.broadcasted_iota(jnp.int32, qs.shape, 1)
    zero = jnp.zeros_like(qs)
    return jnp.concatenate([jnp.where(lane < HD, qs, zero), jnp.where(lane >= HD, qs, zero)], axis=0)


def _merge_halves(o, tq):
    lane = lax.broadcasted_iota(jnp.int32, (tq, LANES), 1)
    return jnp.where(lane < HD, o[:tq], o[tq:])


def _mix_b(shift, q_ref, k_ref, v_ref, o_ref):
    tq = q_ref.shape[0]
    for j in range(4):
        qq = _split_heads(q_ref[:, j * LANES:(j + 1) * LANES])
        o = _softmax_pv(shift, qq, k_ref[...], v_ref[...])
        o_ref[:, j * LANES:(j + 1) * LANES] = _merge_halves(o, tq).astype(BF16)


def _mix_a(local, seq, shift, sink_ref, q_ref, k_ref, v_ref, o_ref):
    tq = q_ref.shape[0]
    nkeys = k_ref.shape[0]
    if local:
        span = tq + 2 * WINDOW
        q0 = pl.program_id(1) * tq
        start = pl.multiple_of(jnp.clip(q0 - WINDOW, 0, seq - span), BLOCK)
        k = jnp.concatenate([k_ref[pl.ds(start, span), :], k_ref[seq:nkeys, :]], axis=0)
        v = jnp.concatenate([v_ref[pl.ds(start, span), :], v_ref[seq:nkeys, :]], axis=0)
        nk = k.shape[0]
        row = lax.broadcasted_iota(jnp.int32, (2 * tq, nk), 0)
        col = lax.broadcasted_iota(jnp.int32, (2 * tq, nk), 1)
        qpos = q0 + jnp.where(row >= tq, row - tq, row)
        dist = (start + col) - qpos
        valid = (col >= span) | ((dist >= -WINDOW) & (dist <= WINDOW))
    else:
        k = k_ref[...]
        v = v_ref[...]
        valid = None
    rows = lax.broadcasted_iota(jnp.int32, (2 * tq, 1), 0)
    for j in range(4):
        qq = _split_heads(q_ref[:, j * LANES:(j + 1) * LANES])
        sink = jnp.where(rows < tq, sink_ref[2 * j], sink_ref[2 * j + 1]) * LOG2E
        o = _softmax_pv(shift, qq, k, v, mask=valid, extra_logit=sink)
        o_ref[:, j * LANES:(j + 1) * LANES] = _merge_halves(o, tq).astype(BF16)


def _mix_c(lam_init, shift, lq1_ref, lk1_ref, lq2_ref, lk2_ref, subln_ref, q_ref, k_ref, v_ref, o_ref):
    tq = q_ref.shape[0]
    lam = (jnp.exp(jnp.sum(lq1_ref[...] * lk1_ref[...], axis=-1, keepdims=True))
           - jnp.exp(jnp.sum(lq2_ref[...] * lk2_ref[...], axis=-1, keepdims=True)) + lam_init)
    for c in range(C_HEADS):
        qq = _split_heads(q_ref[:, c * LANES:(c + 1) * LANES])
        o = _softmax_pv(shift, qq, k_ref[:, c * LANES:(c + 1) * LANES],
                        v_ref[:, c * LANES:(c + 1) * LANES])
        oc = o[:tq] - lam * o[tq:]
        oc = oc * lax.rsqrt(jnp.mean(oc * oc, axis=-1, keepdims=True) + EPS) * subln_ref[...]
        o_ref[:, c * LANES:(c + 1) * LANES] = (oc * (1.0 - lam_init)).astype(BF16)


def _mix_d(shift, q_ref, k_ref, v_ref, o_ref):
    tq = q_ref.shape[0]
    for hp in range(D_HEADS // 2):
        v = v_ref[:, hp * LANES:(hp + 1) * LANES]
        halves = [_softmax_pv(shift, q_ref[:, h * LANES:(h + 1) * LANES],
                              k_ref[:, h * LANES:(h + 1) * LANES], v)
                  for h in (2 * hp, 2 * hp + 1)]
        o = jnp.concatenate(halves, axis=0)
        o_ref[:, hp * LANES:(hp + 1) * LANES] = _merge_halves(o, tq).astype(BF16)


ATTN_TQ = 256


def _attention(body, safe_idx, safe, q, k, v, extra, extra_specs, *, seq_q, ctx_only, name):
    n = q.shape[0]
    nb = n // seq_q
    tq = ATTN_TQ
    nq = seq_q // tq
    if ctx_only:
        kblk = KV_ROWS // CTX_LEN
        kspec = lambda w: pl.BlockSpec((CTX_LEN, w), lambda b, i: (b * kblk + kblk - 1, 0))
    else:
        kspec = lambda w: pl.BlockSpec((KV_ROWS, w), lambda b, i: (b, 0))
    tok = lambda b, i: (b * nq + i, 0)
    return pl.pallas_call(
        functools.partial(_guarded, body, safe_idx),
        out_shape=jax.ShapeDtypeStruct((n, BRANCH_W), BF16),
        grid=(nb, nq),
        in_specs=[pl.BlockSpec(memory_space=pltpu.SMEM)] + list(extra_specs)
                 + [pl.BlockSpec((tq, q.shape[1]), tok), kspec(k.shape[1]), kspec(v.shape[1])],
        out_specs=pl.BlockSpec((tq, BRANCH_W), tok),
        compiler_params=_cparams(("arbitrary", "arbitrary")),
        name=name,
    )(safe, *extra, q, k, v)


def _mix_ab(local, seq, shift, sink_ref, qa_ref, qb_ref, ka_ref, va_ref, kb_ref, vb_ref, oa_ref, ob_ref):
    _mix_a(local, seq, shift, sink_ref, qa_ref, ka_ref, va_ref, oa_ref)
    _mix_b(shift, qb_ref, kb_ref, vb_ref, ob_ref)


def _attention_ab(safe, qa, qb, ka, va, kb, vb, sink, *, seq_q, ctx_only, name):
    n = qa.shape[0]
    nb = n // seq_q
    tq = ATTN_TQ
    nq = seq_q // tq
    if ctx_only:
        kblk = KV_ROWS // CTX_LEN
        kspec = pl.BlockSpec((CTX_LEN, LANES), lambda b, i: (b * kblk + kblk - 1, 0))
    else:
        kspec = pl.BlockSpec((KV_ROWS, LANES), lambda b, i: (b, 0))
    tok = pl.BlockSpec((tq, BRANCH_W), lambda b, i: (b * nq + i, 0))
    smem = pl.BlockSpec(memory_space=pltpu.SMEM)
    return pl.pallas_call(
        functools.partial(_guarded, functools.partial(_mix_ab, not ctx_only, KV_ROWS - CTX_LEN), 0),
        out_shape=[jax.ShapeDtypeStruct((n, BRANCH_W), BF16)] * 2,
        grid=(nb, nq),
        in_specs=[smem, smem, tok, tok] + [kspec] * 4,
        out_specs=[tok, tok],
        compiler_params=_cparams(("arbitrary", "arbitrary")),
        name=name,
    )(safe, sink, qa, qb, ka, va, kb, vb)


def _mixers(qs, kvs, lp, *, seq_q, ctx_only, tag):
    qa, qb, qc, qd = qs
    ka, va, kb, vb, kc, vc, kd, vd = kvs
    safe = lp["safe"]
    row = lambda w: pl.BlockSpec((1, w), lambda b, i: (0, 0))
    common = dict(seq_q=seq_q, ctx_only=ctx_only)
    oa, ob = _attention_ab(safe, qa, qb, ka, va, kb, vb, lp["sink"], name="attn_ab_" + tag, **common)
    oc = _attention(functools.partial(_mix_c, lp["lam_init"]), 1, safe, qc, kc, vc,
                    [lp["lq1"], lp["lk1"], lp["lq2"], lp["lk2"], lp["subln"]],
                    [row(HD)] * 4 + [row(2 * HD)], name="attn_c_" + tag, **common)
    od = _attention(_mix_d, 2, safe, qd, kd, vd, [], [], name="attn_d_" + tag, **common)
    return oa, ob, oc, od


def _merge_kernel(moe, oa_ref, ob_ref, oc_ref, od_ref, g_ref, x_ref, mod_ref, norm_ref,
                  wbr_ref, wout_ref, *refs):
    if moe:
        wr_ref, xo_ref, h_ref, lg_ref = refs
    else:
        xo_ref, h_ref = refs
    m = mod_ref[0]
    y = None
    for n, o_ref in enumerate((oa_ref, ob_ref, oc_ref, od_ref)):
        yn = g_ref[:, n * D_MODEL:(n + 1) * D_MODEL].astype(F32) * _dot(o_ref[...], wbr_ref[n])
        y = yn if y is None else y + yn
    z = _dot(y.astype(BF16), wout_ref[...])
    xn = x_ref[...] + m[2:3] * z
    xo_ref[...] = xn
    h = _adaln(xn, norm_ref[...], m[3:4], m[4:5])
    if moe:
        h_ref[...] = h
        h_hi = h.astype(BF16)
        h_lo = (h - h_hi.astype(F32)).astype(BF16)
        lg_ref[...] = (_dot_nt(wr_ref[0], h_hi) + _dot_nt(wr_ref[1], h_hi)) + _dot_nt(wr_ref[0], h_lo)
    else:
        h_ref[...] = h.astype(BF16)


def _merge(branches, g, x, mod, norm, wbr, wout, router_t, *, seq):
    n = x.shape[0]
    tm = min(MERGE_TM, seq)
    tps = seq // tm
    moe = router_t is not None
    tok = lambda i: (i, 0)
    in_specs = [pl.BlockSpec((tm, BRANCH_W), tok)] * 4 + [
        pl.BlockSpec((tm, GATE_COLS), tok),
        pl.BlockSpec((tm, D_MODEL), tok),
        pl.BlockSpec((1, 6, D_MODEL), lambda i: (i // tps, 0, 0)),
        _resident(norm.shape), _resident(wbr.shape), _resident(wout.shape)]
    args = list(branches) + [g, x, mod, norm, wbr, wout]
    out_shape = [jax.ShapeDtypeStruct((n, D_MODEL), F32),
                 jax.ShapeDtypeStruct((n, D_MODEL), F32 if moe else BF16)]
    out_specs = [pl.BlockSpec((tm, D_MODEL), tok), pl.BlockSpec((tm, D_MODEL), tok)]
    if moe:
        in_specs.append(_resident(router_t.shape))
        args.append(router_t)
        out_shape.append(jax.ShapeDtypeStruct((N_EXPERTS, n), F32))
        out_specs.append(pl.BlockSpec((N_EXPERTS, tm), lambda i: (0, i)))
    return pl.pallas_call(
        functools.partial(_merge_kernel, moe),
        out_shape=out_shape, grid=(n // tm,), in_specs=in_specs, out_specs=out_specs,
        compiler_params=_cparams(("arbitrary",)),
        name="merge_moe" if moe else "merge_dense",
    )(*args)


def _ffn_kernel(h_ref, x_ref, mod_ref, wg_ref, wu_ref, wd_ref, o_ref, hid_ref):
    h = h_ref[...]
    for c in range(0, D_FF, FFN_TF):
        a = _dot(h, wg_ref[:, c:c + FFN_TF])
        u = _dot(h, wu_ref[:, c:c + FFN_TF])
        hid_ref[:, c:c + FFN_TF] = (a * _sigmoid(a) * u).astype(BF16)
    m = mod_ref[0]
    o_ref[...] = x_ref[...] + m[5:6] * _dot(hid_ref[...], wd_ref[...])


def _dense_ffn(h, x, mod, wg, wu, wd, *, seq):
    n = x.shape[0]
    tm = min(FFN_TM, seq)
    tps = seq // tm
    tok = lambda i: (i, 0)
    return pl.pallas_call(
        _ffn_kernel,
        out_shape=jax.ShapeDtypeStruct((n, D_MODEL), F32),
        grid=(n // tm,),
        in_specs=[pl.BlockSpec((tm, D_MODEL), tok), pl.BlockSpec((tm, D_MODEL), tok),
                  pl.BlockSpec((1, 6, D_MODEL), lambda i: (i // tps, 0, 0)),
                  _resident(wg.shape), _resident(wu.shape), _resident(wd.shape)],
        out_specs=pl.BlockSpec((tm, D_MODEL), tok),
        scratch_shapes=[pltpu.VMEM((tm, D_FF), BF16)],
        compiler_params=_cparams(("arbitrary",)),
        name="dense_ffn",
    )(h, x, mod, wg, wu, wd)


ROUTE_CHUNK = 512
DMA_UNROLL = 8


def _route_kernel(n, lg_ref, tri_ref, dest_ref, wts_ref, meta_ref, mem_ref, pos_ref):
    lg = lg_ref[...]
    eidx = lax.broadcasted_iota(jnp.int32, lg.shape, 0).astype(F32)
    none = float(N_EXPERTS)
    m1 = jnp.max(lg, axis=0, keepdims=True)
    i1 = jnp.min(jnp.where(lg == m1, eidx, none), axis=0, keepdims=True)
    lg2 = jnp.where(eidx == i1, -jnp.inf, lg)
    m2 = jnp.max(lg2, axis=0, keepdims=True)
    i2 = jnp.min(jnp.where(lg2 == m2, eidx, none), axis=0, keepdims=True)
    e = jnp.exp(m2 - m1)
    w1 = 1.0 / (1.0 + e)
    wts_ref[0:1, :] = w1
    wts_ref[1:2, :] = e * w1
    mem_ref[...] = jnp.where(eidx == i1, 1.0, 0.0) + jnp.where(eidx == i2, 1.0, 0.0)

    carry = jnp.zeros((N_EXPERTS, 1), F32)
    tri = tri_ref[...]
    for c in range(0, n, ROUTE_CHUNK):
        mc = mem_ref[:, c:c + ROUTE_CHUNK]
        pos_ref[:, c:c + ROUTE_CHUNK] = carry + _dot(mc.astype(BF16), tri)
        carry = carry + jnp.sum(mc, axis=1, keepdims=True)
    padded = jnp.floor((carry + (MOE_TM - 1.0)) * (1.0 / MOE_TM)) * MOE_TM
    esub = lax.broadcasted_iota(jnp.int32, (N_EXPERTS, 1), 0)
    pstart = jnp.zeros((N_EXPERTS, 1), F32)
    for k in range(N_EXPERTS - 1):
        pstart = pstart + jnp.where(esub > k, padded[k:k + 1, :], 0.0)
    pend = pstart + padded
    slot = pstart + pos_ref[...]
    dest_ref[0:1, :] = jnp.sum(jnp.where(eidx == i1, slot, 0.0), axis=0, keepdims=True).astype(jnp.int32)
    dest_ref[1:2, :] = jnp.sum(jnp.where(eidx == i2, slot, 0.0), axis=0, keepdims=True).astype(jnp.int32)

    nblk = pend[N_EXPERTS - 1:N_EXPERTS, :] * (1.0 / MOE_TM)
    brow = jnp.minimum(lax.broadcasted_iota(jnp.int32, (1, LANES), 1).astype(F32), nblk - 1.0) * MOE_TM
    be = jnp.zeros((1, LANES), F32)
    for k in range(N_EXPERTS):
        be = be + jnp.where(pend[k:k + 1, :] <= brow, 1.0, 0.0)
    meta_ref[...] = jnp.zeros(meta_ref.shape, jnp.int32)
    meta_ref[0:1, :] = jnp.minimum(be, N_EXPERTS - 1.0).astype(jnp.int32)
    meta_ref[1:2, :] = jnp.broadcast_to(nblk, (1, LANES)).astype(jnp.int32)


def _route(logits_t):
    n = logits_t.shape[1]
    nblk_max = n * TOP_K // MOE_TM + N_EXPERTS
    assert nblk_max <= LANES and n % ROUTE_CHUNK == 0
    tri = jnp.asarray(np.triu(np.ones((ROUTE_CHUNK, ROUTE_CHUNK)), 1), BF16)
    whole = lambda shape: pl.BlockSpec(shape, lambda: (0,) * len(shape))
    dest, wts, meta = pl.pallas_call(
        functools.partial(_route_kernel, n),
        out_shape=[jax.ShapeDtypeStruct((TOP_K, n), jnp.int32),
                   jax.ShapeDtypeStruct((TOP_K, n), F32),
                   jax.ShapeDtypeStruct((8, LANES), jnp.int32)],
        in_specs=[whole(logits_t.shape), whole(tri.shape)],
        out_specs=[whole((TOP_K, n)), whole((TOP_K, n)), whole((8, LANES))],
        scratch_shapes=[pltpu.VMEM((N_EXPERTS, n), F32), pltpu.VMEM((N_EXPERTS, n), F32)],
        compiler_params=pltpu.CompilerParams(vmem_limit_bytes=VMEM_LIMIT),
        name="moe_route",
    )(logits_t, tri)
    return dest, wts, meta[0, :nblk_max], meta[1, :1], nblk_max


def _dispatch_kernel(tg, dest_ref, h_ref, xin_hbm, xbuf_hbm, sem):
    del xin_hbm

    def copies(r):
        src = h_ref.at[pl.ds(r, 1)]
        return (pltpu.make_async_copy(src, xbuf_hbm.at[pl.ds(dest_ref[0, 0, r], 1)], sem),
                pltpu.make_async_copy(src, xbuf_hbm.at[pl.ds(dest_ref[0, 0, tg + r], 1)], sem))

    def start(r, carry):
        for k, cp in enumerate(copies(r)):
            cp.start(priority=k)
        return carry

    def wait(r, carry):
        for cp in copies(r):
            cp.wait()
        return carry

    lax.fori_loop(0, tg, start, 0, unroll=DMA_UNROLL)
    lax.fori_loop(0, tg, wait, 0, unroll=DMA_UNROLL)


def _tile_dest(dest, t):
    n = dest.shape[1]
    return jnp.transpose(dest.reshape(TOP_K, n // t, t), (1, 0, 2)).reshape(n // t, 1, TOP_K * t)


def _dispatch(h, dest, nrows):
    n = h.shape[0]
    tg = 512
    nt = n // tg
    dest_t = _tile_dest(dest, tg)
    xbuf0 = jnp.zeros((nrows, D_MODEL), F32)
    return pl.pallas_call(
        functools.partial(_dispatch_kernel, tg),
        out_shape=jax.ShapeDtypeStruct((nrows, D_MODEL), F32),
        grid=(nt,),
        in_specs=[pl.BlockSpec((1, 1, TOP_K * tg), lambda i: (i, 0, 0), memory_space=pltpu.SMEM),
                  pl.BlockSpec((tg, D_MODEL), lambda i: (i, 0)), pl.BlockSpec(memory_space=pl.ANY)],
        out_specs=pl.BlockSpec(memory_space=pl.ANY),
        scratch_shapes=[pltpu.SemaphoreType.DMA(())],
        input_output_aliases={2: 0},
        compiler_params=_cparams(("arbitrary",)),
        name="moe_dispatch",
    )(dest_t, h, xbuf0)


def _moe_kernel(nf, blk_e_ref, nblk_ref, x_ref, w1_ref, w3_ref, w2_ref, o_ref, xb_ref, acc_ref):
    del blk_e_ref
    i = pl.program_id(0)
    j = pl.program_id(1)

    @pl.when(i < nblk_ref[0])
    def _():
        @pl.when(j == 0)
        def _():
            xb_ref[...] = x_ref[...].astype(BF16)
            acc_ref[...] = jnp.zeros_like(acc_ref)

        xb = xb_ref[...]
        a = _dot(xb, w1_ref[0])
        u = _dot(xb, w3_ref[0])
        acc_ref[...] += _dot((a * _sigmoid(a) * u).astype(BF16), w2_ref[0])

        @pl.when(j == nf - 1)
        def _():
            o_ref[...] = acc_ref[...]

    @pl.when((i >= nblk_ref[0]) & (j == nf - 1))
    def _():
        o_ref[...] = jnp.zeros_like(o_ref)


def _moe_ffn(xbuf, blk_e, nblk, nblk_max, w1, w3, w2):
    nf = EXPERT_FF // MOE_TF

    def rows(i, j, be, nb):
        return (i, 0)

    def jj(i, j, nb):
        return jnp.where(i < nb[0], j, nf - 1)

    grid_spec = pltpu.PrefetchScalarGridSpec(
        num_scalar_prefetch=2, grid=(nblk_max, nf),
        in_specs=[pl.BlockSpec((MOE_TM, D_MODEL), rows),
                  pl.BlockSpec((1, D_MODEL, MOE_TF), lambda i, j, be, nb: (be[i], 0, jj(i, j, nb))),
                  pl.BlockSpec((1, D_MODEL, MOE_TF), lambda i, j, be, nb: (be[i], 0, jj(i, j, nb))),
                  pl.BlockSpec((1, MOE_TF, D_MODEL), lambda i, j, be, nb: (be[i], jj(i, j, nb), 0))],
        out_specs=pl.BlockSpec((MOE_TM, D_MODEL), rows),
        scratch_shapes=[pltpu.VMEM((MOE_TM, D_MODEL), BF16), pltpu.VMEM((MOE_TM, D_MODEL), F32)])
    return pl.pallas_call(
        functools.partial(_moe_kernel, nf),
        out_shape=jax.ShapeDtypeStruct(xbuf.shape, F32),
        grid_spec=grid_spec,
        compiler_params=_cparams(("arbitrary", "arbitrary")),
        name="moe_experts",
    )(blk_e, nblk, xbuf, w1, w3, w2)


def _combine_kernel(tc, dest_ref, wts_ref, x_ref, mod_ref, y_hbm, o_ref, yg_ref, sem):
    def copies(r):
        return (pltpu.make_async_copy(y_hbm.at[pl.ds(dest_ref[0, 0, r], 1)],
                                      yg_ref.at[0, pl.ds(r, 1)], sem),
                pltpu.make_async_copy(y_hbm.at[pl.ds(dest_ref[0, 0, tc + r], 1)],
                                      yg_ref.at[1, pl.ds(r, 1)], sem))

    def start(r, carry):
        for k, cp in enumerate(copies(r)):
            cp.start(priority=k)
        return carry

    def wait(r, carry):
        for cp in copies(r):
            cp.wait()
        return carry

    lax.fori_loop(0, tc, start, 0, unroll=DMA_UNROLL)
    lax.fori_loop(0, tc, wait, 0, unroll=DMA_UNROLL)
    m = mod_ref[0]
    w = wts_ref[...]
    y = w[:, 0:1] * yg_ref[0] + w[:, 1:2] * yg_ref[1]
    o_ref[...] = x_ref[...] + m[5:6] * y


def _combine(ybuf, dest, wts, x, mod, *, seq):
    n = x.shape[0]
    tc = 256
    nt = n // tc
    tps = seq // tc
    dest_t = _tile_dest(dest, tc)
    wts = wts.T
    tok = lambda i: (i, 0)
    return pl.pallas_call(
        functools.partial(_combine_kernel, tc),
        out_shape=jax.ShapeDtypeStruct((n, D_MODEL), F32),
        grid=(nt,),
        in_specs=[pl.BlockSpec((1, 1, TOP_K * tc), lambda i: (i, 0, 0), memory_space=pltpu.SMEM),
                  pl.BlockSpec((tc, TOP_K), tok),
                  pl.BlockSpec((tc, D_MODEL), tok),
                  pl.BlockSpec((1, 6, D_MODEL), lambda i: (i // tps, 0, 0)),
                  pl.BlockSpec(memory_space=pl.ANY)],
        out_specs=pl.BlockSpec((tc, D_MODEL), tok),
        scratch_shapes=[pltpu.VMEM((TOP_K, tc, D_MODEL), F32), pltpu.SemaphoreType.DMA(())],
        compiler_params=_cparams(("arbitrary",)),
        name="moe_combine",
    )(dest_t, wts, x, mod, ybuf)


def _head_cols(perm):
    return np.concatenate([np.arange(p * HD, (p + 1) * HD) for p in perm])


def _prep_w_in(w):
    hp = _head_cols(HEAD_PERM)
    kv_lo = Q_COLS
    kr_lo = kv_lo + KV_PAD_COLS - LANES
    parts = [w[:, 0:512][:, hp], w[:, 512:1024][:, hp], w[:, 1024:Q_COLS],
             w[:, kv_lo:kr_lo],
             jnp.pad(w[:, kr_lo:kr_lo + QK_ROPE], ((0, 0), (QK_NOPE, LANES - QK_NOPE - QK_ROPE))),
             w[:, kr_lo + QK_ROPE:]]
    return jnp.concatenate(parts, axis=1).astype(BF16)


def _prep_layer(l, p):
    hp = _head_cols(HEAD_PERM)
    zeros = lambda k: jnp.zeros((k,), F32)
    sc = LOG2E / math.sqrt(HD)
    sc_d = LOG2E / math.sqrt(QK_NOPE + QK_ROPE)
    wuq = p["d_w_uq"][l].reshape(Q_LORA, D_HEADS, QK_NOPE + QK_ROPE)
    wuq = jnp.pad(wuq, ((0, 0), (0, 0), (0, LANES - QK_NOPE - QK_ROPE))).reshape(Q_LORA, D_HEADS * LANES)
    wukv = p["d_w_ukv"][l].reshape(KV_LORA, D_HEADS, QK_NOPE + V_HEAD)
    wuk = jnp.pad(wukv[:, :, :QK_NOPE], ((0, 0), (0, 0), (0, LANES - QK_NOPE))).reshape(KV_LORA, D_HEADS * LANES)
    wuv = wukv[:, :, QK_NOPE:].reshape(KV_LORA, D_HEADS * V_HEAD)
    gains = jnp.stack([
        jnp.tile(p["a_qn"][l], 2) * sc, jnp.tile(p["b_qn"][l], 2) * sc, jnp.tile(p["c_qn"][l], 2) * sc,
        jnp.tile(p["a_kn"][l], 2), jnp.tile(p["b_kn"][l], 2), jnp.tile(p["c_kn"][l], 2),
        jnp.concatenate([p["d_qn_nope"][l], p["d_qn_rope"][l], zeros(32)]) * sc_d,
        jnp.concatenate([p["d_kn_nope"][l], zeros(64)]),
        jnp.concatenate([zeros(64), p["d_kn_rope"][l], zeros(32)]),
    ] + [zeros(LANES)] * 7).astype(F32)
    vec = jnp.stack([p["mix_norm"][l],
                     jnp.concatenate([p["d_q_norm"][l], p["d_kv_norm"][l], zeros(D_MODEL - Q_LORA - KV_LORA)])]
                    + [zeros(D_MODEL)] * 6).astype(F32)
    wbr = p["w_br"][l]
    wbr = jnp.stack([wbr[0][hp], wbr[1][hp], wbr[2], wbr[3]]).astype(BF16)

    amax = lambda v: jnp.max(jnp.abs(v))
    bound64 = lambda gq, gk: 1.02 * HD * amax(gq) * amax(gk)
    nq_d = jnp.sqrt(QK_NOPE * amax(gains[6, :QK_NOPE]) ** 2 + QK_ROPE * amax(gains[6, QK_NOPE:]) ** 2)
    nk_d = jnp.sqrt(QK_NOPE * amax(gains[7]) ** 2 + QK_ROPE * amax(gains[8]) ** 2)
    bounds = jnp.stack([
        jnp.maximum(jnp.maximum(bound64(gains[0], gains[3]), bound64(gains[1], gains[4])),
                    LOG2E * amax(p["a_sink"][l])),
        bound64(gains[2], gains[5]),
        1.02 * nq_d * nk_d])
    safe = (bounds <= LOGIT_SAFE).astype(jnp.int32)
    return {
        "safe": safe,
        "w_in": _prep_w_in(p["w_in"][l]),
        "wuq": wuq.astype(BF16),
        "wukv": jnp.concatenate([wuk, wuv], axis=1).astype(BF16),
        "gains": gains, "vec": vec, "wbr": wbr, "wout": p["w_out"][l].astype(BF16),
        "ffn_norm": p["ffn_norm"][l].reshape(1, D_MODEL),
        "sink": p["a_sink"][l][np.asarray(HEAD_PERM)].astype(F32),
        "lq1": p["c_lq1"][l].reshape(1, HD), "lk1": p["c_lk1"][l].reshape(1, HD),
        "lq2": p["c_lq2"][l].reshape(1, HD), "lk2": p["c_lk2"][l].reshape(1, HD),
        "subln": p["c_subln"][l].reshape(1, 2 * HD),
        "lam_init": 0.8 - 0.6 * math.exp(-0.3 * l),
    }


def _group_avg_mats():
    m64 = np.kron(np.eye(4), np.full((HD, HD), 1.0 / HD))
    one = np.zeros((LANES, LANES))
    one[:QK_NOPE, :QK_NOPE] = 1.0 / QK_NOPE
    one[QK_NOPE:QK_NOPE + QK_ROPE, QK_NOPE:QK_NOPE + QK_ROPE] = 1.0 / QK_ROPE
    md = np.kron(np.eye(2), one)
    return jnp.asarray(np.stack([m64, md]), BF16)


def _rope_tables(seq, rotary):
    if not rotary:
        one = jnp.ones((seq, LANES), F32)
        zero = jnp.zeros((seq, LANES), F32)
        return jnp.stack([one, zero, one, zero])
    t = jnp.arange(seq)

    def angles(rot_dim):
        nfreq = rot_dim // 4
        inv = jnp.power(ROPE_THETA, -jnp.arange(nfreq, dtype=F32) / nfreq)
        return jnp.concatenate([(t // GRID_W).astype(F32)[:, None] * inv,
                                (t % GRID_W).astype(F32)[:, None] * inv], axis=-1)

    ah = angles(HD)
    c64 = jnp.tile(jnp.cos(ah), (1, 4))
    s64 = jnp.tile(jnp.concatenate([-jnp.sin(ah), jnp.sin(ah)], axis=1), (1, 2))
    ar = angles(QK_ROPE)
    one = jnp.ones((seq, 1), F32)
    cd = jnp.concatenate([one * jnp.ones((1, QK_NOPE), F32), jnp.cos(ar), jnp.cos(ar),
                          one * jnp.ones((1, LANES - QK_NOPE - QK_ROPE), F32)], axis=1)
    sd = jnp.concatenate([jnp.zeros((seq, QK_NOPE), F32), -jnp.sin(ar), jnp.sin(ar),
                          jnp.zeros((seq, LANES - QK_NOPE - QK_ROPE), F32)], axis=1)
    return jnp.stack([c64, s64, cd, sd])


def kernel(x, c, ctx, c_ctx, w_mod, b_mod, mix_norm, ffn_norm, w_in, a_qn, a_kn, a_sink, b_qn, b_kn, c_qn, c_kn, c_lq1, c_lk1, c_lq2, c_lk2, c_subln, d_q_norm, d_kv_norm, d_w_uq, d_w_ukv, d_qn_nope, d_kn_nope, d_qn_rope, d_kn_rope, w_br, w_out, ff_w_gate, ff_w_up, ff_w_down, moe_router, moe_w1, moe_w3, moe_w2):
    p = dict(mix_norm=mix_norm, ffn_norm=ffn_norm, w_in=w_in, a_qn=a_qn, a_kn=a_kn, a_sink=a_sink,
             b_qn=b_qn, b_kn=b_kn, c_qn=c_qn, c_kn=c_kn, c_lq1=c_lq1, c_lk1=c_lk1, c_lq2=c_lq2,
             c_lk2=c_lk2, c_subln=c_subln, d_q_norm=d_q_norm, d_kv_norm=d_kv_norm, d_w_uq=d_w_uq,
             d_w_ukv=d_w_ukv, d_qn_nope=d_qn_nope, d_kn_nope=d_kn_nope, d_qn_rope=d_qn_rope,
             d_kn_rope=d_kn_rope, w_br=w_br, w_out=w_out)
    nb, seq, _ = x.shape
    assert seq + CTX_LEN == KV_ROWS and ctx.shape[1] == CTX_LEN and nb <= 15
    depth = w_mod.shape[0]
    lat = x.reshape(nb * seq, D_MODEL)
    cx = ctx.reshape(nb * CTX_LEN, D_MODEL)

    cond = jnp.zeros((16, D_MODEL), F32).at[:nb].set(c).at[nb].set(c_ctx)
    mod_all = _modulation(cond, w_mod, b_mod).reshape(depth, 16, 6, D_MODEL)
    rope_lat = _rope_tables(seq, True)
    rope_ctx = _rope_tables(CTX_LEN, False)
    mavg = _group_avg_mats()
    lat_blocks = seq // TM

    for l in range(depth):
        last = l == depth - 1
        lp = _prep_layer(l, p)
        mod_lat = mod_all[l, :nb]
        mod_ctx = jnp.broadcast_to(mod_all[l, nb:nb + 1], (nb, 6, D_MODEL))
        common = (lp["vec"], lp["gains"])
        wts = (lp["wuq"], lp["wukv"])
        kvs = [jnp.zeros((nb * KV_ROWS, wd), BF16) for wd in _KV_WIDTHS]
        if last:
            kvs = _project(cx, mod_ctx, *common, rope_ctx, mavg,
                           lp["w_in"][:, Q_COLS:Q_COLS + KV_PAD_COLS], *wts, seq=CTX_LEN,
                           row_blocks=KV_ROWS // TM, row_off=lat_blocks, with_q=False, kv_in=kvs)
        else:
            outs = _project(cx, mod_ctx, *common, rope_ctx, mavg, lp["w_in"], *wts, seq=CTX_LEN,
                            row_blocks=KV_ROWS // TM, row_off=lat_blocks, with_q=True, kv_in=kvs)
            kvs, q_ctx, g_ctx = outs[:8], outs[8:12], outs[12]
        outs = _project(lat, mod_lat, *common, rope_lat, mavg, lp["w_in"], *wts, seq=seq,
                        row_blocks=KV_ROWS // TM, row_off=0, with_q=True, kv_in=kvs)
        kvs, q_lat, g_lat = outs[:8], outs[8:12], outs[12]
        br_lat = _mixers(q_lat, kvs, lp, seq_q=seq, ctx_only=False, tag="lat")
        moe = l % 2 == 1
        i = l // 2
        router_t = None
        if moe:
            wr = moe_router[i].T
            wr_hi = wr.astype(BF16)
            router_t = jnp.stack([wr_hi, (wr - wr_hi.astype(F32)).astype(BF16)])
        res = _merge(br_lat, g_lat, lat, mod_lat, lp["ffn_norm"], lp["wbr"], lp["wout"], router_t, seq=seq)
        if not last:
            br_ctx = _mixers(q_ctx, kvs, lp, seq_q=CTX_LEN, ctx_only=True, tag="ctx")
            res_ctx = _merge(br_ctx, g_ctx, cx, mod_ctx, lp["ffn_norm"], lp["wbr"], lp["wout"],
                             router_t, seq=CTX_LEN)
        if not moe:
            wg, wu, wd = (ff_w_gate[i].astype(BF16), ff_w_up[i].astype(BF16), ff_w_down[i].astype(BF16))
            lat = _dense_ffn(res[1], res[0], mod_lat, wg, wu, wd, seq=seq)
            if not last:
                cx = _dense_ffn(res_ctx[1], res_ctx[0], mod_ctx, wg, wu, wd, seq=CTX_LEN)
        else:
            w1, w3, w2 = moe_w1[i].astype(BF16), moe_w3[i].astype(BF16), moe_w2[i].astype(BF16)

            def routed(res_t, mod_t, seq_t):
                xr, hr, lg = res_t
                dest, wts_r, blk_e, nblk, nblk_max = _route(lg)
                xbuf = _dispatch(hr, dest, nblk_max * MOE_TM)
                ybuf = _moe_ffn(xbuf, blk_e, nblk, nblk_max, w1, w3, w2)
                return _combine(ybuf, dest, wts_r, xr, mod_t, seq=seq_t)

            lat = routed(res, mod_lat, seq)
            if not last:
                cx = routed(res_ctx, mod_ctx, CTX_LEN)
    return lat.reshape(nb, seq, D_MODEL)
```

```python
import functools
import math

import numpy as np
import jax
import jax.numpy as jnp
from jax import lax
from jax.experimental import pallas as pl
from jax.experimental.pallas import tpu as pltpu

F32 = jnp.float32
BF16 = jnp.bfloat16

D_MODEL = 1024
DEPTH = 2
CTX_LEN = 256
GRID_W = 64
HD = 64
BLOCK = 128
WINDOW = 128
ROPE_THETA = 10000.0
EPS = 1e-6
NEG_INF = -1e30
A_HEADS = 8
C_HEADS = 4
D_HEADS = 8
Q_LORA = 512
KV_LORA = 256
QK_NOPE = 64
QK_ROPE = 32
V_HEAD = 64
BRANCH_W = 512
D_FF = 2816
N_EXPERTS = 8
TOP_K = 2
EXPERT_FF = 3584

LANES = 128
VMEM_LIMIT = 56 * 1024 * 1024
LOG2E = math.log2(math.e)

Q_COLS = 2048
KV_PAD_COLS = 1920
GATE_COLS = 4 * D_MODEL
W_COLS = Q_COLS + KV_PAD_COLS + GATE_COLS
HEAD_PERM = (0, 4, 1, 5, 2, 6, 3, 7)

TM = 512
MERGE_TM = 512
MOE_TM = 512
MOE_TF = 1792
FFN_TM = 512
FFN_TF = 256


def _cparams(sem):
    return pltpu.CompilerParams(dimension_semantics=sem, vmem_limit_bytes=VMEM_LIMIT)


def _resident(shape):
    nd = len(shape)
    return pl.BlockSpec(shape, lambda *_: (0,) * nd, pipeline_mode=pl.Buffered(1))


def _sigmoid(x):
    return 1.0 / (1.0 + jnp.exp(-x))


def _dot(a, b):
    return jnp.dot(a, b, preferred_element_type=F32)


def _dot_nt(a, b):
    return lax.dot_general(a, b, (((1,), (1,)), ((), ())), preferred_element_type=F32)


def _mod_kernel(c_ref, w_ref, b_ref, o_ref):
    cond = c_ref[...]
    s = cond * _sigmoid(cond)
    o_ref[0] = jnp.dot(s, w_ref[0], precision=lax.Precision.HIGHEST,
                       preferred_element_type=F32) + b_ref[0]


def _modulation(cond, w_mod, b_mod):
    depth = w_mod.shape[0]
    nct = 6 * D_MODEL // 1024
    return pl.pallas_call(
        _mod_kernel,
        out_shape=jax.ShapeDtypeStruct((depth, 16, 6 * D_MODEL), F32),
        grid=(depth, nct),
        in_specs=[pl.BlockSpec((16, D_MODEL), lambda l, j: (0, 0)),
                  pl.BlockSpec((1, D_MODEL, 1024), lambda l, j: (l, 0, j)),
                  pl.BlockSpec((1, 1, 1024), lambda l, j: (l, 0, j))],
        out_specs=pl.BlockSpec((1, 16, 1024), lambda l, j: (l, 0, j)),
        compiler_params=_cparams(("arbitrary", "arbitrary")),
        name="modulation",
    )(cond, w_mod, b_mod.reshape(depth, 1, 6 * D_MODEL))


def _adaln(x, g, shift, scale):
    ms = jnp.mean(x * x, axis=-1, keepdims=True)
    y = x * lax.rsqrt(ms + EPS) * g
    return y * (1.0 + scale) + shift


def _group_mean_sq(r, mavg):
    sq = (r * r).astype(BF16)
    w = r.shape[1]
    parts = []
    for c in range(0, w, 2 * LANES):
        cw = min(2 * LANES, w - c)
        parts.append(_dot(sq[:, c:c + cw], mavg[:cw, :cw]))
    return parts[0] if len(parts) == 1 else jnp.concatenate(parts, axis=1)


def _rope_slab(y, cos, sin, half):
    lane = lax.broadcasted_iota(jnp.int32, y.shape, 1)
    first = (lane % (2 * half)) < half
    sw = jnp.where(first, pltpu.roll(y, LANES - half, 1), pltpu.roll(y, half, 1))
    return y * cos + sw * sin


def _norm_rope(r, ms, gain, cos, sin, half):
    outs = []
    for c in range(0, r.shape[1], LANES):
        y = r[:, c:c + LANES] * lax.rsqrt(ms[:, c:c + LANES] + EPS) * gain
        if cos is not None:
            y = _rope_slab(y, cos, sin, half)
        outs.append(y)
    return outs[0] if len(outs) == 1 else jnp.concatenate(outs, axis=1)


def _proj_kernel(with_q, x_ref, mod_ref, vec_ref, gains_ref, rope_ref, mavg_ref,
                 w_ref, wuq_ref, wukv_ref, *refs):
    if with_q:
        (qa_ref, qb_ref, qc_ref, qd_ref, g_ref) = refs[-5:]
        kv_refs = refs[-13:-5]
        kvo, go = Q_COLS, Q_COLS + KV_PAD_COLS
    else:
        kv_refs = refs[-8:]
        kvo = 0
    ka_ref, va_ref, kb_ref, vb_ref, kc_ref, vc_ref, kd_ref, vd_ref = kv_refs

    m = mod_ref[0]
    hb = _adaln(x_ref[...], vec_ref[0:1, :], m[0:1], m[1:2]).astype(BF16)

    def mm(lo, hi):
        return _dot(hb, w_ref[:, lo:hi])

    m64 = mavg_ref[0]
    md = mavg_ref[1]
    c64, s64, cd, sd = rope_ref[0], rope_ref[1], rope_ref[2], rope_ref[3]
    gains = gains_ref[...]

    def gain(i):
        return gains[i:i + 1, :]

    def gates(n):
        gl = mm(go + n * D_MODEL, go + (n + 1) * D_MODEL)
        g_ref[:, n * D_MODEL:(n + 1) * D_MODEL] = _sigmoid(gl).astype(BF16)

    def full_rms(r, g):
        return (r * lax.rsqrt(jnp.mean(r * r, axis=-1, keepdims=True) + EPS) * g).astype(BF16)

    if with_q:
        r_qa, r_qb, r_qc, cq = mm(0, 512), mm(512, 1024), mm(1024, 1536), mm(1536, 2048)
    kvab = mm(kvo, kvo + 512)
    r_kc = mm(kvo + 512, kvo + 1024)
    r_vc = mm(kvo + 1024, kvo + 1536)
    ckv = mm(kvo + 1536, kvo + 1792)
    r_kr = mm(kvo + 1792, kvo + 1920)
    if with_q:
        gates(0)
        gates(1)
        ms_qa, ms_qb, ms_qc = (_group_mean_sq(r, m64) for r in (r_qa, r_qb, r_qc))
        qd = _dot(full_rms(cq, vec_ref[1:2, 0:Q_LORA]), wuq_ref[...])
    r_ka, r_kb = kvab[:, 0:128], kvab[:, 256:384]
    ms_ka, ms_kb, ms_kc = (_group_mean_sq(r, m64) for r in (r_ka, r_kb, r_kc))
    ms_kr = _group_mean_sq(r_kr, md)
    kvd = _dot(full_rms(ckv, vec_ref[1:2, Q_LORA:Q_LORA + KV_LORA]), wukv_ref[...])
    if with_q:
        gates(2)
        ms_qd = _group_mean_sq(qd, md)
    r_kn = kvd[:, 0:D_HEADS * LANES]
    ms_kn = _group_mean_sq(r_kn, md)
    if with_q:
        gates(3)
        qa_ref[...] = _norm_rope(r_qa, ms_qa, gain(0), c64, s64, 32).astype(BF16)
        qb_ref[...] = _norm_rope(r_qb, ms_qb, gain(1), c64, s64, 32).astype(BF16)
        qc_ref[...] = _norm_rope(r_qc, ms_qc, gain(2), c64, s64, 32).astype(BF16)
        qd_ref[...] = _norm_rope(qd, ms_qd, gain(6), cd, sd, 16).astype(BF16)
    ka_ref[...] = _norm_rope(r_ka, ms_ka, gain(3), c64, s64, 32).astype(BF16)
    va_ref[...] = kvab[:, 128:256].astype(BF16)
    kb_ref[...] = _norm_rope(r_kb, ms_kb, gain(4), c64, s64, 32).astype(BF16)
    vb_ref[...] = kvab[:, 384:512].astype(BF16)
    kc_ref[...] = _norm_rope(r_kc, ms_kc, gain(5), c64, s64, 32).astype(BF16)
    vc_ref[...] = r_vc.astype(BF16)
    kr = _norm_rope(r_kr, ms_kr, gain(8), cd, sd, 16)
    kn = _norm_rope(r_kn, ms_kn, gain(7), None, None, 0)
    kd_ref[...] = (kn + jnp.concatenate([kr] * D_HEADS, axis=1)).astype(BF16)
    vd_ref[...] = kvd[:, D_HEADS * LANES:].astype(BF16)


_KV_WIDTHS = (128, 128, 128, 128, 512, 512, D_HEADS * LANES, D_HEADS * V_HEAD)


def _project(x, mod, vec, gains, rope, mavg, w, wuq, wukv, *, seq, tm, with_q):
    n = x.shape[0]
    tps = seq // tm
    grid = (n // tm,)

    def tok(i):
        return (i, 0)

    in_specs = [
        pl.BlockSpec((tm, D_MODEL), tok),
        pl.BlockSpec((1, 6, D_MODEL), lambda i: (i // tps, 0, 0)),
        _resident(vec.shape),
        _resident(gains.shape),
        pl.BlockSpec((4, tm, LANES), lambda i: (0, i % tps, 0)),
        _resident(mavg.shape),
        _resident(w.shape),
        _resident(wuq.shape),
        _resident(wukv.shape),
    ]
    args = [x, mod, vec, gains, rope, mavg, w, wuq, wukv]
    out_shapes = [jax.ShapeDtypeStruct((n, wd), BF16) for wd in _KV_WIDTHS]
    out_specs = [pl.BlockSpec((tm, wd), tok) for wd in _KV_WIDTHS]
    if with_q:
        for wd in (512, 512, 512, D_HEADS * LANES, GATE_COLS):
            out_shapes.append(jax.ShapeDtypeStruct((n, wd), BF16))
            out_specs.append(pl.BlockSpec((tm, wd), tok))
    return pl.pallas_call(
        functools.partial(_proj_kernel, with_q),
        out_shape=out_shapes, grid=grid, in_specs=in_specs, out_specs=out_specs,
        compiler_params=_cparams(("arbitrary",)),
        name="project_q" if with_q else "project_kv",
    )(*args)


LOGIT_SAFE = 40.0


def _softmax_pv(shift, q, kv, mask=None, extra_logit=None):
    scores = [_dot_nt(q, k) for k, _ in kv]
    if mask is not None:
        scores[0] = jnp.where(mask, scores[0], NEG_INF)
    if shift:
        m = functools.reduce(jnp.maximum, [jnp.max(s, axis=-1, keepdims=True) for s in scores])
        if extra_logit is not None:
            m = jnp.maximum(m, extra_logit)
            extra_logit = extra_logit - m
        scores = [s - m for s in scores]
    probs = [jnp.exp2(s) for s in scores]
    l = functools.reduce(jnp.add, [jnp.sum(p, axis=-1, keepdims=True) for p in probs])
    if extra_logit is not None:
        l = l + jnp.exp2(extra_logit)
    o = functools.reduce(jnp.add, [_dot(p.astype(BF16), v) for p, (_, v) in zip(probs, kv)])
    return o / l


def _guarded(body, idx, safe_ref, *refs):
    @pl.when(safe_ref[idx] != 0)
    def _():
        body(False, *refs)

    @pl.when(safe_ref[idx] == 0)
    def _():
        body(True, *refs)


def _split_heads(qs):
    lane = lax.broadcasted_iota(jnp.int32, qs.shape, 1)
    zero = jnp.zeros_like(qs)
    return jnp.concatenate([jnp.where(lane < HD, qs, zero), jnp.where(lane >= HD, qs, zero)], axis=0)


def _merge_halves(o, tq):
    lane = lax.broadcasted_iota(jnp.int32, (tq, LANES), 1)
    return jnp.where(lane < HD, o[:tq], o[tq:])


def _kv_cols(kv_refs, kc, vc):
    return [(k_ref[:, kc * LANES:(kc + 1) * LANES], v_ref[:, vc * LANES:(vc + 1) * LANES])
            for k_ref, v_ref in kv_refs]


def _one_group(groups):
    return [(jnp.concatenate([k for k, _ in groups], axis=0), jnp.concatenate([v for _, v in groups], axis=0))]


def _mix_b(shift, q_ref, kv_refs, o_ref):
    tq = q_ref.shape[0]
    kv = _kv_cols(kv_refs, 0, 0)
    for j in range(4):
        qq = _split_heads(q_ref[:, j * LANES:(j + 1) * LANES])
        o = _softmax_pv(shift, qq, kv)
        o_ref[:, j * LANES:(j + 1) * LANES] = _merge_halves(o, tq).astype(BF16)


def _mix_a(shift, sink_ref, q_ref, kv_refs, o_ref):
    tq = q_ref.shape[0]
    if len(kv_refs) == 2:
        (kl_ref, vl_ref), (kc_ref, vc_ref) = kv_refs
        span = tq + 2 * WINDOW
        q0 = pl.program_id(1) * tq
        start = pl.multiple_of(jnp.clip(q0 - WINDOW, 0, kl_ref.shape[0] - span), BLOCK)
        kv = [(kl_ref[pl.ds(start, span), :], vl_ref[pl.ds(start, span), :]), (kc_ref[...], vc_ref[...])]
        row = lax.broadcasted_iota(jnp.int32, (2 * tq, span), 0)
        col = lax.broadcasted_iota(jnp.int32, (2 * tq, span), 1)
        qpos = q0 + jnp.where(row >= tq, row - tq, row)
        dist = (start + col) - qpos
        valid = (dist >= -WINDOW) & (dist <= WINDOW)
    else:
        kv = _kv_cols(kv_refs, 0, 0)
        valid = None
    rows = lax.broadcasted_iota(jnp.int32, (2 * tq, 1), 0)
    for j in range(4):
        qq = _split_heads(q_ref[:, j * LANES:(j + 1) * LANES])
        sink = jnp.where(rows < tq, sink_ref[2 * j], sink_ref[2 * j + 1]) * LOG2E
        o = _softmax_pv(shift, qq, kv, mask=valid, extra_logit=sink)
        o_ref[:, j * LANES:(j + 1) * LANES] = _merge_halves(o, tq).astype(BF16)


def _mix_c(lam_init, shift, lq1_ref, lk1_ref, lq2_ref, lk2_ref, subln_ref, q_ref, kv_refs, o_ref):
    tq = q_ref.shape[0]
    lam = (jnp.exp(jnp.sum(lq1_ref[...] * lk1_ref[...], axis=-1, keepdims=True))
           - jnp.exp(jnp.sum(lq2_ref[...] * lk2_ref[...], axis=-1, keepdims=True)) + lam_init)
    for c in range(C_HEADS):
        qq = _split_heads(q_ref[:, c * LANES:(c + 1) * LANES])
        o = _softmax_pv(shift, qq, _one_group(_kv_cols(kv_refs, c, c)))
        oc = o[:tq] - lam * o[tq:]
        oc = oc * lax.rsqrt(jnp.mean(oc * oc, axis=-1, keepdims=True) + EPS) * subln_ref[...]
        o_ref[:, c * LANES:(c + 1) * LANES] = (oc * (1.0 - lam_init)).astype(BF16)


def _mix_d(shift, q_ref, kv_refs, o_ref):
    tq = q_ref.shape[0]
    for hp in range(D_HEADS // 2):
        halves = [_softmax_pv(shift, q_ref[:, h * LANES:(h + 1) * LANES], _kv_cols(kv_refs, h, hp))
                  for h in (2 * hp, 2 * hp + 1)]
        o = jnp.concatenate(halves, axis=0)
        o_ref[:, hp * LANES:(hp + 1) * LANES] = _merge_halves(o, tq).astype(BF16)


def _mix_ab(shift, sink_ref, qa_ref, kva_refs, oa_ref, qb_ref, kvb_refs, ob_ref):
    _mix_a(shift, sink_ref, qa_ref, kva_refs, oa_ref)
    _mix_b(shift, qb_ref, kvb_refs, ob_ref)


ATTN_TQ = 256


def _attention(body, safe_idx, safe, extra, extra_specs, units, *, seq_q, name):
    n = units[0][0].shape[0]
    nb = n // seq_q
    tq = ATTN_TQ
    nq = seq_q // tq
    tok = lambda b, i: (b * nq + i, 0)
    per_batch = lambda a: pl.BlockSpec((a.shape[0] // nb, a.shape[1]), lambda b, i: (b, 0))
    args, in_specs, counts = [], [], []
    for q, kv in units:
        args.append(q)
        in_specs.append(pl.BlockSpec((tq, q.shape[1]), tok))
        for pair in kv:
            args.extend(pair)
            in_specs.extend(per_batch(a) for a in pair)
        counts.append(len(kv))
    n_extra = len(extra)

    def kernel(shift, *refs):
        ins, outs = refs[n_extra:len(refs) - len(units)], refs[len(refs) - len(units):]
        packed, pos = [], 0
        for u, nkv in enumerate(counts):
            kv_refs = [(ins[pos + 1 + 2 * g], ins[pos + 2 + 2 * g]) for g in range(nkv)]
            packed.extend([ins[pos], kv_refs, outs[u]])
            pos += 1 + 2 * nkv
        body(shift, *refs[:n_extra], *packed)

    out = pl.pallas_call(
        functools.partial(_guarded, kernel, safe_idx),
        out_shape=[jax.ShapeDtypeStruct((n, BRANCH_W), BF16)] * len(units),
        grid=(nb, nq),
        in_specs=[pl.BlockSpec(memory_space=pltpu.SMEM)] + list(extra_specs) + in_specs,
        out_specs=[pl.BlockSpec((tq, BRANCH_W), tok)] * len(units),
        compiler_params=_cparams(("arbitrary", "arbitrary")),
        name=name,
    )(safe, *extra, *args)
    return out


def _mixers(qs, kv_groups, lp, *, seq_q, tag):
    qa, qb, qc, qd = qs
    pairs = lambda j: [(g[2 * j], g[2 * j + 1]) for g in kv_groups]
    safe = lp["safe"]
    row = lambda w: pl.BlockSpec((1, w), lambda b, i: (0, 0))
    oa, ob = _attention(_mix_ab, 0, safe, [lp["sink"]], [pl.BlockSpec(memory_space=pltpu.SMEM)],
                        [(qa, pairs(0)), (qb, pairs(1))], seq_q=seq_q, name="attn_ab_" + tag)
    oc, = _attention(functools.partial(_mix_c, lp["lam_init"]), 1, safe,
                     [lp["lq1"], lp["lk1"], lp["lq2"], lp["lk2"], lp["subln"]],
                     [row(HD)] * 4 + [row(2 * HD)], [(qc, pairs(2))], seq_q=seq_q, name="attn_c_" + tag)
    od, = _attention(_mix_d, 2, safe, [], [], [(qd, pairs(3))], seq_q=seq_q, name="attn_d_" + tag)
    return oa, ob, oc, od


def _merge_kernel(moe, oa_ref, ob_ref, oc_ref, od_ref, g_ref, x_ref, mod_ref, norm_ref,
                  wbr_ref, wout_ref, *refs):
    if moe:
        wr_ref, xo_ref, h_ref, lg_ref = refs
    else:
        xo_ref, h_ref = refs
    m = mod_ref[0]
    y = None
    for n, o_ref in enumerate((oa_ref, ob_ref, oc_ref, od_ref)):
        yn = g_ref[:, n * D_MODEL:(n + 1) * D_MODEL].astype(F32) * _dot(o_ref[...], wbr_ref[n])
        y = yn if y is None else y + yn
    z = _dot(y.astype(BF16), wout_ref[...])
    xn = x_ref[...] + m[2:3] * z
    xo_ref[...] = xn
    h = _adaln(xn, norm_ref[...], m[3:4], m[4:5])
    if moe:
        h_ref[...] = h
        h_hi = h.astype(BF16)
        h_lo = (h - h_hi.astype(F32)).astype(BF16)
        lg_ref[...] = (_dot_nt(wr_ref[0], h_hi) + _dot_nt(wr_ref[1], h_hi)) + _dot_nt(wr_ref[0], h_lo)
    else:
        h_ref[...] = h.astype(BF16)


def _merge(branches, g, x, mod, norm, wbr, wout, router_t, *, seq):
    n = x.shape[0]
    tm = min(MERGE_TM, seq)
    tps = seq // tm
    moe = router_t is not None
    tok = lambda i: (i, 0)
    in_specs = [pl.BlockSpec((tm, BRANCH_W), tok)] * 4 + [
        pl.BlockSpec((tm, GATE_COLS), tok),
        pl.BlockSpec((tm, D_MODEL), tok),
        pl.BlockSpec((1, 6, D_MODEL), lambda i: (i // tps, 0, 0)),
        _resident(norm.shape), _resident(wbr.shape), _resident(wout.shape)]
    args = list(branches) + [g, x, mod, norm, wbr, wout]
    out_shape = [jax.ShapeDtypeStruct((n, D_MODEL), F32),
                 jax.ShapeDtypeStruct((n, D_MODEL), F32 if moe else BF16)]
    out_specs = [pl.BlockSpec((tm, D_MODEL), tok), pl.BlockSpec((tm, D_MODEL), tok)]
    if moe:
        in_specs.append(_resident(router_t.shape))
        args.append(router_t)
        out_shape.append(jax.ShapeDtypeStruct((N_EXPERTS, n), F32))
        out_specs.append(pl.BlockSpec((N_EXPERTS, tm), lambda i: (0, i)))
    return pl.pallas_call(
        functools.partial(_merge_kernel, moe),
        out_shape=out_shape, grid=(n // tm,), in_specs=in_specs, out_specs=out_specs,
        compiler_params=_cparams(("arbitrary",)),
        name="merge_moe" if moe else "merge_dense",
    )(*args)


def _ffn_kernel(h_ref, x_ref, mod_ref, wg_ref, wu_ref, wd_ref, o_ref, hid_ref):
    h = h_ref[...]
    for c in range(0, D_FF, FFN_TF):
        a = _dot(h, wg_ref[:, c:c + FFN_TF])
        u = _dot(h, wu_ref[:, c:c + FFN_TF])
        hid_ref[:, c:c + FFN_TF] = (a * _sigmoid(a) * u).astype(BF16)
    m = mod_ref[0]
    o_ref[...] = x_ref[...] + m[5:6] * _dot(hid_ref[...], wd_ref[...])


def _dense_ffn(h, x, mod, wg, wu, wd, *, seq):
    n = x.shape[0]
    tm = min(FFN_TM, seq)
    tps = seq // tm
    tok = lambda i: (i, 0)
    return pl.pallas_call(
        _ffn_kernel,
        out_shape=jax.ShapeDtypeStruct((n, D_MODEL), F32),
        grid=(n // tm,),
        in_specs=[pl.BlockSpec((tm, D_MODEL), tok), pl.BlockSpec((tm, D_MODEL), tok),
                  pl.BlockSpec((1, 6, D_MODEL), lambda i: (i // tps, 0, 0)),
                  _resident(wg.shape), _resident(wu.shape), _resident(wd.shape)],
        out_specs=pl.BlockSpec((tm, D_MODEL), tok),
        scratch_shapes=[pltpu.VMEM((tm, D_FF), BF16)],
        compiler_params=_cparams(("arbitrary",)),
        name="dense_ffn",
    )(h, x, mod, wg, wu, wd)


ROUTE_CHUNK = 512
DMA_UNROLL = 8


def _route_kernel(n, lg_ref, tri_ref, dest_ref, wts_ref, meta_ref, mem_ref, pos_ref):
    lg = lg_ref[...]
    eidx = lax.broadcasted_iota(jnp.int32, lg.shape, 0).astype(F32)
    none = float(N_EXPERTS)
    m1 = jnp.max(lg, axis=0, keepdims=True)
    i1 = jnp.min(jnp.where(lg == m1, eidx, none), axis=0, keepdims=True)
    lg2 = jnp.where(eidx == i1, -jnp.inf, lg)
    m2 = jnp.max(lg2, axis=0, keepdims=True)
    i2 = jnp.min(jnp.where(lg2 == m2, eidx, none), axis=0, keepdims=True)
    e = jnp.exp(m2 - m1)
    w1 = 1.0 / (1.0 + e)
    wts_ref[0:1, :] = w1
    wts_ref[1:2, :] = e * w1
    mem_ref[...] = jnp.where(eidx == i1, 1.0, 0.0) + jnp.where(eidx == i2, 1.0, 0.0)

    carry = jnp.zeros((N_EXPERTS, 1), F32)
    tri = tri_ref[...]
    for c in range(0, n, ROUTE_CHUNK):
        mc = mem_ref[:, c:c + ROUTE_CHUNK]
        pos_ref[:, c:c + ROUTE_CHUNK] = carry + _dot(mc.astype(BF16), tri)
        carry = carry + jnp.sum(mc, axis=1, keepdims=True)
    padded = jnp.floor((carry + (MOE_TM - 1.0)) * (1.0 / MOE_TM)) * MOE_TM
    esub = lax.broadcasted_iota(jnp.int32, (N_EXPERTS, 1), 0)
    pstart = jnp.zeros((N_EXPERTS, 1), F32)
    for k in range(N_EXPERTS - 1):
        pstart = pstart + jnp.where(esub > k, padded[k:k + 1, :], 0.0)
    pend = pstart + padded
    slot = pstart + pos_ref[...]
    dest_ref[0:1, :] = jnp.sum(jnp.where(eidx == i1, slot, 0.0), axis=0, keepdims=True).astype(jnp.int32)
    dest_ref[1:2, :] = jnp.sum(jnp.where(eidx == i2, slot, 0.0), axis=0, keepdims=True).astype(jnp.int32)

    nblk = pend[N_EXPERTS - 1:N_EXPERTS, :] * (1.0 / MOE_TM)
    brow = jnp.minimum(lax.broadcasted_iota(jnp.int32, (1, LANES), 1).astype(F32), nblk - 1.0) * MOE_TM
    be = jnp.zeros((1, LANES), F32)
    for k in range(N_EXPERTS):
        be = be + jnp.where(pend[k:k + 1, :] <= brow, 1.0, 0.0)
    meta_ref[...] = jnp.zeros(meta_ref.shape, jnp.int32)
    meta_ref[0:1, :] = jnp.minimum(be, N_EXPERTS - 1.0).astype(jnp.int32)
    meta_ref[1:2, :] = jnp.broadcast_to(nblk, (1, LANES)).astype(jnp.int32)


def _route(logits_t):
    n = logits_t.shape[1]
    nblk_max = n * TOP_K // MOE_TM + N_EXPERTS
    assert nblk_max <= LANES and n % ROUTE_CHUNK == 0
    tri = jnp.asarray(np.triu(np.ones((ROUTE_CHUNK, ROUTE_CHUNK)), 1), BF16)
    whole = lambda shape: pl.BlockSpec(shape, lambda: (0,) * len(shape))
    dest, wts, meta = pl.pallas_call(
        functools.partial(_route_kernel, n),
        out_shape=[jax.ShapeDtypeStruct((TOP_K, n), jnp.int32),
                   jax.ShapeDtypeStruct((TOP_K, n), F32),
                   jax.ShapeDtypeStruct((8, LANES), jnp.int32)],
        in_specs=[whole(logits_t.shape), whole(tri.shape)],
        out_specs=[whole((TOP_K, n)), whole((TOP_K, n)), whole((8, LANES))],
        scratch_shapes=[pltpu.VMEM((N_EXPERTS, n), F32), pltpu.VMEM((N_EXPERTS, n), F32)],
        compiler_params=pltpu.CompilerParams(vmem_limit_bytes=VMEM_LIMIT),
        name="moe_route",
    )(logits_t, tri)
    return dest, wts, meta[0, :nblk_max], meta[1, :1], nblk_max


def _dispatch_kernel(tg, dest_ref, h_ref, xin_hbm, xbuf_hbm, sem):
    del xin_hbm

    def copies(r):
        src = h_ref.at[pl.ds(r, 1)]
        return (pltpu.make_async_copy(src, xbuf_hbm.at[pl.ds(dest_ref[0, 0, r], 1)], sem),
                pltpu.make_async_copy(src, xbuf_hbm.at[pl.ds(dest_ref[0, 0, tg + r], 1)], sem))

    def start(r, carry):
        for k, cp in enumerate(copies(r)):
            cp.start(priority=k)
        return carry

    def wait(r, carry):
        for cp in copies(r):
            cp.wait()
        return carry

    lax.fori_loop(0, tg, start, 0, unroll=DMA_UNROLL)
    lax.fori_loop(0, tg, wait, 0, unroll=DMA_UNROLL)


def _tile_dest(dest, t):
    n = dest.shape[1]
    return jnp.transpose(dest.reshape(TOP_K, n // t, t), (1, 0, 2)).reshape(n // t, 1, TOP_K * t)


def _dispatch(h, dest, nrows):
    n = h.shape[0]
    tg = 512
    nt = n // tg
    dest_t = _tile_dest(dest, tg)
    xbuf0 = jnp.zeros((nrows, D_MODEL), F32)
    return pl.pallas_call(
        functools.partial(_dispatch_kernel, tg),
        out_shape=jax.ShapeDtypeStruct((nrows, D_MODEL), F32),
        grid=(nt,),
        in_specs=[pl.BlockSpec((1, 1, TOP_K * tg), lambda i: (i, 0, 0), memory_space=pltpu.SMEM),
                  pl.BlockSpec((tg, D_MODEL), lambda i: (i, 0)), pl.BlockSpec(memory_space=pl.ANY)],
        out_specs=pl.BlockSpec(memory_space=pl.ANY),
        scratch_shapes=[pltpu.SemaphoreType.DMA(())],
        input_output_aliases={2: 0},
        compiler_params=_cparams(("arbitrary",)),
        name="moe_dispatch",
    )(dest_t, h, xbuf0)


def _moe_kernel(nf, blk_e_ref, nblk_ref, x_ref, w1_ref, w3_ref, w2_ref, o_ref, xb_ref, acc_ref):
    del blk_e_ref
    i = pl.program_id(0)
    j = pl.program_id(1)

    @pl.when(i < nblk_ref[0])
    def _():
        @pl.when(j == 0)
        def _():
            xb_ref[...] = x_ref[...].astype(BF16)
            acc_ref[...] = jnp.zeros_like(acc_ref)

        xb = xb_ref[...]
        a = _dot(xb, w1_ref[0])
        u = _dot(xb, w3_ref[0])
        acc_ref[...] += _dot((a * _sigmoid(a) * u).astype(BF16), w2_ref[0])

        @pl.when(j == nf - 1)
        def _():
            o_ref[...] = acc_ref[...]

    @pl.when((i >= nblk_ref[0]) & (j == nf - 1))
    def _():
        o_ref[...] = jnp.zeros_like(o_ref)


def _moe_ffn(xbuf, blk_e, nblk, nblk_max, w1, w3, w2):
    nf = EXPERT_FF // MOE_TF

    def rows(i, j, be, nb):
        return (i, 0)

    def jj(i, j, nb):
        return jnp.where(i < nb[0], j, nf - 1)

    grid_spec = pltpu.PrefetchScalarGridSpec(
        num_scalar_prefetch=2, grid=(nblk_max, nf),
        in_specs=[pl.BlockSpec((MOE_TM, D_MODEL), rows),
                  pl.BlockSpec((1, D_MODEL, MOE_TF), lambda i, j, be, nb: (be[i], 0, jj(i, j, nb))),
                  pl.BlockSpec((1, D_MODEL, MOE_TF), lambda i, j, be, nb: (be[i], 0, jj(i, j, nb))),
                  pl.BlockSpec((1, MOE_TF, D_MODEL), lambda i, j, be, nb: (be[i], jj(i, j, nb), 0))],
        out_specs=pl.BlockSpec((MOE_TM, D_MODEL), rows),
        scratch_shapes=[pltpu.VMEM((MOE_TM, D_MODEL), BF16), pltpu.VMEM((MOE_TM, D_MODEL), F32)])
    return pl.pallas_call(
        functools.partial(_moe_kernel, nf),
        out_shape=jax.ShapeDtypeStruct(xbuf.shape, F32),
        grid_spec=grid_spec,
        compiler_params=_cparams(("arbitrary", "arbitrary")),
        name="moe_experts",
    )(blk_e, nblk, xbuf, w1, w3, w2)


def _combine_kernel(tc, nt, dest_ref, dest_next_ref, wts_ref, x_ref, mod_ref, y_hbm, o_ref, yg_ref, sems):
    i = pl.program_id(0)
    slot = lax.rem(i, 2)

    def copies(d_ref, s, r):
        return (pltpu.make_async_copy(y_hbm.at[pl.ds(d_ref[0, 0, r], 1)],
                                      yg_ref.at[s, 0, pl.ds(r, 1)], sems.at[s]),
                pltpu.make_async_copy(y_hbm.at[pl.ds(d_ref[0, 0, tc + r], 1)],
                                      yg_ref.at[s, 1, pl.ds(r, 1)], sems.at[s]))

    def request(d_ref, s):
        def start(r, carry):
            for k, cp in enumerate(copies(d_ref, s, r)):
                cp.start(priority=k)
            return carry
        lax.fori_loop(0, tc, start, 0, unroll=DMA_UNROLL)

    @pl.when(i == 0)
    def _():
        request(dest_ref, slot)

    @pl.when(i + 1 < nt)
    def _():
        request(dest_next_ref, 1 - slot)

    def wait(r, carry):
        for cp in copies(dest_ref, slot, r):
            cp.wait()
        return carry

    lax.fori_loop(0, tc, wait, 0, unroll=DMA_UNROLL)
    m = mod_ref[0]
    w = wts_ref[...]
    y = w[:, 0:1] * yg_ref[slot, 0] + w[:, 1:2] * yg_ref[slot, 1]
    o_ref[...] = x_ref[...] + m[5:6] * y


def _combine(ybuf, dest, wts, x, mod, *, seq):
    n = x.shape[0]
    tc = 256
    nt = n // tc
    tps = seq // tc
    dest_t = _tile_dest(dest, tc)
    wts = wts.T
    tok = lambda i: (i, 0)
    slots = lambda step: pl.BlockSpec((1, 1, TOP_K * tc), lambda i: (jnp.minimum(i + step, nt - 1), 0, 0),
                                      memory_space=pltpu.SMEM)
    return pl.pallas_call(
        functools.partial(_combine_kernel, tc, nt),
        out_shape=jax.ShapeDtypeStruct((n, D_MODEL), F32),
        grid=(nt,),
        in_specs=[slots(0), slots(1),
                  pl.BlockSpec((tc, TOP_K), tok),
                  pl.BlockSpec((tc, D_MODEL), tok),
                  pl.BlockSpec((1, 6, D_MODEL), lambda i: (i // tps, 0, 0)),
                  pl.BlockSpec(memory_space=pl.ANY)],
        out_specs=pl.BlockSpec((tc, D_MODEL), tok),
        scratch_shapes=[pltpu.VMEM((2, TOP_K, tc, D_MODEL), F32), pltpu.SemaphoreType.DMA((2,))],
        compiler_params=_cparams(("arbitrary",)),
        name="moe_combine",
    )(dest_t, dest_t, wts, x, mod, ybuf)


def _head_cols(perm):
    return np.concatenate([np.arange(p * HD, (p + 1) * HD) for p in perm])


def _prep_w_in(w):
    hp = _head_cols(HEAD_PERM)
    kv_lo = Q_COLS
    kr_lo = kv_lo + KV_PAD_COLS - LANES
    parts = [w[:, 0:512][:, hp], w[:, 512:1024][:, hp], w[:, 1024:Q_COLS],
             w[:, kv_lo:kr_lo],
             jnp.pad(w[:, kr_lo:kr_lo + QK_ROPE], ((0, 0), (QK_NOPE, LANES - QK_NOPE - QK_ROPE))),
             w[:, kr_lo + QK_ROPE:]]
    return jnp.concatenate(parts, axis=1).astype(BF16)


def _prep_layer(l, p):
    hp = _head_cols(HEAD_PERM)
    zeros = lambda k: jnp.zeros((k,), F32)
    sc = LOG2E / math.sqrt(HD)
    sc_d = LOG2E / math.sqrt(QK_NOPE + QK_ROPE)
    wuq = p["d_w_uq"][l].reshape(Q_LORA, D_HEADS, QK_NOPE + QK_ROPE)
    wuq = jnp.pad(wuq, ((0, 0), (0, 0), (0, LANES - QK_NOPE - QK_ROPE))).reshape(Q_LORA, D_HEADS * LANES)
    wukv = p["d_w_ukv"][l].reshape(KV_LORA, D_HEADS, QK_NOPE + V_HEAD)
    wuk = jnp.pad(wukv[:, :, :QK_NOPE], ((0, 0), (0, 0), (0, LANES - QK_NOPE))).reshape(KV_LORA, D_HEADS * LANES)
    wuv = wukv[:, :, QK_NOPE:].reshape(KV_LORA, D_HEADS * V_HEAD)
    gains = jnp.stack([
        jnp.tile(p["a_qn"][l], 2) * sc, jnp.tile(p["b_qn"][l], 2) * sc, jnp.tile(p["c_qn"][l], 2) * sc,
        jnp.tile(p["a_kn"][l], 2), jnp.tile(p["b_kn"][l], 2), jnp.tile(p["c_kn"][l], 2),
        jnp.concatenate([p["d_qn_nope"][l], p["d_qn_rope"][l], zeros(32)]) * sc_d,
        jnp.concatenate([p["d_kn_nope"][l], zeros(64)]),
        jnp.concatenate([zeros(64), p["d_kn_rope"][l], zeros(32)]),
    ] + [zeros(LANES)] * 7).astype(F32)
    vec = jnp.stack([p["mix_norm"][l],
                     jnp.concatenate([p["d_q_norm"][l], p["d_kv_norm"][l], zeros(D_MODEL - Q_LORA - KV_LORA)])]
                    + [zeros(D_MODEL)] * 6).astype(F32)
    wbr = p["w_br"][l]
    wbr = jnp.stack([wbr[0][hp], wbr[1][hp], wbr[2], wbr[3]]).astype(BF16)

    amax = lambda v: jnp.max(jnp.abs(v))
    bound64 = lambda gq, gk: 1.02 * HD * amax(gq) * amax(gk)
    nq_d = jnp.sqrt(QK_NOPE * amax(gains[6, :QK_NOPE]) ** 2 + QK_ROPE * amax(gains[6, QK_NOPE:]) ** 2)
    nk_d = jnp.sqrt(QK_NOPE * amax(gains[7]) ** 2 + QK_ROPE * amax(gains[8]) ** 2)
    bounds = jnp.stack([
        jnp.maximum(jnp.maximum(bound64(gains[0], gains[3]), bound64(gains[1], gains[4])),
                    LOG2E * amax(p["a_sink"][l])),
        bound64(gains[2], gains[5]),
        1.02 * nq_d * nk_d])
    safe = (bounds <= LOGIT_SAFE).astype(jnp.int32)
    return {
        "safe": safe,
        "w_in": _prep_w_in(p["w_in"][l]),
        "wuq": wuq.astype(BF16),
        "wukv": jnp.concatenate([wuk, wuv], axis=1).astype(BF16),
        "gains": gains, "vec": vec, "wbr": wbr, "wout": p["w_out"][l].astype(BF16),
        "ffn_norm": p["ffn_norm"][l].reshape(1, D_MODEL),
        "sink": p["a_sink"][l][np.asarray(HEAD_PERM)].astype(F32),
        "lq1": p["c_lq1"][l].reshape(1, HD), "lk1": p["c_lk1"][l].reshape(1, HD),
        "lq2": p["c_lq2"][l].reshape(1, HD), "lk2": p["c_lk2"][l].reshape(1, HD),
        "subln": p["c_subln"][l].reshape(1, 2 * HD),
        "lam_init": 0.8 - 0.6 * math.exp(-0.3 * l),
    }


def _group_avg_mats():
    m64 = np.kron(np.eye(4), np.full((HD, HD), 1.0 / HD))
    one = np.zeros((LANES, LANES))
    one[:QK_NOPE, :QK_NOPE] = 1.0 / QK_NOPE
    one[QK_NOPE:QK_NOPE + QK_ROPE, QK_NOPE:QK_NOPE + QK_ROPE] = 1.0 / QK_ROPE
    md = np.kron(np.eye(2), one)
    return jnp.asarray(np.stack([m64, md]), BF16)


def _rope_tables(seq, rotary):
    if not rotary:
        one = jnp.ones((seq, LANES), F32)
        zero = jnp.zeros((seq, LANES), F32)
        return jnp.stack([one, zero, one, zero])
    t = jnp.arange(seq)

    def angles(rot_dim):
        nfreq = rot_dim // 4
        inv = jnp.power(ROPE_THETA, -jnp.arange(nfreq, dtype=F32) / nfreq)
        return jnp.concatenate([(t // GRID_W).astype(F32)[:, None] * inv,
                                (t % GRID_W).astype(F32)[:, None] * inv], axis=-1)

    ah = angles(HD)
    c64 = jnp.tile(jnp.cos(ah), (1, 4))
    s64 = jnp.tile(jnp.concatenate([-jnp.sin(ah), jnp.sin(ah)], axis=1), (1, 2))
    ar = angles(QK_ROPE)
    one = jnp.ones((seq, 1), F32)
    cd = jnp.concatenate([one * jnp.ones((1, QK_NOPE), F32), jnp.cos(ar), jnp.cos(ar),
                          one * jnp.ones((1, LANES - QK_NOPE - QK_ROPE), F32)], axis=1)
    sd = jnp.concatenate([jnp.zeros((seq, QK_NOPE), F32), -jnp.sin(ar), jnp.sin(ar),
                          jnp.zeros((seq, LANES - QK_NOPE - QK_ROPE), F32)], axis=1)
    return jnp.stack([c64, s64, cd, sd])


def kernel(x, c, ctx, c_ctx, w_mod, b_mod, mix_norm, ffn_norm, w_in, a_qn, a_kn, a_sink, b_qn, b_kn, c_qn, c_kn, c_lq1, c_lk1, c_lq2, c_lk2, c_subln, d_q_norm, d_kv_norm, d_w_uq, d_w_ukv, d_qn_nope, d_kn_nope, d_qn_rope, d_kn_rope, w_br, w_out, ff_w_gate, ff_w_up, ff_w_down, moe_router, moe_w1, moe_w3, moe_w2):
    p = dict(mix_norm=mix_norm, ffn_norm=ffn_norm, w_in=w_in, a_qn=a_qn, a_kn=a_kn, a_sink=a_sink,
             b_qn=b_qn, b_kn=b_kn, c_qn=c_qn, c_kn=c_kn, c_lq1=c_lq1, c_lk1=c_lk1, c_lq2=c_lq2,
             c_lk2=c_lk2, c_subln=c_subln, d_q_norm=d_q_norm, d_kv_norm=d_kv_norm, d_w_uq=d_w_uq,
             d_w_ukv=d_w_ukv, d_qn_nope=d_qn_nope, d_kn_nope=d_kn_nope, d_qn_rope=d_qn_rope,
             d_kn_rope=d_kn_rope, w_br=w_br, w_out=w_out)
    nb, seq, _ = x.shape
    assert seq % TM == 0 and ctx.shape[1] == CTX_LEN and nb <= 15
    depth = w_mod.shape[0]
    lat = x.reshape(nb * seq, D_MODEL)
    cx = ctx.reshape(nb * CTX_LEN, D_MODEL)

    cond = jnp.zeros((16, D_MODEL), F32).at[:nb].set(c).at[nb].set(c_ctx)
    mod_all = _modulation(cond, w_mod, b_mod).reshape(depth, 16, 6, D_MODEL)
    rope_lat = _rope_tables(seq, True)
    rope_ctx = _rope_tables(CTX_LEN, False)
    mavg = _group_avg_mats()

    for l in range(depth):
        last = l == depth - 1
        lp = _prep_layer(l, p)
        mod_lat = mod_all[l, :nb]
        mod_ctx = jnp.broadcast_to(mod_all[l, nb:nb + 1], (nb, 6, D_MODEL))
        common = (lp["vec"], lp["gains"])
        wts = (lp["wuq"], lp["wukv"])
        if last:
            kv_ctx = _project(cx, mod_ctx, *common, rope_ctx, mavg,
                              lp["w_in"][:, Q_COLS:Q_COLS + KV_PAD_COLS], *wts, seq=CTX_LEN,
                              tm=CTX_LEN, with_q=False)
        else:
            outs = _project(cx, mod_ctx, *common, rope_ctx, mavg, lp["w_in"], *wts, seq=CTX_LEN,
                            tm=CTX_LEN, with_q=True)
            kv_ctx, q_ctx, g_ctx = outs[:8], outs[8:12], outs[12]
        outs = _project(lat, mod_lat, *common, rope_lat, mavg, lp["w_in"], *wts, seq=seq,
                        tm=TM, with_q=True)
        kv_lat, q_lat, g_lat = outs[:8], outs[8:12], outs[12]
        br_lat = _mixers(q_lat, [kv_lat, kv_ctx], lp, seq_q=seq, tag="lat")
        moe = l % 2 == 1
        i = l // 2
        router_t = None
        if moe:
            wr = moe_router[i].T
            wr_hi = wr.astype(BF16)
            router_t = jnp.stack([wr_hi, (wr - wr_hi.astype(F32)).astype(BF16)])
        res = _merge(br_lat, g_lat, lat, mod_lat, lp["ffn_norm"], lp["wbr"], lp["wout"], router_t, seq=seq)
        if not last:
            br_ctx = _mixers(q_ctx, [kv_ctx], lp, seq_q=CTX_LEN, tag="ctx")
            res_ctx = _merge(br_ctx, g_ctx, cx, mod_ctx, lp["ffn_norm"], lp["wbr"], lp["wout"],
                             router_t, seq=CTX_LEN)
        if not moe:
            wg, wu, wd = (ff_w_gate[i].astype(BF16), ff_w_up[i].astype(BF16), ff_w_down[i].astype(BF16))
            lat = _dense_ffn(res[1], res[0], mod_lat, wg, wu, wd, seq=seq)
            if not last:
                cx = _dense_ffn(res_ctx[1], res_ctx[0], mod_ctx, wg, wu, wd, seq=CTX_LEN)
        else:
            w1, w3, w2 = moe_w1[i].astype(BF16), moe_w3[i].astype(BF16), moe_w2[i].astype(BF16)

            def routed(res_t, mod_t, seq_t):
                xr, hr, lg = res_t
                dest, wts_r, blk_e, nblk, nblk_max = _route(lg)
                xbuf = _dispatch(hr, dest, nblk_max * MOE_TM)
                ybuf = _moe_ffn(xbuf, blk_e, nblk, nblk_max, w1, w3, w2)
                return _combine(ybuf, dest, wts_r, xr, mod_t, seq=seq_t)

            lat = routed(res, mod_lat, seq)
            if not last:
                cx = routed(res_ctx, mod_ctx, CTX_LEN)
    return lat.reshape(nb, seq, D_MODEL)
```

```python
import functools
import math

import numpy as np
import jax
import jax.numpy as jnp
from jax import lax
from jax.experimental import pallas as pl
from jax.experimental.pallas import tpu as pltpu

F32 = jnp.float32
BF16 = jnp.bfloat16

D_MODEL = 1024
DEPTH = 2
CTX_LEN = 256
GRID_W = 64
HD = 64
BLOCK = 128
WINDOW = 128
ROPE_THETA = 10000.0
EPS = 1e-6
NEG_INF = -1e30
A_HEADS = 8
C_HEADS = 4
D_HEADS = 8
Q_LORA = 512
KV_LORA = 256
QK_NOPE = 64
QK_ROPE = 32
V_HEAD = 64
BRANCH_W = 512
D_FF = 2816
N_EXPERTS = 8
TOP_K = 2
EXPERT_FF = 3584

LANES = 128
VMEM_LIMIT = 56 * 1024 * 1024
LOG2E = math.log2(math.e)

Q_COLS = 2048
KV_PAD_COLS = 1920
GATE_COLS = 4 * D_MODEL
W_COLS = Q_COLS + KV_PAD_COLS + GATE_COLS
HEAD_PERM = (0, 4, 1, 5, 2, 6, 3, 7)

TM = 512
MERGE_TM = 512
MOE_TM = 512
MOE_TF = 1792
FFN_TM = 512
FFN_TF = 256


def _cparams(sem):
    return pltpu.CompilerParams(dimension_semantics=sem, vmem_limit_bytes=VMEM_LIMIT)


def _resident(shape):
    nd = len(shape)
    return pl.BlockSpec(shape, lambda *_: (0,) * nd, pipeline_mode=pl.Buffered(1))


def _sigmoid(x):
    return 1.0 / (1.0 + jnp.exp(-x))


def _dot(a, b):
    return jnp.dot(a, b, preferred_element_type=F32)


def _dot_nt(a, b):
    return lax.dot_general(a, b, (((1,), (1,)), ((), ())), preferred_element_type=F32)


def _mod_kernel(c_ref, w_ref, b_ref, o_ref):
    cond = c_ref[...]
    s = cond * _sigmoid(cond)
    o_ref[0] = jnp.dot(s, w_ref[0], precision=lax.Precision.HIGHEST,
                       preferred_element_type=F32) + b_ref[0]


def _modulation(cond, w_mod, b_mod):
    depth = w_mod.shape[0]
    nct = 6 * D_MODEL // 1024
    return pl.pallas_call(
        _mod_kernel,
        out_shape=jax.ShapeDtypeStruct((depth, 16, 6 * D_MODEL), F32),
        grid=(depth, nct),
        in_specs=[pl.BlockSpec((16, D_MODEL), lambda l, j: (0, 0)),
                  pl.BlockSpec((1, D_MODEL, 1024), lambda l, j: (l, 0, j)),
                  pl.BlockSpec((1, 1, 1024), lambda l, j: (l, 0, j))],
        out_specs=pl.BlockSpec((1, 16, 1024), lambda l, j: (l, 0, j)),
        compiler_params=_cparams(("arbitrary", "arbitrary")),
        name="modulation",
    )(cond, w_mod, b_mod.reshape(depth, 1, 6 * D_MODEL))


def _adaln(x, g, shift, scale):
    ms = jnp.mean(x * x, axis=-1, keepdims=True)
    y = x * lax.rsqrt(ms + EPS) * g
    return y * (1.0 + scale) + shift


def _group_mean_sq(r, mavg):
    sq = (r * r).astype(BF16)
    w = r.shape[1]
    parts = []
    for c in range(0, w, 2 * LANES):
        cw = min(2 * LANES, w - c)
        parts.append(_dot(sq[:, c:c + cw], mavg[:cw, :cw]))
    return parts[0] if len(parts) == 1 else jnp.concatenate(parts, axis=1)


def _rope_slab(y, cos, sin, half):
    lane = lax.broadcasted_iota(jnp.int32, y.shape, 1)
    first = (lane % (2 * half)) < half
    sw = jnp.where(first, pltpu.roll(y, LANES - half, 1), pltpu.roll(y, half, 1))
    return y * cos + sw * sin


def _norm_rope(r, ms, gain, cos, sin, half):
    outs = []
    for c in range(0, r.shape[1], LANES):
        y = r[:, c:c + LANES] * lax.rsqrt(ms[:, c:c + LANES] + EPS) * gain
        if cos is not None:
            y = _rope_slab(y, cos, sin, half)
        outs.append(y)
    return outs[0] if len(outs) == 1 else jnp.concatenate(outs, axis=1)


def _proj_kernel(with_q, x_ref, mod_ref, vec_ref, gains_ref, rope_ref, mavg_ref,
                 w_ref, wuq_ref, wukv_ref, *refs):
    if with_q:
        (qa_ref, qb_ref, qc_ref, qd_ref, g_ref) = refs[-5:]
        kv_refs = refs[-13:-5]
        kvo, go = Q_COLS, Q_COLS + KV_PAD_COLS
    else:
        kv_refs = refs[-8:]
        kvo = 0
    ka_ref, va_ref, kb_ref, vb_ref, kc_ref, vc_ref, kd_ref, vd_ref = kv_refs

    m = mod_ref[0]
    hb = _adaln(x_ref[...], vec_ref[0:1, :], m[0:1], m[1:2]).astype(BF16)

    def mm(lo, hi):
        return _dot(hb, w_ref[:, lo:hi])

    m64 = mavg_ref[0]
    md = mavg_ref[1]
    c64, s64, cd, sd = rope_ref[0], rope_ref[1], rope_ref[2], rope_ref[3]
    gains = gains_ref[...]

    def gain(i):
        return gains[i:i + 1, :]

    def gates(n):
        gl = mm(go + n * D_MODEL, go + (n + 1) * D_MODEL)
        g_ref[:, n * D_MODEL:(n + 1) * D_MODEL] = _sigmoid(gl).astype(BF16)

    def full_rms(r, g):
        return (r * lax.rsqrt(jnp.mean(r * r, axis=-1, keepdims=True) + EPS) * g).astype(BF16)

    if with_q:
        r_qa, r_qb, r_qc, cq = mm(0, 512), mm(512, 1024), mm(1024, 1536), mm(1536, 2048)
    kvab = mm(kvo, kvo + 512)
    r_kc = mm(kvo + 512, kvo + 1024)
    r_vc = mm(kvo + 1024, kvo + 1536)
    ckv = mm(kvo + 1536, kvo + 1792)
    r_kr = mm(kvo + 1792, kvo + 1920)
    if with_q:
        gates(0)
        gates(1)
        ms_qa, ms_qb, ms_qc = (_group_mean_sq(r, m64) for r in (r_qa, r_qb, r_qc))
        qd = _dot(full_rms(cq, vec_ref[1:2, 0:Q_LORA]), wuq_ref[...])
    r_ka, r_kb = kvab[:, 0:128], kvab[:, 256:384]
    ms_ka, ms_kb, ms_kc = (_group_mean_sq(r, m64) for r in (r_ka, r_kb, r_kc))
    ms_kr = _group_mean_sq(r_kr, md)
    kvd = _dot(full_rms(ckv, vec_ref[1:2, Q_LORA:Q_LORA + KV_LORA]), wukv_ref[...])
    if with_q:
        gates(2)
        ms_qd = _group_mean_sq(qd, md)
    r_kn = kvd[:, 0:D_HEADS * LANES]
    ms_kn = _group_mean_sq(r_kn, md)
    if with_q:
        gates(3)
        qa_ref[...] = _norm_rope(r_qa, ms_qa, gain(0), c64, s64, 32).astype(BF16)
        qb_ref[...] = _norm_rope(r_qb, ms_qb, gain(1), c64, s64, 32).astype(BF16)
        qc_ref[...] = _norm_rope(r_qc, ms_qc, gain(2), c64, s64, 32).astype(BF16)
        qd_ref[...] = _norm_rope(qd, ms_qd, gain(6), cd, sd, 16).astype(BF16)
    ka_ref[...] = _norm_rope(r_ka, ms_ka, gain(3), c64, s64, 32).astype(BF16)
    va_ref[...] = kvab[:, 128:256].astype(BF16)
    kb_ref[...] = _norm_rope(r_kb, ms_kb, gain(4), c64, s64, 32).astype(BF16)
    vb_ref[...] = kvab[:, 384:512].astype(BF16)
    kc_ref[...] = _norm_rope(r_kc, ms_kc, gain(5), c64, s64, 32).astype(BF16)
    vc_ref[...] = r_vc.astype(BF16)
    kr = _norm_rope(r_kr, ms_kr, gain(8), cd, sd, 16)
    kn = _norm_rope(r_kn, ms_kn, gain(7), None, None, 0)
    kd_ref[...] = (kn + jnp.concatenate([kr] * D_HEADS, axis=1)).astype(BF16)
    vd_ref[...] = kvd[:, D_HEADS * LANES:].astype(BF16)


_KV_WIDTHS = (128, 128, 128, 128, 512, 512, D_HEADS * LANES, D_HEADS * V_HEAD)


def _project(x, mod, vec, gains, rope, mavg, w, wuq, wukv, *, seq, tm, with_q):
    n = x.shape[0]
    tps = seq // tm
    grid = (n // tm,)

    def tok(i):
        return (i, 0)

    in_specs = [
        pl.BlockSpec((tm, D_MODEL), tok),
        pl.BlockSpec((1, 6, D_MODEL), lambda i: (i // tps, 0, 0)),
        _resident(vec.shape),
        _resident(gains.shape),
        pl.BlockSpec((4, tm, LANES), lambda i: (0, i % tps, 0)),
        _resident(mavg.shape),
        _resident(w.shape),
        _resident(wuq.shape),
        _resident(wukv.shape),
    ]
    args = [x, mod, vec, gains, rope, mavg, w, wuq, wukv]
    out_shapes = [jax.ShapeDtypeStruct((n, wd), BF16) for wd in _KV_WIDTHS]
    out_specs = [pl.BlockSpec((tm, wd), tok) for wd in _KV_WIDTHS]
    if with_q:
        for wd in (512, 512, 512, D_HEADS * LANES, GATE_COLS):
            out_shapes.append(jax.ShapeDtypeStruct((n, wd), BF16))
            out_specs.append(pl.BlockSpec((tm, wd), tok))
    return pl.pallas_call(
        functools.partial(_proj_kernel, with_q),
        out_shape=out_shapes, grid=grid, in_specs=in_specs, out_specs=out_specs,
        compiler_params=_cparams(("arbitrary",)),
        name="project_q" if with_q else "project_kv",
    )(*args)


LOGIT_SAFE = 40.0


def _softmax_pv(shift, q, kv, mask=None, extra_logit=None):
    scores = [_dot_nt(q, k) for k, _ in kv]
    if mask is not None:
        scores[0] = jnp.where(mask, scores[0], NEG_INF)
    if shift:
        m = functools.reduce(jnp.maximum, [jnp.max(s, axis=-1, keepdims=True) for s in scores])
        if extra_logit is not None:
            m = jnp.maximum(m, extra_logit)
            extra_logit = extra_logit - m
        scores = [s - m for s in scores]
    probs = [jnp.exp2(s) for s in scores]
    l = functools.reduce(jnp.add, [jnp.sum(p, axis=-1, keepdims=True) for p in probs])
    if extra_logit is not None:
        l = l + jnp.exp2(extra_logit)
    o = functools.reduce(jnp.add, [_dot(p.astype(BF16), v) for p, (_, v) in zip(probs, kv)])
    return o / l


def _guarded(body, idx, safe_ref, *refs):
    @pl.when(safe_ref[idx] != 0)
    def _():
        body(False, *refs)

    @pl.when(safe_ref[idx] == 0)
    def _():
        body(True, *refs)


def _split_heads(qs):
    lane = lax.broadcasted_iota(jnp.int32, qs.shape, 1)
    zero = jnp.zeros_like(qs)
    return jnp.concatenate([jnp.where(lane < HD, qs, zero), jnp.where(lane >= HD, qs, zero)], axis=0)


def _merge_halves(o, tq):
    lane = lax.broadcasted_iota(jnp.int32, (tq, LANES), 1)
    return jnp.where(lane < HD, o[:tq], o[tq:])


def _kv_cols(kv_refs, kc, vc):
    return [(k_ref[:, kc * LANES:(kc + 1) * LANES], v_ref[:, vc * LANES:(vc + 1) * LANES])
            for k_ref, v_ref in kv_refs]


def _one_group(groups):
    return [(jnp.concatenate([k for k, _ in groups], axis=0), jnp.concatenate([v for _, v in groups], axis=0))]


def _mix_b(shift, q_ref, kv_refs, o_ref):
    tq = q_ref.shape[0]
    kv = _kv_cols(kv_refs, 0, 0)

    def unit(j):
        qq = _split_heads(q_ref[:, j * LANES:(j + 1) * LANES])
        o = _softmax_pv(shift, qq, kv)
        o_ref[:, j * LANES:(j + 1) * LANES] = _merge_halves(o, tq).astype(BF16)

    return [functools.partial(unit, j) for j in range(4)]


def _mix_a(shift, sink_ref, q_ref, kv_refs, o_ref):
    tq = q_ref.shape[0]
    if len(kv_refs) == 2:
        (kl_ref, vl_ref), (kc_ref, vc_ref) = kv_refs
        span = tq + 2 * WINDOW
        q0 = pl.program_id(1) * tq
        start = pl.multiple_of(jnp.clip(q0 - WINDOW, 0, kl_ref.shape[0] - span), BLOCK)
        kv = [(kl_ref[pl.ds(start, span), :], vl_ref[pl.ds(start, span), :]), (kc_ref[...], vc_ref[...])]
        row = lax.broadcasted_iota(jnp.int32, (2 * tq, span), 0)
        col = lax.broadcasted_iota(jnp.int32, (2 * tq, span), 1)
        qpos = q0 + jnp.where(row >= tq, row - tq, row)
        dist = (start + col) - qpos
        valid = (dist >= -WINDOW) & (dist <= WINDOW)
    else:
        kv = _kv_cols(kv_refs, 0, 0)
        valid = None
    rows = lax.broadcasted_iota(jnp.int32, (2 * tq, 1), 0)

    def unit(j):
        qq = _split_heads(q_ref[:, j * LANES:(j + 1) * LANES])
        sink = jnp.where(rows < tq, sink_ref[2 * j], sink_ref[2 * j + 1]) * LOG2E
        o = _softmax_pv(shift, qq, kv, mask=valid, extra_logit=sink)
        o_ref[:, j * LANES:(j + 1) * LANES] = _merge_halves(o, tq).astype(BF16)

    return [functools.partial(unit, j) for j in range(4)]


def _mix_c(lam_init, shift, lq1_ref, lk1_ref, lq2_ref, lk2_ref, subln_ref, q_ref, kv_refs, o_ref):
    tq = q_ref.shape[0]
    lam = (jnp.exp(jnp.sum(lq1_ref[...] * lk1_ref[...], axis=-1, keepdims=True))
           - jnp.exp(jnp.sum(lq2_ref[...] * lk2_ref[...], axis=-1, keepdims=True)) + lam_init)
    for c in range(C_HEADS):
        qq = _split_heads(q_ref[:, c * LANES:(c + 1) * LANES])
        o = _softmax_pv(shift, qq, _one_group(_kv_cols(kv_refs, c, c)))
        oc = o[:tq] - lam * o[tq:]
        oc = oc * lax.rsqrt(jnp.mean(oc * oc, axis=-1, keepdims=True) + EPS) * subln_ref[...]
        o_ref[:, c * LANES:(c + 1) * LANES] = (oc * (1.0 - lam_init)).astype(BF16)


def _mix_d(shift, q_ref, kv_refs, o_ref):
    tq = q_ref.shape[0]
    for hp in range(D_HEADS // 2):
        halves = [_softmax_pv(shift, q_ref[:, h * LANES:(h + 1) * LANES], _kv_cols(kv_refs, h, hp))
                  for h in (2 * hp, 2 * hp + 1)]
        o = jnp.concatenate(halves, axis=0)
        o_ref[:, hp * LANES:(hp + 1) * LANES] = _merge_halves(o, tq).astype(BF16)


def _mix_ab(shift, sink_ref, qa_ref, kva_refs, oa_ref, qb_ref, kvb_refs, ob_ref):
    for unit_b, unit_a in zip(_mix_b(shift, qb_ref, kvb_refs, ob_ref),
                              _mix_a(shift, sink_ref, qa_ref, kva_refs, oa_ref)):
        unit_b()
        unit_a()


ATTN_TQ = 256


def _attention(body, safe_idx, safe, extra, extra_specs, units, *, seq_q, name):
    n = units[0][0].shape[0]
    nb = n // seq_q
    tq = ATTN_TQ
    nq = seq_q // tq
    tok = lambda b, i: (b * nq + i, 0)
    per_batch = lambda a: pl.BlockSpec((a.shape[0] // nb, a.shape[1]), lambda b, i: (b, 0))
    args, in_specs, counts = [], [], []
    for q, kv in units:
        args.append(q)
        in_specs.append(pl.BlockSpec((tq, q.shape[1]), tok))
        for pair in kv:
            args.extend(pair)
            in_specs.extend(per_batch(a) for a in pair)
        counts.append(len(kv))
    n_extra = len(extra)

    def kernel(shift, *refs):
        ins, outs = refs[n_extra:len(refs) - len(units)], refs[len(refs) - len(units):]
        packed, pos = [], 0
        for u, nkv in enumerate(counts):
            kv_refs = [(ins[pos + 1 + 2 * g], ins[pos + 2 + 2 * g]) for g in range(nkv)]
            packed.extend([ins[pos], kv_refs, outs[u]])
            pos += 1 + 2 * nkv
        body(shift, *refs[:n_extra], *packed)

    out = pl.pallas_call(
        functools.partial(_guarded, kernel, safe_idx),
        out_shape=[jax.ShapeDtypeStruct((n, BRANCH_W), BF16)] * len(units),
        grid=(nb, nq),
        in_specs=[pl.BlockSpec(memory_space=pltpu.SMEM)] + list(extra_specs) + in_specs,
        out_specs=[pl.BlockSpec((tq, BRANCH_W), tok)] * len(units),
        compiler_params=_cparams(("arbitrary", "arbitrary")),
        name=name,
    )(safe, *extra, *args)
    return out


def _mixers(qs, kv_groups, lp, *, seq_q, tag):
    qa, qb, qc, qd = qs
    pairs = lambda j: [(g[2 * j], g[2 * j + 1]) for g in kv_groups]
    safe = lp["safe"]
    row = lambda w: pl.BlockSpec((1, w), lambda b, i: (0, 0))
    oa, ob = _attention(_mix_ab, 0, safe, [lp["sink"]], [pl.BlockSpec(memory_space=pltpu.SMEM)],
                        [(qa, pairs(0)), (qb, pairs(1))], seq_q=seq_q, name="attn_ab_" + tag)
    oc, = _attention(functools.partial(_mix_c, lp["lam_init"]), 1, safe,
                     [lp["lq1"], lp["lk1"], lp["lq2"], lp["lk2"], lp["subln"]],
                     [row(HD)] * 4 + [row(2 * HD)], [(qc, pairs(2))], seq_q=seq_q, name="attn_c_" + tag)
    od, = _attention(_mix_d, 2, safe, [], [], [(qd, pairs(3))], seq_q=seq_q, name="attn_d_" + tag)
    return oa, ob, oc, od


def _merge_kernel(moe, oa_ref, ob_ref, oc_ref, od_ref, g_ref, x_ref, mod_ref, norm_ref,
                  wbr_ref, wout_ref, *refs):
    if moe:
        wr_ref, xo_ref, h_ref, lg_ref = refs
    else:
        xo_ref, h_ref = refs
    m = mod_ref[0]
    y = None
    for n, o_ref in enumerate((oa_ref, ob_ref, oc_ref, od_ref)):
        yn = g_ref[:, n * D_MODEL:(n + 1) * D_MODEL].astype(F32) * _dot(o_ref[...], wbr_ref[n])
        y = yn if y is None else y + yn
    z = _dot(y.astype(BF16), wout_ref[...])
    xn = x_ref[...] + m[2:3] * z
    xo_ref[...] = xn
    h = _adaln(xn, norm_ref[...], m[3:4], m[4:5])
    if moe:
        h_ref[...] = h
        h_hi = h.astype(BF16)
        h_lo = (h - h_hi.astype(F32)).astype(BF16)
        lg_ref[...] = (_dot_nt(wr_ref[0], h_hi) + _dot_nt(wr_ref[1], h_hi)) + _dot_nt(wr_ref[0], h_lo)
    else:
        h_ref[...] = h.astype(BF16)


def _merge(branches, g, x, mod, norm, wbr, wout, router_t, *, seq):
    n = x.shape[0]
    tm = min(MERGE_TM, seq)
    tps = seq // tm
    moe = router_t is not None
    tok = lambda i: (i, 0)
    in_specs = [pl.BlockSpec((tm, BRANCH_W), tok)] * 4 + [
        pl.BlockSpec((tm, GATE_COLS), tok),
        pl.BlockSpec((tm, D_MODEL), tok),
        pl.BlockSpec((1, 6, D_MODEL), lambda i: (i // tps, 0, 0)),
        _resident(norm.shape), _resident(wbr.shape), _resident(wout.shape)]
    args = list(branches) + [g, x, mod, norm, wbr, wout]
    out_shape = [jax.ShapeDtypeStruct((n, D_MODEL), F32),
                 jax.ShapeDtypeStruct((n, D_MODEL), F32 if moe else BF16)]
    out_specs = [pl.BlockSpec((tm, D_MODEL), tok), pl.BlockSpec((tm, D_MODEL), tok)]
    if moe:
        in_specs.append(_resident(router_t.shape))
        args.append(router_t)
        out_shape.append(jax.ShapeDtypeStruct((N_EXPERTS, n), F32))
        out_specs.append(pl.BlockSpec((N_EXPERTS, tm), lambda i: (0, i)))
    return pl.pallas_call(
        functools.partial(_merge_kernel, moe),
        out_shape=out_shape, grid=(n // tm,), in_specs=in_specs, out_specs=out_specs,
        compiler_params=_cparams(("arbitrary",)),
        name="merge_moe" if moe else "merge_dense",
    )(*args)


def _ffn_kernel(h_ref, x_ref, mod_ref, wg_ref, wu_ref, wd_ref, o_ref, hid_ref):
    h = h_ref[...]
    for c in range(0, D_FF, FFN_TF):
        a = _dot(h, wg_ref[:, c:c + FFN_TF])
        u = _dot(h, wu_ref[:, c:c + FFN_TF])
        hid_ref[:, c:c + FFN_TF] = (a * _sigmoid(a) * u).astype(BF16)
    m = mod_ref[0]
    o_ref[...] = x_ref[...] + m[5:6] * _dot(hid_ref[...], wd_ref[...])


def _dense_ffn(h, x, mod, wg, wu, wd, *, seq):
    n = x.shape[0]
    tm = min(FFN_TM, seq)
    tps = seq // tm
    tok = lambda i: (i, 0)
    return pl.pallas_call(
        _ffn_kernel,
        out_shape=jax.ShapeDtypeStruct((n, D_MODEL), F32),
        grid=(n // tm,),
        in_specs=[pl.BlockSpec((tm, D_MODEL), tok), pl.BlockSpec((tm, D_MODEL), tok),
                  pl.BlockSpec((1, 6, D_MODEL), lambda i: (i // tps, 0, 0)),
                  _resident(wg.shape), _resident(wu.shape), _resident(wd.shape)],
        out_specs=pl.BlockSpec((tm, D_MODEL), tok),
        scratch_shapes=[pltpu.VMEM((tm, D_FF), BF16)],
        compiler_params=_cparams(("arbitrary",)),
        name="dense_ffn",
    )(h, x, mod, wg, wu, wd)


ROUTE_CHUNK = 512
DMA_UNROLL = 8


def _route_kernel(n, lg_ref, tri_ref, dest_ref, wts_ref, meta_ref, mem_ref, pos_ref):
    lg = lg_ref[...]
    eidx = lax.broadcasted_iota(jnp.int32, lg.shape, 0).astype(F32)
    none = float(N_EXPERTS)
    m1 = jnp.max(lg, axis=0, keepdims=True)
    i1 = jnp.min(jnp.where(lg == m1, eidx, none), axis=0, keepdims=True)
    lg2 = jnp.where(eidx == i1, -jnp.inf, lg)
    m2 = jnp.max(lg2, axis=0, keepdims=True)
    i2 = jnp.min(jnp.where(lg2 == m2, eidx, none), axis=0, keepdims=True)
    e = jnp.exp(m2 - m1)
    w1 = 1.0 / (1.0 + e)
    wts_ref[0:1, :] = w1
    wts_ref[1:2, :] = e * w1
    mem_ref[...] = jnp.where(eidx == i1, 1.0, 0.0) + jnp.where(eidx == i2, 1.0, 0.0)

    carry = jnp.zeros((N_EXPERTS, 1), F32)
    tri = tri_ref[...]
    for c in range(0, n, ROUTE_CHUNK):
        mc = mem_ref[:, c:c + ROUTE_CHUNK]
        pos_ref[:, c:c + ROUTE_CHUNK] = carry + _dot(mc.astype(BF16), tri)
        carry = carry + jnp.sum(mc, axis=1, keepdims=True)
    padded = jnp.floor((carry + (MOE_TM - 1.0)) * (1.0 / MOE_TM)) * MOE_TM
    esub = lax.broadcasted_iota(jnp.int32, (N_EXPERTS, 1), 0)
    pstart = jnp.zeros((N_EXPERTS, 1), F32)
    for k in range(N_EXPERTS - 1):
        pstart = pstart + jnp.where(esub > k, padded[k:k + 1, :], 0.0)
    pend = pstart + padded
    slot = pstart + pos_ref[...]
    dest_ref[0:1, :] = jnp.sum(jnp.where(eidx == i1, slot, 0.0), axis=0, keepdims=True).astype(jnp.int32)
    dest_ref[1:2, :] = jnp.sum(jnp.where(eidx == i2, slot, 0.0), axis=0, keepdims=True).astype(jnp.int32)

    nblk = pend[N_EXPERTS - 1:N_EXPERTS, :] * (1.0 / MOE_TM)
    brow = jnp.minimum(lax.broadcasted_iota(jnp.int32, (1, LANES), 1).astype(F32), nblk - 1.0) * MOE_TM
    be = jnp.zeros((1, LANES), F32)
    for k in range(N_EXPERTS):
        be = be + jnp.where(pend[k:k + 1, :] <= brow, 1.0, 0.0)
    meta_ref[...] = jnp.zeros(meta_ref.shape, jnp.int32)
    meta_ref[0:1, :] = jnp.minimum(be, N_EXPERTS - 1.0).astype(jnp.int32)
    meta_ref[1:2, :] = jnp.broadcast_to(nblk, (1, LANES)).astype(jnp.int32)


def _route(logits_t):
    n = logits_t.shape[1]
    nblk_max = n * TOP_K // MOE_TM + N_EXPERTS
    assert nblk_max <= LANES and n % ROUTE_CHUNK == 0
    tri = jnp.asarray(np.triu(np.ones((ROUTE_CHUNK, ROUTE_CHUNK)), 1), BF16)
    whole = lambda shape: pl.BlockSpec(shape, lambda: (0,) * len(shape))
    dest, wts, meta = pl.pallas_call(
        functools.partial(_route_kernel, n),
        out_shape=[jax.ShapeDtypeStruct((TOP_K, n), jnp.int32),
                   jax.ShapeDtypeStruct((TOP_K, n), F32),
                   jax.ShapeDtypeStruct((8, LANES), jnp.int32)],
        in_specs=[whole(logits_t.shape), whole(tri.shape)],
        out_specs=[whole((TOP_K, n)), whole((TOP_K, n)), whole((8, LANES))],
        scratch_shapes=[pltpu.VMEM((N_EXPERTS, n), F32), pltpu.VMEM((N_EXPERTS, n), F32)],
        compiler_params=pltpu.CompilerParams(vmem_limit_bytes=VMEM_LIMIT),
        name="moe_route",
    )(logits_t, tri)
    return dest, wts, meta[0, :nblk_max], meta[1, :1], nblk_max


def _dispatch_kernel(tg, dest_ref, h_ref, xin_hbm, xbuf_hbm, sem):
    del xin_hbm

    def copies(r):
        src = h_ref.at[pl.ds(r, 1)]
        return (pltpu.make_async_copy(src, xbuf_hbm.at[pl.ds(dest_ref[0, 0, r], 1)], sem),
                pltpu.make_async_copy(src, xbuf_hbm.at[pl.ds(dest_ref[0, 0, tg + r], 1)], sem))

    def start(r, carry):
        for k, cp in enumerate(copies(r)):
            cp.start(priority=k)
        return carry

    def wait(r, carry):
        for cp in copies(r):
            cp.wait()
        return carry

    lax.fori_loop(0, tg, start, 0, unroll=DMA_UNROLL)
    lax.fori_loop(0, tg, wait, 0, unroll=DMA_UNROLL)


def _tile_dest(dest, t):
    n = dest.shape[1]
    return jnp.transpose(dest.reshape(TOP_K, n // t, t), (1, 0, 2)).reshape(n // t, 1, TOP_K * t)


def _dispatch(h, dest, nrows):
    n = h.shape[0]
    tg = 512
    nt = n // tg
    dest_t = _tile_dest(dest, tg)
    xbuf0 = jnp.zeros((nrows, D_MODEL), F32)
    return pl.pallas_call(
        functools.partial(_dispatch_kernel, tg),
        out_shape=jax.ShapeDtypeStruct((nrows, D_MODEL), F32),
        grid=(nt,),
        in_specs=[pl.BlockSpec((1, 1, TOP_K * tg), lambda i: (i, 0, 0), memory_space=pltpu.SMEM),
                  pl.BlockSpec((tg, D_MODEL), lambda i: (i, 0)), pl.BlockSpec(memory_space=pl.ANY)],
        out_specs=pl.BlockSpec(memory_space=pl.ANY),
        scratch_shapes=[pltpu.SemaphoreType.DMA(())],
        input_output_aliases={2: 0},
        compiler_params=_cparams(("arbitrary",)),
        name="moe_dispatch",
    )(dest_t, h, xbuf0)


def _moe_kernel(nf, blk_e_ref, nblk_ref, x_ref, w1_ref, w3_ref, w2_ref, o_ref, xb_ref, hid_ref):
    del blk_e_ref
    i = pl.program_id(0)
    j = pl.program_id(1)

    @pl.when(i < nblk_ref[0])
    def _():
        @pl.when(j == 0)
        def _():
            xb_ref[...] = x_ref[...].astype(BF16)

        xb = xb_ref[...]
        for c in range(0, MOE_TF, FFN_TF):
            a = _dot(xb, w1_ref[0, :, c:c + FFN_TF])
            u = _dot(xb, w3_ref[0, :, c:c + FFN_TF])
            hid_ref[:, c:c + FFN_TF] = (a * _sigmoid(a) * u).astype(BF16)
        y = _dot(hid_ref[...], w2_ref[0])

        @pl.when(j == 0)
        def _():
            o_ref[...] = y

        @pl.when(j > 0)
        def _():
            o_ref[...] += y

    @pl.when((i >= nblk_ref[0]) & (j == nf - 1))
    def _():
        o_ref[...] = jnp.zeros_like(o_ref)


def _moe_ffn(xbuf, blk_e, nblk, nblk_max, w1, w3, w2):
    nf = EXPERT_FF // MOE_TF

    def rows(i, j, be, nb):
        return (i, 0)

    def jj(i, j, nb):
        return jnp.where(i < nb[0], j, nf - 1)

    grid_spec = pltpu.PrefetchScalarGridSpec(
        num_scalar_prefetch=2, grid=(nblk_max, nf),
        in_specs=[pl.BlockSpec((MOE_TM, D_MODEL), rows),
                  pl.BlockSpec((1, D_MODEL, MOE_TF), lambda i, j, be, nb: (be[i], 0, jj(i, j, nb))),
                  pl.BlockSpec((1, D_MODEL, MOE_TF), lambda i, j, be, nb: (be[i], 0, jj(i, j, nb))),
                  pl.BlockSpec((1, MOE_TF, D_MODEL), lambda i, j, be, nb: (be[i], jj(i, j, nb), 0))],
        out_specs=pl.BlockSpec((MOE_TM, D_MODEL), rows),
        scratch_shapes=[pltpu.VMEM((MOE_TM, D_MODEL), BF16), pltpu.VMEM((MOE_TM, MOE_TF), BF16)])
    return pl.pallas_call(
        functools.partial(_moe_kernel, nf),
        out_shape=jax.ShapeDtypeStruct(xbuf.shape, F32),
        grid_spec=grid_spec,
        compiler_params=_cparams(("arbitrary", "arbitrary")),
        name="moe_experts",
    )(blk_e, nblk, xbuf, w1, w3, w2)


def _combine_kernel(tc, nt, dest_ref, dest_next_ref, wts_ref, x_ref, mod_ref, y_hbm, o_ref, yg_ref, sems):
    i = pl.program_id(0)
    slot = lax.rem(i, 2)

    def copies(d_ref, s, r):
        return (pltpu.make_async_copy(y_hbm.at[pl.ds(d_ref[0, 0, r], 1)],
                                      yg_ref.at[s, 0, pl.ds(r, 1)], sems.at[s]),
                pltpu.make_async_copy(y_hbm.at[pl.ds(d_ref[0, 0, tc + r], 1)],
                                      yg_ref.at[s, 1, pl.ds(r, 1)], sems.at[s]))

    def request(d_ref, s):
        def start(r, carry):
            for k, cp in enumerate(copies(d_ref, s, r)):
                cp.start(priority=k)
            return carry
        lax.fori_loop(0, tc, start, 0, unroll=DMA_UNROLL)

    @pl.when(i == 0)
    def _():
        request(dest_ref, slot)

    @pl.when(i + 1 < nt)
    def _():
        request(dest_next_ref, 1 - slot)

    def wait(r, carry):
        for cp in copies(dest_ref, slot, r):
            cp.wait()
        return carry

    lax.fori_loop(0, tc, wait, 0, unroll=DMA_UNROLL)
    m = mod_ref[0]
    w = wts_ref[...]
    y = w[:, 0:1] * yg_ref[slot, 0] + w[:, 1:2] * yg_ref[slot, 1]
    o_ref[...] = x_ref[...] + m[5:6] * y


def _combine(ybuf, dest, wts, x, mod, *, seq):
    n = x.shape[0]
    tc = 256
    nt = n // tc
    tps = seq // tc
    dest_t = _tile_dest(dest, tc)
    wts = wts.T
    tok = lambda i: (i, 0)
    slots = lambda step: pl.BlockSpec((1, 1, TOP_K * tc), lambda i: (jnp.minimum(i + step, nt - 1), 0, 0),
                                      memory_space=pltpu.SMEM)
    return pl.pallas_call(
        functools.partial(_combine_kernel, tc, nt),
        out_shape=jax.ShapeDtypeStruct((n, D_MODEL), F32),
        grid=(nt,),
        in_specs=[slots(0), slots(1),
                  pl.BlockSpec((tc, TOP_K), tok),
                  pl.BlockSpec((tc, D_MODEL), tok),
                  pl.BlockSpec((1, 6, D_MODEL), lambda i: (i // tps, 0, 0)),
                  pl.BlockSpec(memory_space=pl.ANY)],
        out_specs=pl.BlockSpec((tc, D_MODEL), tok),
        scratch_shapes=[pltpu.VMEM((2, TOP_K, tc, D_MODEL), F32), pltpu.SemaphoreType.DMA((2,))],
        compiler_params=_cparams(("arbitrary",)),
        name="moe_combine",
    )(dest_t, dest_t, wts, x, mod, ybuf)


def _prep_w_in(w):
    kv_lo = Q_COLS
    kr_lo = kv_lo + KV_PAD_COLS - LANES
    heads = lambda base: [w[:, base + h * HD:base + (h + 1) * HD] for h in HEAD_PERM]
    parts = heads(0) + heads(512) + [
             w[:, 1024:Q_COLS],
             w[:, kv_lo:kr_lo],
             jnp.pad(w[:, kr_lo:kr_lo + QK_ROPE], ((0, 0), (QK_NOPE, LANES - QK_NOPE - QK_ROPE))),
             w[:, kr_lo + QK_ROPE:]]
    return jnp.concatenate(parts, axis=1).astype(BF16)


def _prep_layer(l, p):
    zeros = lambda k: jnp.zeros((k,), F32)
    sc = LOG2E / math.sqrt(HD)
    sc_d = LOG2E / math.sqrt(QK_NOPE + QK_ROPE)
    wuq = p["d_w_uq"][l].reshape(Q_LORA, D_HEADS, QK_NOPE + QK_ROPE)
    wuq = jnp.pad(wuq, ((0, 0), (0, 0), (0, LANES - QK_NOPE - QK_ROPE))).reshape(Q_LORA, D_HEADS * LANES)
    wukv = p["d_w_ukv"][l].reshape(KV_LORA, D_HEADS, QK_NOPE + V_HEAD)
    wuk = jnp.pad(wukv[:, :, :QK_NOPE], ((0, 0), (0, 0), (0, LANES - QK_NOPE))).reshape(KV_LORA, D_HEADS * LANES)
    wuv = wukv[:, :, QK_NOPE:].reshape(KV_LORA, D_HEADS * V_HEAD)
    gains = jnp.stack([
        jnp.tile(p["a_qn"][l], 2) * sc, jnp.tile(p["b_qn"][l], 2) * sc, jnp.tile(p["c_qn"][l], 2) * sc,
        jnp.tile(p["a_kn"][l], 2), jnp.tile(p["b_kn"][l], 2), jnp.tile(p["c_kn"][l], 2),
        jnp.concatenate([p["d_qn_nope"][l], p["d_qn_rope"][l], zeros(32)]) * sc_d,
        jnp.concatenate([p["d_kn_nope"][l], zeros(64)]),
        jnp.concatenate([zeros(64), p["d_kn_rope"][l], zeros(32)]),
    ] + [zeros(LANES)] * 7).astype(F32)
    vec = jnp.stack([p["mix_norm"][l],
                     jnp.concatenate([p["d_q_norm"][l], p["d_kv_norm"][l], zeros(D_MODEL - Q_LORA - KV_LORA)])]
                    + [zeros(D_MODEL)] * 6).astype(F32)
    wbr = p["w_br"][l]
    head_rows = lambda m: jnp.concatenate([m[h * HD:(h + 1) * HD] for h in HEAD_PERM], axis=0)
    wbr = jnp.stack([head_rows(wbr[0]), head_rows(wbr[1]), wbr[2], wbr[3]]).astype(BF16)

    amax = lambda v: jnp.max(jnp.abs(v))
    bound64 = lambda gq, gk: 1.02 * HD * amax(gq) * amax(gk)
    nq_d = jnp.sqrt(QK_NOPE * amax(gains[6, :QK_NOPE]) ** 2 + QK_ROPE * amax(gains[6, QK_NOPE:]) ** 2)
    nk_d = jnp.sqrt(QK_NOPE * amax(gains[7]) ** 2 + QK_ROPE * amax(gains[8]) ** 2)
    bounds = jnp.stack([
        jnp.maximum(jnp.maximum(bound64(gains[0], gains[3]), bound64(gains[1], gains[4])),
                    LOG2E * amax(p["a_sink"][l])),
        bound64(gains[2], gains[5]),
        1.02 * nq_d * nk_d])
    safe = (bounds <= LOGIT_SAFE).astype(jnp.int32)
    return {
        "safe": safe,
        "w_in": _prep_w_in(p["w_in"][l]),
        "wuq": wuq.astype(BF16),
        "wukv": jnp.concatenate([wuk, wuv], axis=1).astype(BF16),
        "gains": gains, "vec": vec, "wbr": wbr, "wout": p["w_out"][l].astype(BF16),
        "ffn_norm": p["ffn_norm"][l].reshape(1, D_MODEL),
        "sink": p["a_sink"][l][np.asarray(HEAD_PERM)].astype(F32),
        "lq1": p["c_lq1"][l].reshape(1, HD), "lk1": p["c_lk1"][l].reshape(1, HD),
        "lq2": p["c_lq2"][l].reshape(1, HD), "lk2": p["c_lk2"][l].reshape(1, HD),
        "subln": p["c_subln"][l].reshape(1, 2 * HD),
        "lam_init": 0.8 - 0.6 * math.exp(-0.3 * l),
    }


def _group_avg_mats():
    m64 = np.kron(np.eye(4), np.full((HD, HD), 1.0 / HD))
    one = np.zeros((LANES, LANES))
    one[:QK_NOPE, :QK_NOPE] = 1.0 / QK_NOPE
    one[QK_NOPE:QK_NOPE + QK_ROPE, QK_NOPE:QK_NOPE + QK_ROPE] = 1.0 / QK_ROPE
    md = np.kron(np.eye(2), one)
    return jnp.asarray(np.stack([m64, md]), BF16)


def _rope_tables(seq, rotary):
    if not rotary:
        one = jnp.ones((seq, LANES), F32)
        zero = jnp.zeros((seq, LANES), F32)
        return jnp.stack([one, zero, one, zero])
    t = jnp.arange(seq)

    def angles(rot_dim):
        nfreq = rot_dim // 4
        inv = jnp.power(ROPE_THETA, -jnp.arange(nfreq, dtype=F32) / nfreq)
        return jnp.concatenate([(t // GRID_W).astype(F32)[:, None] * inv,
                                (t % GRID_W).astype(F32)[:, None] * inv], axis=-1)

    ah = angles(HD)
    c64 = jnp.tile(jnp.cos(ah), (1, 4))
    s64 = jnp.tile(jnp.concatenate([-jnp.sin(ah), jnp.sin(ah)], axis=1), (1, 2))
    ar = angles(QK_ROPE)
    one = jnp.ones((seq, 1), F32)
    cd = jnp.concatenate([one * jnp.ones((1, QK_NOPE), F32), jnp.cos(ar), jnp.cos(ar),
                          one * jnp.ones((1, LANES - QK_NOPE - QK_ROPE), F32)], axis=1)
    sd = jnp.concatenate([jnp.zeros((seq, QK_NOPE), F32), -jnp.sin(ar), jnp.sin(ar),
                          jnp.zeros((seq, LANES - QK_NOPE - QK_ROPE), F32)], axis=1)
    return jnp.stack([c64, s64, cd, sd])


def kernel(x, c, ctx, c_ctx, w_mod, b_mod, mix_norm, ffn_norm, w_in, a_qn, a_kn, a_sink, b_qn, b_kn, c_qn, c_kn, c_lq1, c_lk1, c_lq2, c_lk2, c_subln, d_q_norm, d_kv_norm, d_w_uq, d_w_ukv, d_qn_nope, d_kn_nope, d_qn_rope, d_kn_rope, w_br, w_out, ff_w_gate, ff_w_up, ff_w_down, moe_router, moe_w1, moe_w3, moe_w2):
    p = dict(mix_norm=mix_norm, ffn_norm=ffn_norm, w_in=w_in, a_qn=a_qn, a_kn=a_kn, a_sink=a_sink,
             b_qn=b_qn, b_kn=b_kn, c_qn=c_qn, c_kn=c_kn, c_lq1=c_lq1, c_lk1=c_lk1, c_lq2=c_lq2,
             c_lk2=c_lk2, c_subln=c_subln, d_q_norm=d_q_norm, d_kv_norm=d_kv_norm, d_w_uq=d_w_uq,
             d_w_ukv=d_w_ukv, d_qn_nope=d_qn_nope, d_kn_nope=d_kn_nope, d_qn_rope=d_qn_rope,
             d_kn_rope=d_kn_rope, w_br=w_br, w_out=w_out)
    nb, seq, _ = x.shape
    assert seq % TM == 0 and ctx.shape[1] == CTX_LEN and nb <= 15
    depth = w_mod.shape[0]
    lat = x.reshape(nb * seq, D_MODEL)
    cx = ctx.reshape(nb * CTX_LEN, D_MODEL)

    cond = jnp.zeros((16, D_MODEL), F32).at[:nb].set(c).at[nb].set(c_ctx)
    mod_all = _modulation(cond, w_mod, b_mod).reshape(depth, 16, 6, D_MODEL)
    rope_lat = _rope_tables(seq, True)
    rope_ctx = _rope_tables(CTX_LEN, False)
    mavg = _group_avg_mats()

    for l in range(depth):
        last = l == depth - 1
        lp = _prep_layer(l, p)
        mod_lat = mod_all[l, :nb]
        mod_ctx = jnp.broadcast_to(mod_all[l, nb:nb + 1], (nb, 6, D_MODEL))
        common = (lp["vec"], lp["gains"])
        wts = (lp["wuq"], lp["wukv"])
        if last:
            kv_ctx = _project(cx, mod_ctx, *common, rope_ctx, mavg,
                              lp["w_in"][:, Q_COLS:Q_COLS + KV_PAD_COLS], *wts, seq=CTX_LEN,
                              tm=CTX_LEN, with_q=False)
        else:
            outs = _project(cx, mod_ctx, *common, rope_ctx, mavg, lp["w_in"], *wts, seq=CTX_LEN,
                            tm=CTX_LEN, with_q=True)
            kv_ctx, q_ctx, g_ctx = outs[:8], outs[8:12], outs[12]
        outs = _project(lat, mod_lat, *common, rope_lat, mavg, lp["w_in"], *wts, seq=seq,
                        tm=TM, with_q=True)
        kv_lat, q_lat, g_lat = outs[:8], outs[8:12], outs[12]
        br_lat = _mixers(q_lat, [kv_lat, kv_ctx], lp, seq_q=seq, tag="lat")
        moe = l % 2 == 1
        i = l // 2
        router_t = None
        if moe:
            wr = moe_router[i].T
            wr_hi = wr.astype(BF16)
            router_t = jnp.stack([wr_hi, (wr - wr_hi.astype(F32)).astype(BF16)])
        res = _merge(br_lat, g_lat, lat, mod_lat, lp["ffn_norm"], lp["wbr"], lp["wout"], router_t, seq=seq)
        if not last:
            br_ctx = _mixers(q_ctx, [kv_ctx], lp, seq_q=CTX_LEN, tag="ctx")
            res_ctx = _merge(br_ctx, g_ctx, cx, mod_ctx, lp["ffn_norm"], lp["wbr"], lp["wout"],
                             router_t, seq=CTX_LEN)
        if not moe:
            wg, wu, wd = (ff_w_gate[i].astype(BF16), ff_w_up[i].astype(BF16), ff_w_down[i].astype(BF16))
            lat = _dense_ffn(res[1], res[0], mod_lat, wg, wu, wd, seq=seq)
            if not last:
                cx = _dense_ffn(res_ctx[1], res_ctx[0], mod_ctx, wg, wu, wd, seq=CTX_LEN)
        else:
            w1, w3, w2 = moe_w1[i].astype(BF16), moe_w3[i].astype(BF16), moe_w2[i].astype(BF16)

            def routed(res_t, mod_t, seq_t):
                xr, hr, lg = res_t
                dest, wts_r, blk_e, nblk, nblk_max = _route(lg)
                xbuf = _dispatch(hr, dest, nblk_max * MOE_TM)
                ybuf = _moe_ffn(xbuf, blk_e, nblk, nblk_max, w1, w3, w2)
                return _combine(ybuf, dest, wts_r, xr, mod_t, seq=seq_t)

            lat = routed(res, mod_lat, seq)
            if not last:
                cx = routed(res_ctx, mod_ctx, CTX_LEN)
    return lat.reshape(nb, seq, D_MODEL)
```

```python
import functools
import math

import numpy as np
import jax
import jax.numpy as jnp
from jax import lax
from jax.experimental import pallas as pl
from jax.experimental.pallas import tpu as pltpu

F32 = jnp.float32
BF16 = jnp.bfloat16

D_MODEL = 1024
DEPTH = 2
CTX_LEN = 256
GRID_W = 64
HD = 64
BLOCK = 128
WINDOW = 128
ROPE_THETA = 10000.0
EPS = 1e-6
NEG_INF = -1e30
A_HEADS = 8
C_HEADS = 4
D_HEADS = 8
Q_LORA = 512
KV_LORA = 256
QK_NOPE = 64
QK_ROPE = 32
V_HEAD = 64
BRANCH_W = 512
D_FF = 2816
N_EXPERTS = 8
TOP_K = 2
EXPERT_FF = 3584

LANES = 128
VMEM_LIMIT = 56 * 1024 * 1024
LOG2E = math.log2(math.e)

Q_COLS = 2048
KV_PAD_COLS = 1920
GATE_COLS = 4 * D_MODEL
W_COLS = Q_COLS + KV_PAD_COLS + GATE_COLS
HEAD_PERM = (0, 4, 1, 5, 2, 6, 3, 7)

TM = 512
MERGE_TM = 512
MOE_TM = 512
MOE_TF = 1792
FFN_TM = 512
FFN_TF = 256


def _cparams(sem):
    return pltpu.CompilerParams(dimension_semantics=sem, vmem_limit_bytes=VMEM_LIMIT)


def _resident(shape):
    nd = len(shape)
    return pl.BlockSpec(shape, lambda *_: (0,) * nd, pipeline_mode=pl.Buffered(1))


def _sigmoid(x):
    return 1.0 / (1.0 + jnp.exp(-x))


def _dot(a, b):
    return jnp.dot(a, b, preferred_element_type=F32)


def _dot_nt(a, b):
    return lax.dot_general(a, b, (((1,), (1,)), ((), ())), preferred_element_type=F32)


def _mod_kernel(c_ref, w_ref, b_ref, o_ref):
    cond = c_ref[...]
    s = cond * _sigmoid(cond)
    o_ref[0] = jnp.dot(s, w_ref[0], precision=lax.Precision.HIGHEST,
                       preferred_element_type=F32) + b_ref[0]


def _modulation(cond, w_mod, b_mod):
    depth = w_mod.shape[0]
    nct = 6 * D_MODEL // 1024
    return pl.pallas_call(
        _mod_kernel,
        out_shape=jax.ShapeDtypeStruct((depth, 16, 6 * D_MODEL), F32),
        grid=(depth, nct),
        in_specs=[pl.BlockSpec((16, D_MODEL), lambda l, j: (0, 0)),
                  pl.BlockSpec((1, D_MODEL, 1024), lambda l, j: (l, 0, j)),
                  pl.BlockSpec((1, 1, 1024), lambda l, j: (l, 0, j))],
        out_specs=pl.BlockSpec((1, 16, 1024), lambda l, j: (l, 0, j)),
        compiler_params=_cparams(("arbitrary", "arbitrary")),
        name="modulation",
    )(cond, w_mod, b_mod.reshape(depth, 1, 6 * D_MODEL))


def _adaln(x, g, shift, scale):
    ms = jnp.mean(x * x, axis=-1, keepdims=True)
    y = x * lax.rsqrt(ms + EPS) * g
    return y * (1.0 + scale) + shift


def _group_mean_sq(r, mavg):
    sq = (r * r).astype(BF16)
    w = r.shape[1]
    parts = []
    for c in range(0, w, 2 * LANES):
        cw = min(2 * LANES, w - c)
        parts.append(_dot(sq[:, c:c + cw], mavg[:cw, :cw]))
    return parts[0] if len(parts) == 1 else jnp.concatenate(parts, axis=1)


def _rope_slab(y, cos, sin, half):
    lane = lax.broadcasted_iota(jnp.int32, y.shape, 1)
    first = (lane % (2 * half)) < half
    sw = jnp.where(first, pltpu.roll(y, LANES - half, 1), pltpu.roll(y, half, 1))
    return y * cos + sw * sin


def _norm_rope(r, ms, gain, cos, sin, half):
    outs = []
    for c in range(0, r.shape[1], LANES):
        y = r[:, c:c + LANES] * lax.rsqrt(ms[:, c:c + LANES] + EPS) * gain
        if cos is not None:
            y = _rope_slab(y, cos, sin, half)
        outs.append(y)
    return outs[0] if len(outs) == 1 else jnp.concatenate(outs, axis=1)


def _proj_kernel(with_q, x_ref, mod_ref, vec_ref, gains_ref, rope_ref, mavg_ref,
                 w_ref, wuq_ref, wukv_ref, *refs):
    if with_q:
        (qa_ref, qb_ref, qc_ref, qd_ref, g_ref) = refs[-5:]
        kv_refs = refs[-13:-5]
        kvo, go = Q_COLS, Q_COLS + KV_PAD_COLS
    else:
        kv_refs = refs[-8:]
        kvo = 0
    ka_ref, va_ref, kb_ref, vb_ref, kc_ref, vc_ref, kd_ref, vd_ref = kv_refs

    m = mod_ref[0]
    hb = _adaln(x_ref[...], vec_ref[0:1, :], m[0:1], m[1:2]).astype(BF16)

    def mm(lo, hi):
        return _dot(hb, w_ref[:, lo:hi])

    m64 = mavg_ref[0]
    md = mavg_ref[1]
    c64, s64, cd, sd = rope_ref[0], rope_ref[1], rope_ref[2], rope_ref[3]
    gains = gains_ref[...]

    def gain(i):
        return gains[i:i + 1, :]

    def gates(n):
        gl = mm(go + n * D_MODEL, go + (n + 1) * D_MODEL)
        g_ref[:, n * D_MODEL:(n + 1) * D_MODEL] = _sigmoid(gl).astype(BF16)

    def full_rms(r, g):
        return (r * lax.rsqrt(jnp.mean(r * r, axis=-1, keepdims=True) + EPS) * g).astype(BF16)

    if with_q:
        r_qa, r_qb, r_qc, cq = mm(0, 512), mm(512, 1024), mm(1024, 1536), mm(1536, 2048)
    kvab = mm(kvo, kvo + 512)
    r_kc = mm(kvo + 512, kvo + 1024)
    r_vc = mm(kvo + 1024, kvo + 1536)
    ckv = mm(kvo + 1536, kvo + 1792)
    r_kr = mm(kvo + 1792, kvo + 1920)
    if with_q:
        gates(0)
        ms_qa, ms_qb, ms_qc = (_group_mean_sq(r, m64) for r in (r_qa, r_qb, r_qc))
        qd = _dot(full_rms(cq, vec_ref[1:2, 0:Q_LORA]), wuq_ref[...])
    r_ka, r_kb = kvab[:, 0:128], kvab[:, 256:384]
    ms_ka, ms_kb, ms_kc = (_group_mean_sq(r, m64) for r in (r_ka, r_kb, r_kc))
    ms_kr = _group_mean_sq(r_kr, md)
    kvd = _dot(full_rms(ckv, vec_ref[1:2, Q_LORA:Q_LORA + KV_LORA]), wukv_ref[...])
    if with_q:
        gates(1)
        qa_ref[...] = _norm_rope(r_qa, ms_qa, gain(0), c64, s64, 32).astype(BF16)
        qb_ref[...] = _norm_rope(r_qb, ms_qb, gain(1), c64, s64, 32).astype(BF16)
        qc_ref[...] = _norm_rope(r_qc, ms_qc, gain(2), c64, s64, 32).astype(BF16)
        ms_qd = _group_mean_sq(qd, md)
    r_kn = kvd[:, 0:D_HEADS * LANES]
    ms_kn = _group_mean_sq(r_kn, md)
    if with_q:
        gates(2)
    ka_ref[...] = _norm_rope(r_ka, ms_ka, gain(3), c64, s64, 32).astype(BF16)
    va_ref[...] = kvab[:, 128:256].astype(BF16)
    kb_ref[...] = _norm_rope(r_kb, ms_kb, gain(4), c64, s64, 32).astype(BF16)
    vb_ref[...] = kvab[:, 384:512].astype(BF16)
    kc_ref[...] = _norm_rope(r_kc, ms_kc, gain(5), c64, s64, 32).astype(BF16)
    vc_ref[...] = r_vc.astype(BF16)
    if with_q:
        qd_ref[...] = _norm_rope(qd, ms_qd, gain(6), cd, sd, 16).astype(BF16)
        gates(3)
    kr = _norm_rope(r_kr, ms_kr, gain(8), cd, sd, 16)
    kn = _norm_rope(r_kn, ms_kn, gain(7), None, None, 0)
    kd_ref[...] = (kn + jnp.concatenate([kr] * D_HEADS, axis=1)).astype(BF16)
    vd_ref[...] = kvd[:, D_HEADS * LANES:].astype(BF16)


_KV_WIDTHS = (128, 128, 128, 128, 512, 512, D_HEADS * LANES, D_HEADS * V_HEAD)


def _project(x, mod, vec, gains, rope, mavg, w, wuq, wukv, *, seq, tm, with_q):
    n = x.shape[0]
    tps = seq // tm
    grid = (n // tm,)

    def tok(i):
        return (i, 0)

    in_specs = [
        pl.BlockSpec((tm, D_MODEL), tok),
        pl.BlockSpec((1, 6, D_MODEL), lambda i: (i // tps, 0, 0)),
        _resident(vec.shape),
        _resident(gains.shape),
        pl.BlockSpec((4, tm, LANES), lambda i: (0, i % tps, 0)),
        _resident(mavg.shape),
        _resident(w.shape),
        _resident(wuq.shape),
        _resident(wukv.shape),
    ]
    args = [x, mod, vec, gains, rope, mavg, w, wuq, wukv]
    out_shapes = [jax.ShapeDtypeStruct((n, wd), BF16) for wd in _KV_WIDTHS]
    out_specs = [pl.BlockSpec((tm, wd), tok) for wd in _KV_WIDTHS]
    if with_q:
        for wd in (512, 512, 512, D_HEADS * LANES, GATE_COLS):
            out_shapes.append(jax.ShapeDtypeStruct((n, wd), BF16))
            out_specs.append(pl.BlockSpec((tm, wd), tok))
    return pl.pallas_call(
        functools.partial(_proj_kernel, with_q),
        out_shape=out_shapes, grid=grid, in_specs=in_specs, out_specs=out_specs,
        compiler_params=_cparams(("arbitrary",)),
        name="project_q" if with_q else "project_kv",
    )(*args)


LOGIT_SAFE = 40.0


def _scores(q, kv, mask=None):
    scores = [_dot_nt(q, k) for k, _ in kv]
    if mask is not None:
        scores[0] = jnp.where(mask, scores[0], NEG_INF)
    return scores


def _attend(shift, scores, kv, extra_logit=None):
    if shift:
        m = functools.reduce(jnp.maximum, [jnp.max(s, axis=-1, keepdims=True) for s in scores])
        if extra_logit is not None:
            m = jnp.maximum(m, extra_logit)
            extra_logit = extra_logit - m
        scores = [s - m for s in scores]
    probs = [jnp.exp2(s) for s in scores]
    l = functools.reduce(jnp.add, [jnp.sum(p, axis=-1, keepdims=True) for p in probs])
    if extra_logit is not None:
        l = l + jnp.exp2(extra_logit)
    o = functools.reduce(jnp.add, [_dot(p.astype(BF16), v) for p, (_, v) in zip(probs, kv)])
    return o / l


def _run_skewed(units):
    pending = None
    for start, finish in units:
        state = start()
        if pending is not None:
            pending[0](pending[1])
        pending = (finish, state)
    pending[0](pending[1])


def _guarded(body, idx, safe_ref, *refs):
    @pl.when(safe_ref[idx] != 0)
    def _():
        body(False, *refs)

    @pl.when(safe_ref[idx] == 0)
    def _():
        body(True, *refs)


def _split_heads(qs):
    lane = lax.broadcasted_iota(jnp.int32, qs.shape, 1)
    zero = jnp.zeros_like(qs)
    return jnp.concatenate([jnp.where(lane < HD, qs, zero), jnp.where(lane >= HD, qs, zero)], axis=0)


def _merge_halves(o, tq):
    lane = lax.broadcasted_iota(jnp.int32, (tq, LANES), 1)
    return jnp.where(lane < HD, o[:tq], o[tq:])


def _kv_cols(kv_refs, kc, vc):
    return [(k_ref[:, kc * LANES:(kc + 1) * LANES], v_ref[:, vc * LANES:(vc + 1) * LANES])
            for k_ref, v_ref in kv_refs]


def _one_group(groups):
    return [(jnp.concatenate([k for k, _ in groups], axis=0), jnp.concatenate([v for _, v in groups], axis=0))]


def _mix_b(shift, q_ref, kv_refs, o_ref):
    tq = q_ref.shape[0]
    kv = _kv_cols(kv_refs, 0, 0)

    def start(j):
        return _scores(_split_heads(q_ref[:, j * LANES:(j + 1) * LANES]), kv)

    def finish(j, scores):
        o = _attend(shift, scores, kv)
        o_ref[:, j * LANES:(j + 1) * LANES] = _merge_halves(o, tq).astype(BF16)

    return [(functools.partial(start, j), functools.partial(finish, j)) for j in range(4)]


def _mix_a(shift, sink_ref, q_ref, kv_refs, o_ref):
    tq = q_ref.shape[0]
    if len(kv_refs) == 2:
        (kl_ref, vl_ref), (kc_ref, vc_ref) = kv_refs
        span = tq + 2 * WINDOW
        q0 = pl.program_id(1) * tq
        start = pl.multiple_of(jnp.clip(q0 - WINDOW, 0, kl_ref.shape[0] - span), BLOCK)
        kv = [(kl_ref[pl.ds(start, span), :], vl_ref[pl.ds(start, span), :]), (kc_ref[...], vc_ref[...])]
        row = lax.broadcasted_iota(jnp.int32, (2 * tq, span), 0)
        col = lax.broadcasted_iota(jnp.int32, (2 * tq, span), 1)
        qpos = q0 + jnp.where(row >= tq, row - tq, row)
        dist = (start + col) - qpos
        valid = (dist >= -WINDOW) & (dist <= WINDOW)
    else:
        kv = _kv_cols(kv_refs, 0, 0)
        valid = None
    rows = lax.broadcasted_iota(jnp.int32, (2 * tq, 1), 0)

    def start(j):
        return _scores(_split_heads(q_ref[:, j * LANES:(j + 1) * LANES]), kv, valid)

    def finish(j, scores):
        sink = jnp.where(rows < tq, sink_ref[2 * j], sink_ref[2 * j + 1]) * LOG2E
        o = _attend(shift, scores, kv, extra_logit=sink)
        o_ref[:, j * LANES:(j + 1) * LANES] = _merge_halves(o, tq).astype(BF16)

    return [(functools.partial(start, j), functools.partial(finish, j)) for j in range(4)]


def _mix_c(lam_init, shift, lq1_ref, lk1_ref, lq2_ref, lk2_ref, subln_ref, q_ref, kv_refs, o_ref):
    tq = q_ref.shape[0]
    lam = (jnp.exp(jnp.sum(lq1_ref[...] * lk1_ref[...], axis=-1, keepdims=True))
           - jnp.exp(jnp.sum(lq2_ref[...] * lk2_ref[...], axis=-1, keepdims=True)) + lam_init)

    def start(c):
        kv = _one_group(_kv_cols(kv_refs, c, c))
        return kv, _scores(_split_heads(q_ref[:, c * LANES:(c + 1) * LANES]), kv)

    def finish(c, state):
        kv, scores = state
        o = _attend(shift, scores, kv)
        oc = o[:tq] - lam * o[tq:]
        oc = oc * lax.rsqrt(jnp.mean(oc * oc, axis=-1, keepdims=True) + EPS) * subln_ref[...]
        o_ref[:, c * LANES:(c + 1) * LANES] = (oc * (1.0 - lam_init)).astype(BF16)

    _run_skewed([(functools.partial(start, c), functools.partial(finish, c)) for c in range(C_HEADS)])


def _mix_d(shift, q_ref, kv_refs, o_ref):
    tq = q_ref.shape[0]

    def start(h):
        return _scores(q_ref[:, h * LANES:(h + 1) * LANES], _kv_cols(kv_refs, h, h // 2))

    done = {}

    def finish(h, scores):
        done[h] = _attend(shift, scores, _kv_cols(kv_refs, h, h // 2))
        if h % 2 == 1:
            o = jnp.concatenate([done.pop(h - 1), done.pop(h)], axis=0)
            o_ref[:, (h // 2) * LANES:(h // 2 + 1) * LANES] = _merge_halves(o, tq).astype(BF16)

    _run_skewed([(functools.partial(start, h), functools.partial(finish, h)) for h in range(D_HEADS)])


def _mix_ab(shift, sink_ref, qa_ref, kva_refs, oa_ref, qb_ref, kvb_refs, ob_ref):
    units = []
    for unit_b, unit_a in zip(_mix_b(shift, qb_ref, kvb_refs, ob_ref),
                              _mix_a(shift, sink_ref, qa_ref, kva_refs, oa_ref)):
        units += [unit_b, unit_a]
    _run_skewed(units)


ATTN_TQ = 256


def _attention(body, safe_idx, safe, extra, extra_specs, units, *, seq_q, name):
    n = units[0][0].shape[0]
    nb = n // seq_q
    tq = ATTN_TQ
    nq = seq_q // tq
    tok = lambda b, i: (b * nq + i, 0)
    per_batch = lambda a: pl.BlockSpec((a.shape[0] // nb, a.shape[1]), lambda b, i: (b, 0))
    args, in_specs, counts = [], [], []
    for q, kv in units:
        args.append(q)
        in_specs.append(pl.BlockSpec((tq, q.shape[1]), tok))
        for pair in kv:
            args.extend(pair)
            in_specs.extend(per_batch(a) for a in pair)
        counts.append(len(kv))
    n_extra = len(extra)

    def kernel(shift, *refs):
        ins, outs = refs[n_extra:len(refs) - len(units)], refs[len(refs) - len(units):]
        packed, pos = [], 0
        for u, nkv in enumerate(counts):
            kv_refs = [(ins[pos + 1 + 2 * g], ins[pos + 2 + 2 * g]) for g in range(nkv)]
            packed.extend([ins[pos], kv_refs, outs[u]])
            pos += 1 + 2 * nkv
        body(shift, *refs[:n_extra], *packed)

    out = pl.pallas_call(
        functools.partial(_guarded, kernel, safe_idx),
        out_shape=[jax.ShapeDtypeStruct((n, BRANCH_W), BF16)] * len(units),
        grid=(nb, nq),
        in_specs=[pl.BlockSpec(memory_space=pltpu.SMEM)] + list(extra_specs) + in_specs,
        out_specs=[pl.BlockSpec((tq, BRANCH_W), tok)] * len(units),
        compiler_params=_cparams(("arbitrary", "arbitrary")),
        name=name,
    )(safe, *extra, *args)
    return out


def _mixers(qs, kv_groups, lp, *, seq_q, tag):
    qa, qb, qc, qd = qs
    pairs = lambda j: [(g[2 * j], g[2 * j + 1]) for g in kv_groups]
    safe = lp["safe"]
    row = lambda w: pl.BlockSpec((1, w), lambda b, i: (0, 0))
    oa, ob = _attention(_mix_ab, 0, safe, [lp["sink"]], [pl.BlockSpec(memory_space=pltpu.SMEM)],
                        [(qa, pairs(0)), (qb, pairs(1))], seq_q=seq_q, name="attn_ab_" + tag)
    oc, = _attention(functools.partial(_mix_c, lp["lam_init"]), 1, safe,
                     [lp["lq1"], lp["lk1"], lp["lq2"], lp["lk2"], lp["subln"]],
                     [row(HD)] * 4 + [row(2 * HD)], [(qc, pairs(2))], seq_q=seq_q, name="attn_c_" + tag)
    od, = _attention(_mix_d, 2, safe, [], [], [(qd, pairs(3))], seq_q=seq_q, name="attn_d_" + tag)
    return oa, ob, oc, od


def _merge_kernel(moe, oa_ref, ob_ref, oc_ref, od_ref, g_ref, x_ref, mod_ref, norm_ref,
                  wbr_ref, wout_ref, *refs):
    if moe:
        wr_ref, xo_ref, h_ref, lg_ref = refs
    else:
        xo_ref, h_ref = refs
    m = mod_ref[0]
    y = None
    for n, o_ref in enumerate((oa_ref, ob_ref, oc_ref, od_ref)):
        yn = g_ref[:, n * D_MODEL:(n + 1) * D_MODEL].astype(F32) * _dot(o_ref[...], wbr_ref[n])
        y = yn if y is None else y + yn
    z = _dot(y.astype(BF16), wout_ref[...])
    xn = x_ref[...] + m[2:3] * z
    xo_ref[...] = xn
    h = _adaln(xn, norm_ref[...], m[3:4], m[4:5])
    if moe:
        h_ref[...] = h
        h_hi = h.astype(BF16)
        h_lo = (h - h_hi.astype(F32)).astype(BF16)
        lg_ref[...] = (_dot_nt(wr_ref[0], h_hi) + _dot_nt(wr_ref[1], h_hi)) + _dot_nt(wr_ref[0], h_lo)
    else:
        h_ref[...] = h.astype(BF16)


def _merge(branches, g, x, mod, norm, wbr, wout, router_t, *, seq):
    n = x.shape[0]
    tm = min(MERGE_TM, seq)
    tps = seq // tm
    moe = router_t is not None
    tok = lambda i: (i, 0)
    in_specs = [pl.BlockSpec((tm, BRANCH_W), tok)] * 4 + [
        pl.BlockSpec((tm, GATE_COLS), tok),
        pl.BlockSpec((tm, D_MODEL), tok),
        pl.BlockSpec((1, 6, D_MODEL), lambda i: (i // tps, 0, 0)),
        _resident(norm.shape), _resident(wbr.shape), _resident(wout.shape)]
    args = list(branches) + [g, x, mod, norm, wbr, wout]
    out_shape = [jax.ShapeDtypeStruct((n, D_MODEL), F32),
                 jax.ShapeDtypeStruct((n, D_MODEL), F32 if moe else BF16)]
    out_specs = [pl.BlockSpec((tm, D_MODEL), tok), pl.BlockSpec((tm, D_MODEL), tok)]
    if moe:
        in_specs.append(_resident(router_t.shape))
        args.append(router_t)
        out_shape.append(jax.ShapeDtypeStruct((N_EXPERTS, n), F32))
        out_specs.append(pl.BlockSpec((N_EXPERTS, tm), lambda i: (0, i)))
    return pl.pallas_call(
        functools.partial(_merge_kernel, moe),
        out_shape=out_shape, grid=(n // tm,), in_specs=in_specs, out_specs=out_specs,
        compiler_params=_cparams(("arbitrary",)),
        name="merge_moe" if moe else "merge_dense",
    )(*args)


def _ffn_kernel(h_ref, x_ref, mod_ref, wg_ref, wu_ref, wd_ref, o_ref, hid_ref):
    h = h_ref[...]
    for c in range(0, D_FF, FFN_TF):
        a = _dot(h, wg_ref[:, c:c + FFN_TF])
        u = _dot(h, wu_ref[:, c:c + FFN_TF])
        hid_ref[:, c:c + FFN_TF] = (a * _sigmoid(a) * u).astype(BF16)
    m = mod_ref[0]
    o_ref[...] = x_ref[...] + m[5:6] * _dot(hid_ref[...], wd_ref[...])


def _dense_ffn(h, x, mod, wg, wu, wd, *, seq):
    n = x.shape[0]
    tm = min(FFN_TM, seq)
    tps = seq // tm
    tok = lambda i: (i, 0)
    return pl.pallas_call(
        _ffn_kernel,
        out_shape=jax.ShapeDtypeStruct((n, D_MODEL), F32),
        grid=(n // tm,),
        in_specs=[pl.BlockSpec((tm, D_MODEL), tok), pl.BlockSpec((tm, D_MODEL), tok),
                  pl.BlockSpec((1, 6, D_MODEL), lambda i: (i // tps, 0, 0)),
                  _resident(wg.shape), _resident(wu.shape), _resident(wd.shape)],
        out_specs=pl.BlockSpec((tm, D_MODEL), tok),
        scratch_shapes=[pltpu.VMEM((tm, D_FF), BF16)],
        compiler_params=_cparams(("arbitrary",)),
        name="dense_ffn",
    )(h, x, mod, wg, wu, wd)


ROUTE_CHUNK = 512
DMA_UNROLL = 8


def _route_kernel(n, lg_ref, tri_ref, dest_ref, wts_ref, meta_ref, mem_ref, pos_ref):
    lg = lg_ref[...]
    eidx = lax.broadcasted_iota(jnp.int32, lg.shape, 0).astype(F32)
    none = float(N_EXPERTS)
    m1 = jnp.max(lg, axis=0, keepdims=True)
    i1 = jnp.min(jnp.where(lg == m1, eidx, none), axis=0, keepdims=True)
    lg2 = jnp.where(eidx == i1, -jnp.inf, lg)
    m2 = jnp.max(lg2, axis=0, keepdims=True)
    i2 = jnp.min(jnp.where(lg2 == m2, eidx, none), axis=0, keepdims=True)
    e = jnp.exp(m2 - m1)
    w1 = 1.0 / (1.0 + e)
    wts_ref[0:1, :] = w1
    wts_ref[1:2, :] = e * w1
    mem_ref[...] = jnp.where(eidx == i1, 1.0, 0.0) + jnp.where(eidx == i2, 1.0, 0.0)

    carry = jnp.zeros((N_EXPERTS, 1), F32)
    tri = tri_ref[...]
    for c in range(0, n, ROUTE_CHUNK):
        mc = mem_ref[:, c:c + ROUTE_CHUNK]
        pos_ref[:, c:c + ROUTE_CHUNK] = carry + _dot(mc.astype(BF16), tri)
        carry = carry + jnp.sum(mc, axis=1, keepdims=True)
    padded = jnp.floor((carry + (MOE_TM - 1.0)) * (1.0 / MOE_TM)) * MOE_TM
    esub = lax.broadcasted_iota(jnp.int32, (N_EXPERTS, 1), 0)
    pstart = jnp.zeros((N_EXPERTS, 1), F32)
    for k in range(N_EXPERTS - 1):
        pstart = pstart + jnp.where(esub > k, padded[k:k + 1, :], 0.0)
    pend = pstart + padded
    slot = pstart + pos_ref[...]
    dest_ref[0:1, :] = jnp.sum(jnp.where(eidx == i1, slot, 0.0), axis=0, keepdims=True).astype(jnp.int32)
    dest_ref[1:2, :] = jnp.sum(jnp.where(eidx == i2, slot, 0.0), axis=0, keepdims=True).astype(jnp.int32)

    nblk = pend[N_EXPERTS - 1:N_EXPERTS, :] * (1.0 / MOE_TM)
    brow = jnp.minimum(lax.broadcasted_iota(jnp.int32, (1, LANES), 1).astype(F32), nblk - 1.0) * MOE_TM
    be = jnp.zeros((1, LANES), F32)
    for k in range(N_EXPERTS):
        be = be + jnp.where(pend[k:k + 1, :] <= brow, 1.0, 0.0)
    meta_ref[...] = jnp.zeros(meta_ref.shape, jnp.int32)
    meta_ref[0:1, :] = jnp.minimum(be, N_EXPERTS - 1.0).astype(jnp.int32)
    meta_ref[1:2, :] = jnp.broadcast_to(nblk, (1, LANES)).astype(jnp.int32)


def _route(logits_t):
    n = logits_t.shape[1]
    nblk_max = n * TOP_K // MOE_TM + N_EXPERTS
    assert nblk_max <= LANES and n % ROUTE_CHUNK == 0
    tri = jnp.asarray(np.triu(np.ones((ROUTE_CHUNK, ROUTE_CHUNK)), 1), BF16)
    whole = lambda shape: pl.BlockSpec(shape, lambda: (0,) * len(shape))
    dest, wts, meta = pl.pallas_call(
        functools.partial(_route_kernel, n),
        out_shape=[jax.ShapeDtypeStruct((TOP_K, n), jnp.int32),
                   jax.ShapeDtypeStruct((TOP_K, n), F32),
                   jax.ShapeDtypeStruct((8, LANES), jnp.int32)],
        in_specs=[whole(logits_t.shape), whole(tri.shape)],
        out_specs=[whole((TOP_K, n)), whole((TOP_K, n)), whole((8, LANES))],
        scratch_shapes=[pltpu.VMEM((N_EXPERTS, n), F32), pltpu.VMEM((N_EXPERTS, n), F32)],
        compiler_params=pltpu.CompilerParams(vmem_limit_bytes=VMEM_LIMIT),
        name="moe_route",
    )(logits_t, tri)
    return dest, wts, meta[0, :nblk_max], meta[1, :1], nblk_max


def _dispatch_kernel(tg, dest_ref, h_ref, xin_hbm, xbuf_hbm, sem):
    del xin_hbm

    def copies(r):
        src = h_ref.at[pl.ds(r, 1)]
        return (pltpu.make_async_copy(src, xbuf_hbm.at[pl.ds(dest_ref[0, 0, r], 1)], sem),
                pltpu.make_async_copy(src, xbuf_hbm.at[pl.ds(dest_ref[0, 0, tg + r], 1)], sem))

    def start(r, carry):
        for k, cp in enumerate(copies(r)):
            cp.start(priority=k)
        return carry

    def wait(r, carry):
        for cp in copies(r):
            cp.wait()
        return carry

    lax.fori_loop(0, tg, start, 0, unroll=DMA_UNROLL)
    lax.fori_loop(0, tg, wait, 0, unroll=DMA_UNROLL)


def _tile_dest(dest, t):
    n = dest.shape[1]
    return jnp.transpose(dest.reshape(TOP_K, n // t, t), (1, 0, 2)).reshape(n // t, 1, TOP_K * t)


def _dispatch(h, dest, nrows):
    n = h.shape[0]
    tg = min(1024, n)
    nt = n // tg
    dest_t = _tile_dest(dest, tg)
    xbuf0 = jnp.zeros((nrows, D_MODEL), F32)
    return pl.pallas_call(
        functools.partial(_dispatch_kernel, tg),
        out_shape=jax.ShapeDtypeStruct((nrows, D_MODEL), F32),
        grid=(nt,),
        in_specs=[pl.BlockSpec((1, 1, TOP_K * tg), lambda i: (i, 0, 0), memory_space=pltpu.SMEM),
                  pl.BlockSpec((tg, D_MODEL), lambda i: (i, 0)), pl.BlockSpec(memory_space=pl.ANY)],
        out_specs=pl.BlockSpec(memory_space=pl.ANY),
        scratch_shapes=[pltpu.SemaphoreType.DMA(())],
        input_output_aliases={2: 0},
        compiler_params=_cparams(("arbitrary",)),
        name="moe_dispatch",
    )(dest_t, h, xbuf0)


def _moe_kernel(nf, blk_e_ref, nblk_ref, x_ref, w1_ref, w3_ref, w2_ref, o_ref, xb_ref, hid_ref):
    del blk_e_ref
    i = pl.program_id(0)
    j = pl.program_id(1)

    @pl.when(i < nblk_ref[0])
    def _():
        @pl.when(j == 0)
        def _():
            xb_ref[...] = x_ref[...].astype(BF16)

        xb = xb_ref[...]
        for c in range(0, MOE_TF, FFN_TF):
            a = _dot(xb, w1_ref[0, :, c:c + FFN_TF])
            u = _dot(xb, w3_ref[0, :, c:c + FFN_TF])
            hid_ref[:, c:c + FFN_TF] = (a * _sigmoid(a) * u).astype(BF16)
        y = _dot(hid_ref[...], w2_ref[0])

        @pl.when(j == 0)
        def _():
            o_ref[...] = y

        @pl.when(j > 0)
        def _():
            o_ref[...] += y

    @pl.when((i >= nblk_ref[0]) & (j == nf - 1))
    def _():
        o_ref[...] = jnp.zeros_like(o_ref)


def _moe_ffn(xbuf, blk_e, nblk, nblk_max, w1, w3, w2):
    nf = EXPERT_FF // MOE_TF

    def rows(i, j, be, nb):
        return (i, 0)

    def jj(i, j, nb):
        return jnp.where(i < nb[0], j, nf - 1)

    grid_spec = pltpu.PrefetchScalarGridSpec(
        num_scalar_prefetch=2, grid=(nblk_max, nf),
        in_specs=[pl.BlockSpec((MOE_TM, D_MODEL), rows),
                  pl.BlockSpec((1, D_MODEL, MOE_TF), lambda i, j, be, nb: (be[i], 0, jj(i, j, nb))),
                  pl.BlockSpec((1, D_MODEL, MOE_TF), lambda i, j, be, nb: (be[i], 0, jj(i, j, nb))),
                  pl.BlockSpec((1, MOE_TF, D_MODEL), lambda i, j, be, nb: (be[i], jj(i, j, nb), 0))],
        out_specs=pl.BlockSpec((MOE_TM, D_MODEL), rows),
        scratch_shapes=[pltpu.VMEM((MOE_TM, D_MODEL), BF16), pltpu.VMEM((MOE_TM, MOE_TF), BF16)])
    return pl.pallas_call(
        functools.partial(_moe_kernel, nf),
        out_shape=jax.ShapeDtypeStruct(xbuf.shape, F32),
        grid_spec=grid_spec,
        compiler_params=_cparams(("arbitrary", "arbitrary")),
        name="moe_experts",
    )(blk_e, nblk, xbuf, w1, w3, w2)


def _combine_kernel(tc, nt, dest_ref, dest_next_ref, wts_ref, x_ref, mod_ref, y_hbm, o_ref, yg_ref, sems):
    i = pl.program_id(0)
    slot = lax.rem(i, 2)

    def copies(d_ref, s, r):
        return (pltpu.make_async_copy(y_hbm.at[pl.ds(d_ref[0, 0, r], 1)],
                                      yg_ref.at[s, 0, pl.ds(r, 1)], sems.at[s]),
                pltpu.make_async_copy(y_hbm.at[pl.ds(d_ref[0, 0, tc + r], 1)],
                                      yg_ref.at[s, 1, pl.ds(r, 1)], sems.at[s]))

    def request(d_ref, s):
        def start(r, carry):
            for k, cp in enumerate(copies(d_ref, s, r)):
                cp.start(priority=k)
            return carry
        lax.fori_loop(0, tc, start, 0, unroll=DMA_UNROLL)

    @pl.when(i == 0)
    def _():
        request(dest_ref, slot)

    @pl.when(i + 1 < nt)
    def _():
        request(dest_next_ref, 1 - slot)

    def wait(r, carry):
        for cp in copies(dest_ref, slot, r):
            cp.wait()
        return carry

    lax.fori_loop(0, tc, wait, 0, unroll=DMA_UNROLL)
    m = mod_ref[0]
    w = wts_ref[...]
    y = w[:, 0:1] * yg_ref[slot, 0] + w[:, 1:2] * yg_ref[slot, 1]
    o_ref[...] = x_ref[...] + m[5:6] * y


def _combine(ybuf, dest, wts, x, mod, *, seq):
    n = x.shape[0]
    tc = 256
    nt = n // tc
    tps = seq // tc
    dest_t = _tile_dest(dest, tc)
    wts = wts.T
    tok = lambda i: (i, 0)
    slots = lambda step: pl.BlockSpec((1, 1, TOP_K * tc), lambda i: (jnp.minimum(i + step, nt - 1), 0, 0),
                                      memory_space=pltpu.SMEM)
    return pl.pallas_call(
        functools.partial(_combine_kernel, tc, nt),
        out_shape=jax.ShapeDtypeStruct((n, D_MODEL), F32),
        grid=(nt,),
        in_specs=[slots(0), slots(1),
                  pl.BlockSpec((tc, TOP_K), tok),
                  pl.BlockSpec((tc, D_MODEL), tok),
                  pl.BlockSpec((1, 6, D_MODEL), lambda i: (i // tps, 0, 0)),
                  pl.BlockSpec(memory_space=pl.ANY)],
        out_specs=pl.BlockSpec((tc, D_MODEL), tok),
        scratch_shapes=[pltpu.VMEM((2, TOP_K, tc, D_MODEL), F32), pltpu.SemaphoreType.DMA((2,))],
        compiler_params=_cparams(("arbitrary",)),
        name="moe_combine",
    )(dest_t, dest_t, wts, x, mod, ybuf)


def _head_index():
    return np.concatenate([np.arange(h * HD, (h + 1) * HD) for h in HEAD_PERM])


def _prep_w_in(w):
    hp = _head_index()
    kv_lo = Q_COLS
    kr_lo = kv_lo + KV_PAD_COLS - LANES
    parts = [w[:, 0:512][:, hp], w[:, 512:1024][:, hp], w[:, 1024:Q_COLS],
             w[:, kv_lo:kr_lo],
             jnp.pad(w[:, kr_lo:kr_lo + QK_ROPE], ((0, 0), (QK_NOPE, LANES - QK_NOPE - QK_ROPE))),
             w[:, kr_lo + QK_ROPE:]]
    return jnp.concatenate(parts, axis=1).astype(BF16)


def _prep_layer(l, p):
    zeros = lambda k: jnp.zeros((k,), F32)
    sc = LOG2E / math.sqrt(HD)
    sc_d = LOG2E / math.sqrt(QK_NOPE + QK_ROPE)
    wuq = p["d_w_uq"][l].reshape(Q_LORA, D_HEADS, QK_NOPE + QK_ROPE)
    wuq = jnp.pad(wuq, ((0, 0), (0, 0), (0, LANES - QK_NOPE - QK_ROPE))).reshape(Q_LORA, D_HEADS * LANES)
    wukv = p["d_w_ukv"][l].reshape(KV_LORA, D_HEADS, QK_NOPE + V_HEAD)
    wuk = jnp.pad(wukv[:, :, :QK_NOPE], ((0, 0), (0, 0), (0, LANES - QK_NOPE))).reshape(KV_LORA, D_HEADS * LANES)
    wuv = wukv[:, :, QK_NOPE:].reshape(KV_LORA, D_HEADS * V_HEAD)
    gains = jnp.stack([
        jnp.tile(p["a_qn"][l], 2) * sc, jnp.tile(p["b_qn"][l], 2) * sc, jnp.tile(p["c_qn"][l], 2) * sc,
        jnp.tile(p["a_kn"][l], 2), jnp.tile(p["b_kn"][l], 2), jnp.tile(p["c_kn"][l], 2),
        jnp.concatenate([p["d_qn_nope"][l], p["d_qn_rope"][l], zeros(32)]) * sc_d,
        jnp.concatenate([p["d_kn_nope"][l], zeros(64)]),
        jnp.concatenate([zeros(64), p["d_kn_rope"][l], zeros(32)]),
    ] + [zeros(LANES)] * 7).astype(F32)
    vec = jnp.stack([p["mix_norm"][l],
                     jnp.concatenate([p["d_q_norm"][l], p["d_kv_norm"][l], zeros(D_MODEL - Q_LORA - KV_LORA)])]
                    + [zeros(D_MODEL)] * 6).astype(F32)
    wbr = p["w_br"][l]
    hp = _head_index()
    wbr = jnp.stack([wbr[0][hp], wbr[1][hp], wbr[2], wbr[3]]).astype(BF16)

    amax = lambda v: jnp.max(jnp.abs(v))
    bound64 = lambda gq, gk: 1.02 * HD * amax(gq) * amax(gk)
    nq_d = jnp.sqrt(QK_NOPE * amax(gains[6, :QK_NOPE]) ** 2 + QK_ROPE * amax(gains[6, QK_NOPE:]) ** 2)
    nk_d = jnp.sqrt(QK_NOPE * amax(gains[7]) ** 2 + QK_ROPE * amax(gains[8]) ** 2)
    bounds = jnp.stack([
        jnp.maximum(jnp.maximum(bound64(gains[0], gains[3]), bound64(gains[1], gains[4])),
                    LOG2E * amax(p["a_sink"][l])),
        bound64(gains[2], gains[5]),
        1.02 * nq_d * nk_d])
    safe = (bounds <= LOGIT_SAFE).astype(jnp.int32)
    return {
        "safe": safe,
        "w_in": _prep_w_in(p["w_in"][l]),
        "wuq": wuq.astype(BF16),
        "wukv": jnp.concatenate([wuk, wuv], axis=1).astype(BF16),
        "gains": gains, "vec": vec, "wbr": wbr, "wout": p["w_out"][l].astype(BF16),
        "ffn_norm": p["ffn_norm"][l].reshape(1, D_MODEL),
        "sink": p["a_sink"][l][np.asarray(HEAD_PERM)].astype(F32),
        "lq1": p["c_lq1"][l].reshape(1, HD), "lk1": p["c_lk1"][l].reshape(1, HD),
        "lq2": p["c_lq2"][l].reshape(1, HD), "lk2": p["c_lk2"][l].reshape(1, HD),
        "subln": p["c_subln"][l].reshape(1, 2 * HD),
        "lam_init": 0.8 - 0.6 * math.exp(-0.3 * l),
    }


def _group_avg_mats():
    m64 = np.kron(np.eye(4), np.full((HD, HD), 1.0 / HD))
    one = np.zeros((LANES, LANES))
    one[:QK_NOPE, :QK_NOPE] = 1.0 / QK_NOPE
    one[QK_NOPE:QK_NOPE + QK_ROPE, QK_NOPE:QK_NOPE + QK_ROPE] = 1.0 / QK_ROPE
    md = np.kron(np.eye(2), one)
    return jnp.asarray(np.stack([m64, md]), BF16)


def _rope_tables(seq, rotary):
    if not rotary:
        one = jnp.ones((seq, LANES), F32)
        zero = jnp.zeros((seq, LANES), F32)
        return jnp.stack([one, zero, one, zero])
    t = jnp.arange(seq)

    def angles(rot_dim):
        nfreq = rot_dim // 4
        inv = jnp.power(ROPE_THETA, -jnp.arange(nfreq, dtype=F32) / nfreq)
        return jnp.concatenate([(t // GRID_W).astype(F32)[:, None] * inv,
                                (t % GRID_W).astype(F32)[:, None] * inv], axis=-1)

    ah = angles(HD)
    c64 = jnp.tile(jnp.cos(ah), (1, 4))
    s64 = jnp.tile(jnp.concatenate([-jnp.sin(ah), jnp.sin(ah)], axis=1), (1, 2))
    ar = angles(QK_ROPE)
    one = jnp.ones((seq, 1), F32)
    cd = jnp.concatenate([one * jnp.ones((1, QK_NOPE), F32), jnp.cos(ar), jnp.cos(ar),
                          one * jnp.ones((1, LANES - QK_NOPE - QK_ROPE), F32)], axis=1)
    sd = jnp.concatenate([jnp.zeros((seq, QK_NOPE), F32), -jnp.sin(ar), jnp.sin(ar),
                          jnp.zeros((seq, LANES - QK_NOPE - QK_ROPE), F32)], axis=1)
    return jnp.stack([c64, s64, cd, sd])


def kernel(x, c, ctx, c_ctx, w_mod, b_mod, mix_norm, ffn_norm, w_in, a_qn, a_kn, a_sink, b_qn, b_kn, c_qn, c_kn, c_lq1, c_lk1, c_lq2, c_lk2, c_subln, d_q_norm, d_kv_norm, d_w_uq, d_w_ukv, d_qn_nope, d_kn_nope, d_qn_rope, d_kn_rope, w_br, w_out, ff_w_gate, ff_w_up, ff_w_down, moe_router, moe_w1, moe_w3, moe_w2):
    p = dict(mix_norm=mix_norm, ffn_norm=ffn_norm, w_in=w_in, a_qn=a_qn, a_kn=a_kn, a_sink=a_sink,
             b_qn=b_qn, b_kn=b_kn, c_qn=c_qn, c_kn=c_kn, c_lq1=c_lq1, c_lk1=c_lk1, c_lq2=c_lq2,
             c_lk2=c_lk2, c_subln=c_subln, d_q_norm=d_q_norm, d_kv_norm=d_kv_norm, d_w_uq=d_w_uq,
             d_w_ukv=d_w_ukv, d_qn_nope=d_qn_nope, d_kn_nope=d_kn_nope, d_qn_rope=d_qn_rope,
             d_kn_rope=d_kn_rope, w_br=w_br, w_out=w_out)
    nb, seq, _ = x.shape
    assert seq % TM == 0 and ctx.shape[1] == CTX_LEN and nb <= 15
    depth = w_mod.shape[0]
    lat = x.reshape(nb * seq, D_MODEL)
    cx = ctx.reshape(nb * CTX_LEN, D_MODEL)

    cond = jnp.zeros((16, D_MODEL), F32).at[:nb].set(c).at[nb].set(c_ctx)
    mod_all = _modulation(cond, w_mod, b_mod).reshape(depth, 16, 6, D_MODEL)
    rope_lat = _rope_tables(seq, True)
    rope_ctx = _rope_tables(CTX_LEN, False)
    mavg = _group_avg_mats()

    for l in range(depth):
        last = l == depth - 1
        lp = _prep_layer(l, p)
        mod_lat = mod_all[l, :nb]
        mod_ctx = jnp.broadcast_to(mod_all[l, nb:nb + 1], (nb, 6, D_MODEL))
        common = (lp["vec"], lp["gains"])
        wts = (lp["wuq"], lp["wukv"])
        if last:
            kv_ctx = _project(cx, mod_ctx, *common, rope_ctx, mavg,
                              lp["w_in"][:, Q_COLS:Q_COLS + KV_PAD_COLS], *wts, seq=CTX_LEN,
                              tm=CTX_LEN, with_q=False)
        else:
            outs = _project(cx, mod_ctx, *common, rope_ctx, mavg, lp["w_in"], *wts, seq=CTX_LEN,
                            tm=CTX_LEN, with_q=True)
            kv_ctx, q_ctx, g_ctx = outs[:8], outs[8:12], outs[12]
        outs = _project(lat, mod_lat, *common, rope_lat, mavg, lp["w_in"], *wts, seq=seq,
                        tm=TM, with_q=True)
        kv_lat, q_lat, g_lat = outs[:8], outs[8:12], outs[12]
        br_lat = _mixers(q_lat, [kv_lat, kv_ctx], lp, seq_q=seq, tag="lat")
        moe = l % 2 == 1
        i = l // 2
        router_t = None
        if moe:
            wr = moe_router[i].T
            wr_hi = wr.astype(BF16)
            router_t = jnp.stack([wr_hi, (wr - wr_hi.astype(F32)).astype(BF16)])
        res = _merge(br_lat, g_lat, lat, mod_lat, lp["ffn_norm"], lp["wbr"], lp["wout"], router_t, seq=seq)
        if not last:
            br_ctx = _mixers(q_ctx, [kv_ctx], lp, seq_q=CTX_LEN, tag="ctx")
            res_ctx = _merge(br_ctx, g_ctx, cx, mod_ctx, lp["ffn_norm"], lp["wbr"], lp["wout"],
                             router_t, seq=CTX_LEN)
        if not moe:
            wg, wu, wd = (ff_w_gate[i].astype(BF16), ff_w_up[i].astype(BF16), ff_w_down[i].astype(BF16))
            lat = _dense_ffn(res[1], res[0], mod_lat, wg, wu, wd, seq=seq)
            if not last:
                cx = _dense_ffn(res_ctx[1], res_ctx[0], mod_ctx, wg, wu, wd, seq=CTX_LEN)
        else:
            w1, w3, w2 = moe_w1[i].astype(BF16), moe_w3[i].astype(BF16), moe_w2[i].astype(BF16)

            def routed(res_t, mod_t, seq_t):
                xr, hr, lg = res_t
                dest, wts_r, blk_e, nblk, nblk_max = _route(lg)
                xbuf = _dispatch(hr, dest, nblk_max * MOE_TM)
                ybuf = _moe_ffn(xbuf, blk_e, nblk, nblk_max, w1, w3, w2)
                return _combine(ybuf, dest, wts_r, xr, mod_t, seq=seq_t)

            lat = routed(res, mod_lat, seq)
            if not last:
                cx = routed(res_ctx, mod_ctx, CTX_LEN)
    return lat.reshape(nb, seq, D_MODEL)
```

```python
import functools
import math

import numpy as np
import jax
import jax.numpy as jnp
from jax import lax
from jax.experimental import pallas as pl
from jax.experimental.pallas import tpu as pltpu

F32 = jnp.float32
BF16 = jnp.bfloat16

D_MODEL = 1024
DEPTH = 2
CTX_LEN = 256
GRID_W = 64
HD = 64
BLOCK = 128
WINDOW = 128
ROPE_THETA = 10000.0
EPS = 1e-6
NEG_INF = -1e30
A_HEADS = 8
C_HEADS = 4
D_HEADS = 8
Q_LORA = 512
KV_LORA = 256
QK_NOPE = 64
QK_ROPE = 32
V_HEAD = 64
BRANCH_W = 512
D_FF = 2816
N_EXPERTS = 8
TOP_K = 2
EXPERT_FF = 3584

LANES = 128
VMEM_LIMIT = 56 * 1024 * 1024
LOG2E = math.log2(math.e)

Q_COLS = 2048
KV_PAD_COLS = 1920
GATE_COLS = 4 * D_MODEL
W_COLS = Q_COLS + KV_PAD_COLS + GATE_COLS
HEAD_PERM = (0, 4, 1, 5, 2, 6, 3, 7)

TM = 512
MERGE_TM = 512
MOE_TM = 512
MOE_TF = 1792
FFN_TM = 512
FFN_TF = 256


def _cparams(sem):
    return pltpu.CompilerParams(dimension_semantics=sem, vmem_limit_bytes=VMEM_LIMIT)


def _resident(shape):
    nd = len(shape)
    return pl.BlockSpec(shape, lambda *_: (0,) * nd, pipeline_mode=pl.Buffered(1))


def _sigmoid(x):
    return 1.0 / (1.0 + jnp.exp(-x))


def _dot(a, b):
    return jnp.dot(a, b, preferred_element_type=F32)


def _dot_nt(a, b):
    return lax.dot_general(a, b, (((1,), (1,)), ((), ())), preferred_element_type=F32)


def _mod_kernel(c_ref, w_ref, b_ref, o_ref):
    cond = c_ref[...]
    s = cond * _sigmoid(cond)
    o_ref[0] = jnp.dot(s, w_ref[0], precision=lax.Precision.HIGHEST,
                       preferred_element_type=F32) + b_ref[0]


def _modulation(cond, w_mod, b_mod):
    depth = w_mod.shape[0]
    nct = 6 * D_MODEL // 1024
    return pl.pallas_call(
        _mod_kernel,
        out_shape=jax.ShapeDtypeStruct((depth, 16, 6 * D_MODEL), F32),
        grid=(depth, nct),
        in_specs=[pl.BlockSpec((16, D_MODEL), lambda l, j: (0, 0)),
                  pl.BlockSpec((1, D_MODEL, 1024), lambda l, j: (l, 0, j)),
                  pl.BlockSpec((1, 1, 1024), lambda l, j: (l, 0, j))],
        out_specs=pl.BlockSpec((1, 16, 1024), lambda l, j: (l, 0, j)),
        compiler_params=_cparams(("arbitrary", "arbitrary")),
        name="modulation",
    )(cond, w_mod, b_mod.reshape(depth, 1, 6 * D_MODEL))


def _adaln(x, g, shift, scale):
    ms = jnp.mean(x * x, axis=-1, keepdims=True)
    y = x * lax.rsqrt(ms + EPS) * g
    return y * (1.0 + scale) + shift


def _group_mean_sq(r, mavg):
    sq = (r * r).astype(BF16)
    w = r.shape[1]
    parts = []
    for c in range(0, w, 2 * LANES):
        cw = min(2 * LANES, w - c)
        parts.append(_dot(sq[:, c:c + cw], mavg[:cw, :cw]))
    return parts[0] if len(parts) == 1 else jnp.concatenate(parts, axis=1)


def _rope_slab(y, cos, sin, half):
    lane = lax.broadcasted_iota(jnp.int32, y.shape, 1)
    first = (lane % (2 * half)) < half
    sw = jnp.where(first, pltpu.roll(y, LANES - half, 1), pltpu.roll(y, half, 1))
    return y * cos + sw * sin


def _norm_rope(r, ms, gain, cos, sin, half):
    outs = []
    for c in range(0, r.shape[1], LANES):
        y = r[:, c:c + LANES] * lax.rsqrt(ms[:, c:c + LANES] + EPS) * gain
        if cos is not None:
            y = _rope_slab(y, cos, sin, half)
        outs.append(y)
    return outs[0] if len(outs) == 1 else jnp.concatenate(outs, axis=1)


def _proj_kernel(with_q, x_ref, mod_ref, vec_ref, gains_ref, rope_ref, mavg_ref,
                 w_ref, wuq_ref, wukv_ref, *refs):
    if with_q:
        (qa_ref, qb_ref, qc_ref, qd_ref, g_ref) = refs[-5:]
        kv_refs = refs[-13:-5]
        kvo, go = Q_COLS, Q_COLS + KV_PAD_COLS
    else:
        kv_refs = refs[-8:]
        kvo = 0
    ka_ref, va_ref, kb_ref, vb_ref, kc_ref, vc_ref, kd_ref, vd_ref = kv_refs

    m = mod_ref[0]
    hb = _adaln(x_ref[...], vec_ref[0:1, :], m[0:1], m[1:2]).astype(BF16)

    def mm(lo, hi):
        return _dot(hb, w_ref[:, lo:hi])

    m64 = mavg_ref[0]
    md = mavg_ref[1]
    c64, s64, cd, sd = rope_ref[0], rope_ref[1], rope_ref[2], rope_ref[3]
    gains = gains_ref[...]

    def gain(i):
        return gains[i:i + 1, :]

    def gates(n):
        gl = mm(go + n * D_MODEL, go + (n + 1) * D_MODEL)
        g_ref[:, n * D_MODEL:(n + 1) * D_MODEL] = _sigmoid(gl).astype(BF16)

    def full_rms(r, g):
        return (r * lax.rsqrt(jnp.mean(r * r, axis=-1, keepdims=True) + EPS) * g).astype(BF16)

    if with_q:
        r_qa, r_qb, r_qc, cq = mm(0, 512), mm(512, 1024), mm(1024, 1536), mm(1536, 2048)
    kvab = mm(kvo, kvo + 512)
    r_kc = mm(kvo + 512, kvo + 1024)
    r_vc = mm(kvo + 1024, kvo + 1536)
    ckv = mm(kvo + 1536, kvo + 1792)
    r_kr = mm(kvo + 1792, kvo + 1920)
    if with_q:
        gates(0)
        ms_qa, ms_qb, ms_qc = (_group_mean_sq(r, m64) for r in (r_qa, r_qb, r_qc))
        qd = _dot(full_rms(cq, vec_ref[1:2, 0:Q_LORA]), wuq_ref[...])
    r_ka, r_kb = kvab[:, 0:128], kvab[:, 256:384]
    ms_ka, ms_kb, ms_kc = (_group_mean_sq(r, m64) for r in (r_ka, r_kb, r_kc))
    ms_kr = _group_mean_sq(r_kr, md)
    kvd = _dot(full_rms(ckv, vec_ref[1:2, Q_LORA:Q_LORA + KV_LORA]), wukv_ref[...])
    if with_q:
        gates(1)
        qa_ref[...] = _norm_rope(r_qa, ms_qa, gain(0), c64, s64, 32).astype(BF16)
        qb_ref[...] = _norm_rope(r_qb, ms_qb, gain(1), c64, s64, 32).astype(BF16)
        qc_ref[...] = _norm_rope(r_qc, ms_qc, gain(2), c64, s64, 32).astype(BF16)
        ms_qd = _group_mean_sq(qd, md)
    r_kn = kvd[:, 0:D_HEADS * LANES]
    ms_kn = _group_mean_sq(r_kn, md)
    if with_q:
        gates(2)
    ka_ref[...] = _norm_rope(r_ka, ms_ka, gain(3), c64, s64, 32).astype(BF16)
    va_ref[...] = kvab[:, 128:256].astype(BF16)
    kb_ref[...] = _norm_rope(r_kb, ms_kb, gain(4), c64, s64, 32).astype(BF16)
    vb_ref[...] = kvab[:, 384:512].astype(BF16)
    kc_ref[...] = _norm_rope(r_kc, ms_kc, gain(5), c64, s64, 32).astype(BF16)
    vc_ref[...] = r_vc.astype(BF16)
    if with_q:
        qd_ref[...] = _norm_rope(qd, ms_qd, gain(6), cd, sd, 16).astype(BF16)
        gates(3)
    kr = _norm_rope(r_kr, ms_kr, gain(8), cd, sd, 16)
    kn = _norm_rope(r_kn, ms_kn, gain(7), None, None, 0)
    kd_ref[...] = (kn + jnp.concatenate([kr] * D_HEADS, axis=1)).astype(BF16)
    vd_ref[...] = kvd[:, D_HEADS * LANES:].astype(BF16)


_KV_WIDTHS = (128, 128, 128, 128, 512, 512, D_HEADS * LANES, D_HEADS * V_HEAD)


def _project(x, mod, vec, gains, rope, mavg, w, wuq, wukv, *, seq, tm, with_q):
    n = x.shape[0]
    tps = seq // tm
    grid = (n // tm,)

    def tok(i):
        return (i, 0)

    in_specs = [
        pl.BlockSpec((tm, D_MODEL), tok),
        pl.BlockSpec((1, 6, D_MODEL), lambda i: (i // tps, 0, 0)),
        _resident(vec.shape),
        _resident(gains.shape),
        pl.BlockSpec((4, tm, LANES), lambda i: (0, i % tps, 0)),
        _resident(mavg.shape),
        _resident(w.shape),
        _resident(wuq.shape),
        _resident(wukv.shape),
    ]
    args = [x, mod, vec, gains, rope, mavg, w, wuq, wukv]
    out_shapes = [jax.ShapeDtypeStruct((n, wd), BF16) for wd in _KV_WIDTHS]
    out_specs = [pl.BlockSpec((tm, wd), tok) for wd in _KV_WIDTHS]
    if with_q:
        for wd in (512, 512, 512, D_HEADS * LANES, GATE_COLS):
            out_shapes.append(jax.ShapeDtypeStruct((n, wd), BF16))
            out_specs.append(pl.BlockSpec((tm, wd), tok))
    return pl.pallas_call(
        functools.partial(_proj_kernel, with_q),
        out_shape=out_shapes, grid=grid, in_specs=in_specs, out_specs=out_specs,
        compiler_params=_cparams(("arbitrary",)),
        name="project_q" if with_q else "project_kv",
    )(*args)


LOGIT_SAFE = 40.0


def _scores(q, kv, mask=None):
    scores = [_dot_nt(q, k) for k, _ in kv]
    if mask is not None:
        scores[0] = jnp.where(mask, scores[0], NEG_INF)
    return scores


def _attend(shift, scores, kv, extra_logit=None):
    if shift:
        m = functools.reduce(jnp.maximum, [jnp.max(s, axis=-1, keepdims=True) for s in scores])
        if extra_logit is not None:
            m = jnp.maximum(m, extra_logit)
            extra_logit = extra_logit - m
        scores = [s - m for s in scores]
    probs = [jnp.exp2(s) for s in scores]
    l = functools.reduce(jnp.add, [jnp.sum(p, axis=-1, keepdims=True) for p in probs])
    if extra_logit is not None:
        l = l + jnp.exp2(extra_logit)
    o = functools.reduce(jnp.add, [_dot(p.astype(BF16), v) for p, (_, v) in zip(probs, kv)])
    return o / l


def _run_skewed(units):
    pending = None
    for start, finish in units:
        state = start()
        if pending is not None:
            pending[0](pending[1])
        pending = (finish, state)
    pending[0](pending[1])


def _guarded(body, idx, safe_ref, *refs):
    @pl.when(safe_ref[idx] != 0)
    def _():
        body(False, *refs)

    @pl.when(safe_ref[idx] == 0)
    def _():
        body(True, *refs)


def _split_heads(qs):
    lane = lax.broadcasted_iota(jnp.int32, qs.shape, 1)
    zero = jnp.zeros_like(qs)
    return jnp.concatenate([jnp.where(lane < HD, qs, zero), jnp.where(lane >= HD, qs, zero)], axis=0)


def _merge_halves(o, tq):
    lane = lax.broadcasted_iota(jnp.int32, (tq, LANES), 1)
    return jnp.where(lane < HD, o[:tq], o[tq:])


def _kv_cols(kv_refs, kc, vc):
    return [(k_ref[:, kc * LANES:(kc + 1) * LANES], v_ref[:, vc * LANES:(vc + 1) * LANES])
            for k_ref, v_ref in kv_refs]


def _one_group(groups):
    return [(jnp.concatenate([k for k, _ in groups], axis=0), jnp.concatenate([v for _, v in groups], axis=0))]


def _mix_b(shift, q_ref, kv_refs, o_ref):
    tq = q_ref.shape[0]
    kv = _kv_cols(kv_refs, 0, 0)

    def start(j):
        return _scores(_split_heads(q_ref[:, j * LANES:(j + 1) * LANES]), kv)

    def finish(j, scores):
        o = _attend(shift, scores, kv)
        o_ref[:, j * LANES:(j + 1) * LANES] = _merge_halves(o, tq).astype(BF16)

    return [(functools.partial(start, j), functools.partial(finish, j)) for j in range(4)]


def _mix_a(shift, sink_ref, q_ref, kv_refs, o_ref):
    tq = q_ref.shape[0]
    if len(kv_refs) == 2:
        (kl_ref, vl_ref), (kc_ref, vc_ref) = kv_refs
        span = tq + 2 * WINDOW
        q0 = pl.program_id(1) * tq
        start = pl.multiple_of(jnp.clip(q0 - WINDOW, 0, kl_ref.shape[0] - span), BLOCK)
        kv = [(kl_ref[pl.ds(start, span), :], vl_ref[pl.ds(start, span), :]), (kc_ref[...], vc_ref[...])]
        row = lax.broadcasted_iota(jnp.int32, (2 * tq, span), 0)
        col = lax.broadcasted_iota(jnp.int32, (2 * tq, span), 1)
        qpos = q0 + jnp.where(row >= tq, row - tq, row)
        dist = (start + col) - qpos
        valid = (dist >= -WINDOW) & (dist <= WINDOW)
    else:
        kv = _kv_cols(kv_refs, 0, 0)
        valid = None
    rows = lax.broadcasted_iota(jnp.int32, (2 * tq, 1), 0)

    def start(j):
        return _scores(_split_heads(q_ref[:, j * LANES:(j + 1) * LANES]), kv, valid)

    def finish(j, scores):
        sink = jnp.where(rows < tq, sink_ref[2 * j], sink_ref[2 * j + 1]) * LOG2E
        o = _attend(shift, scores, kv, extra_logit=sink)
        o_ref[:, j * LANES:(j + 1) * LANES] = _merge_halves(o, tq).astype(BF16)

    return [(functools.partial(start, j), functools.partial(finish, j)) for j in range(4)]


def _mix_c(lam_init, shift, lq1_ref, lk1_ref, lq2_ref, lk2_ref, subln_ref, q_ref, kv_refs, o_ref):
    tq = q_ref.shape[0]
    lam = (jnp.exp(jnp.sum(lq1_ref[...] * lk1_ref[...], axis=-1, keepdims=True))
           - jnp.exp(jnp.sum(lq2_ref[...] * lk2_ref[...], axis=-1, keepdims=True)) + lam_init)

    def start(c):
        kv = _one_group(_kv_cols(kv_refs, c, c))
        return kv, _scores(_split_heads(q_ref[:, c * LANES:(c + 1) * LANES]), kv)

    def finish(c, state):
        kv, scores = state
        o = _attend(shift, scores, kv)
        oc = o[:tq] - lam * o[tq:]
        oc = oc * lax.rsqrt(jnp.mean(oc * oc, axis=-1, keepdims=True) + EPS) * subln_ref[...]
        o_ref[:, c * LANES:(c + 1) * LANES] = (oc * (1.0 - lam_init)).astype(BF16)

    _run_skewed([(functools.partial(start, c), functools.partial(finish, c)) for c in range(C_HEADS)])


def _mix_d(shift, q_ref, kv_refs, o_ref):
    tq = q_ref.shape[0]

    def start(h):
        return _scores(q_ref[:, h * LANES:(h + 1) * LANES], _kv_cols(kv_refs, h, h // 2))

    done = {}

    def finish(h, scores):
        done[h] = _attend(shift, scores, _kv_cols(kv_refs, h, h // 2))
        if h % 2 == 1:
            o = jnp.concatenate([done.pop(h - 1), done.pop(h)], axis=0)
            o_ref[:, (h // 2) * LANES:(h // 2 + 1) * LANES] = _merge_halves(o, tq).astype(BF16)

    _run_skewed([(functools.partial(start, h), functools.partial(finish, h)) for h in range(D_HEADS)])


def _mix_ab(shift, sink_ref, qa_ref, kva_refs, oa_ref, qb_ref, kvb_refs, ob_ref):
    units = []
    for unit_b, unit_a in zip(_mix_b(shift, qb_ref, kvb_refs, ob_ref),
                              _mix_a(shift, sink_ref, qa_ref, kva_refs, oa_ref)):
        units += [unit_b, unit_a]
    _run_skewed(units)


ATTN_TQ = 256


def _attention(body, safe_idx, safe, extra, extra_specs, units, *, seq_q, name):
    n = units[0][0].shape[0]
    nb = n // seq_q
    tq = ATTN_TQ
    nq = seq_q // tq
    tok = lambda b, i: (b * nq + i, 0)
    per_batch = lambda a: pl.BlockSpec((a.shape[0] // nb, a.shape[1]), lambda b, i: (b, 0))
    args, in_specs, counts = [], [], []
    for q, kv in units:
        args.append(q)
        in_specs.append(pl.BlockSpec((tq, q.shape[1]), tok))
        for pair in kv:
            args.extend(pair)
            in_specs.extend(per_batch(a) for a in pair)
        counts.append(len(kv))
    n_extra = len(extra)

    def kernel(shift, *refs):
        ins, outs = refs[n_extra:len(refs) - len(units)], refs[len(refs) - len(units):]
        packed, pos = [], 0
        for u, nkv in enumerate(counts):
            kv_refs = [(ins[pos + 1 + 2 * g], ins[pos + 2 + 2 * g]) for g in range(nkv)]
            packed.extend([ins[pos], kv_refs, outs[u]])
            pos += 1 + 2 * nkv
        body(shift, *refs[:n_extra], *packed)

    out = pl.pallas_call(
        functools.partial(_guarded, kernel, safe_idx),
        out_shape=[jax.ShapeDtypeStruct((n, BRANCH_W), BF16)] * len(units),
        grid=(nb, nq),
        in_specs=[pl.BlockSpec(memory_space=pltpu.SMEM)] + list(extra_specs) + in_specs,
        out_specs=[pl.BlockSpec((tq, BRANCH_W), tok)] * len(units),
        compiler_params=_cparams(("arbitrary", "arbitrary")),
        name=name,
    )(safe, *extra, *args)
    return out


def _mixers(qs, kv_groups, lp, *, seq_q, tag):
    qa, qb, qc, qd = qs
    pairs = lambda j: [(g[2 * j], g[2 * j + 1]) for g in kv_groups]
    safe = lp["safe"]
    row = lambda w: pl.BlockSpec((1, w), lambda b, i: (0, 0))
    oa, ob = _attention(_mix_ab, 0, safe, [lp["sink"]], [pl.BlockSpec(memory_space=pltpu.SMEM)],
                        [(qa, pairs(0)), (qb, pairs(1))], seq_q=seq_q, name="attn_ab_" + tag)
    oc, = _attention(functools.partial(_mix_c, lp["lam_init"]), 1, safe,
                     [lp["lq1"], lp["lk1"], lp["lq2"], lp["lk2"], lp["subln"]],
                     [row(HD)] * 4 + [row(2 * HD)], [(qc, pairs(2))], seq_q=seq_q, name="attn_c_" + tag)
    od, = _attention(_mix_d, 2, safe, [], [], [(qd, pairs(3))], seq_q=seq_q, name="attn_d_" + tag)
    return oa, ob, oc, od


def _merge_kernel(moe, oa_ref, ob_ref, oc_ref, od_ref, g_ref, x_ref, mod_ref, norm_ref,
                  wbr_ref, wout_ref, *refs):
    if moe:
        wr_ref, xo_ref, h_ref, lg_ref = refs
    else:
        xo_ref, h_ref = refs
    m = mod_ref[0]
    y = None
    for n, o_ref in enumerate((oa_ref, ob_ref, oc_ref, od_ref)):
        yn = g_ref[:, n * D_MODEL:(n + 1) * D_MODEL].astype(F32) * _dot(o_ref[...], wbr_ref[n])
        y = yn if y is None else y + yn
    z = _dot(y.astype(BF16), wout_ref[...])
    xn = x_ref[...] + m[2:3] * z
    xo_ref[...] = xn
    h = _adaln(xn, norm_ref[...], m[3:4], m[4:5])
    if moe:
        h_ref[...] = h
        h_hi = h.astype(BF16)
        h_lo = (h - h_hi.astype(F32)).astype(BF16)
        lg_ref[...] = (_dot_nt(wr_ref[0], h_hi) + _dot_nt(wr_ref[1], h_hi)) + _dot_nt(wr_ref[0], h_lo)
    else:
        h_ref[...] = h.astype(BF16)


def _merge(branches, g, x, mod, norm, wbr, wout, router_t, *, seq):
    n = x.shape[0]
    tm = min(MERGE_TM, seq)
    tps = seq // tm
    moe = router_t is not None
    tok = lambda i: (i, 0)
    in_specs = [pl.BlockSpec((tm, BRANCH_W), tok)] * 4 + [
        pl.BlockSpec((tm, GATE_COLS), tok),
        pl.BlockSpec((tm, D_MODEL), tok),
        pl.BlockSpec((1, 6, D_MODEL), lambda i: (i // tps, 0, 0)),
        _resident(norm.shape), _resident(wbr.shape), _resident(wout.shape)]
    args = list(branches) + [g, x, mod, norm, wbr, wout]
    out_shape = [jax.ShapeDtypeStruct((n, D_MODEL), F32),
                 jax.ShapeDtypeStruct((n, D_MODEL), F32 if moe else BF16)]
    out_specs = [pl.BlockSpec((tm, D_MODEL), tok), pl.BlockSpec((tm, D_MODEL), tok)]
    if moe:
        in_specs.append(_resident(router_t.shape))
        args.append(router_t)
        out_shape.append(jax.ShapeDtypeStruct((N_EXPERTS, n), F32))
        out_specs.append(pl.BlockSpec((N_EXPERTS, tm), lambda i: (0, i)))
    return pl.pallas_call(
        functools.partial(_merge_kernel, moe),
        out_shape=out_shape, grid=(n // tm,), in_specs=in_specs, out_specs=out_specs,
        compiler_params=_cparams(("arbitrary",)),
        name="merge_moe" if moe else "merge_dense",
    )(*args)


def _ffn_kernel(h_ref, x_ref, mod_ref, wg_ref, wu_ref, wd_ref, o_ref, hid_ref):
    h = h_ref[...]
    for c in range(0, D_FF, FFN_TF):
        a = _dot(h, wg_ref[:, c:c + FFN_TF])
        u = _dot(h, wu_ref[:, c:c + FFN_TF])
        hid_ref[:, c:c + FFN_TF] = (a * _sigmoid(a) * u).astype(BF16)
    m = mod_ref[0]
    o_ref[...] = x_ref[...] + m[5:6] * _dot(hid_ref[...], wd_ref[...])


def _dense_ffn(h, x, mod, wg, wu, wd, *, seq):
    n = x.shape[0]
    tm = min(FFN_TM, seq)
    tps = seq // tm
    tok = lambda i: (i, 0)
    return pl.pallas_call(
        _ffn_kernel,
        out_shape=jax.ShapeDtypeStruct((n, D_MODEL), F32),
        grid=(n // tm,),
        in_specs=[pl.BlockSpec((tm, D_MODEL), tok), pl.BlockSpec((tm, D_MODEL), tok),
                  pl.BlockSpec((1, 6, D_MODEL), lambda i: (i // tps, 0, 0)),
                  _resident(wg.shape), _resident(wu.shape), _resident(wd.shape)],
        out_specs=pl.BlockSpec((tm, D_MODEL), tok),
        scratch_shapes=[pltpu.VMEM((tm, D_FF), BF16)],
        compiler_params=_cparams(("arbitrary",)),
        name="dense_ffn",
    )(h, x, mod, wg, wu, wd)


ROUTE_CHUNK = 512
DMA_UNROLL = 8


def _route_kernel(n, lg_ref, tri_ref, dest_ref, wts_ref, meta_ref, mem_ref, pos_ref):
    lg = lg_ref[...]
    eidx = lax.broadcasted_iota(jnp.int32, lg.shape, 0).astype(F32)
    none = float(N_EXPERTS)
    m1 = jnp.max(lg, axis=0, keepdims=True)
    i1 = jnp.min(jnp.where(lg == m1, eidx, none), axis=0, keepdims=True)
    lg2 = jnp.where(eidx == i1, -jnp.inf, lg)
    m2 = jnp.max(lg2, axis=0, keepdims=True)
    i2 = jnp.min(jnp.where(lg2 == m2, eidx, none), axis=0, keepdims=True)
    e = jnp.exp(m2 - m1)
    w1 = 1.0 / (1.0 + e)
    wts_ref[0:1, :] = w1
    wts_ref[1:2, :] = e * w1
    mem_ref[...] = jnp.where(eidx == i1, 1.0, 0.0) + jnp.where(eidx == i2, 1.0, 0.0)

    carry = jnp.zeros((N_EXPERTS, 1), F32)
    tri = tri_ref[...]
    for c in range(0, n, ROUTE_CHUNK):
        mc = mem_ref[:, c:c + ROUTE_CHUNK]
        pos_ref[:, c:c + ROUTE_CHUNK] = carry + _dot(mc.astype(BF16), tri)
        carry = carry + jnp.sum(mc, axis=1, keepdims=True)
    padded = jnp.floor((carry + (MOE_TM - 1.0)) * (1.0 / MOE_TM)) * MOE_TM
    esub = lax.broadcasted_iota(jnp.int32, (N_EXPERTS, 1), 0)
    pstart = jnp.zeros((N_EXPERTS, 1), F32)
    for k in range(N_EXPERTS - 1):
        pstart = pstart + jnp.where(esub > k, padded[k:k + 1, :], 0.0)
    pend = pstart + padded
    slot = pstart + pos_ref[...]
    dest_ref[0:1, :] = jnp.sum(jnp.where(eidx == i1, slot, 0.0), axis=0, keepdims=True).astype(jnp.int32)
    dest_ref[1:2, :] = jnp.sum(jnp.where(eidx == i2, slot, 0.0), axis=0, keepdims=True).astype(jnp.int32)

    nblk = pend[N_EXPERTS - 1:N_EXPERTS, :] * (1.0 / MOE_TM)
    brow = jnp.minimum(lax.broadcasted_iota(jnp.int32, (1, LANES), 1).astype(F32), nblk - 1.0) * MOE_TM
    be = jnp.zeros((1, LANES), F32)
    for k in range(N_EXPERTS):
        be = be + jnp.where(pend[k:k + 1, :] <= brow, 1.0, 0.0)
    lane = lax.broadcasted_iota(jnp.int32, (1, LANES), 1)
    last_blk = jnp.zeros((1, LANES), F32)
    used = jnp.zeros((1, LANES), F32)
    for k in range(N_EXPERTS):
        last_blk = jnp.where(lane == k, pend[k:k + 1, :] * (1.0 / MOE_TM) - 1.0, last_blk)
        used = jnp.where(lane == k, jnp.where(padded[k:k + 1, :] > 0.0, 1.0, 0.0), used)
    meta_ref[...] = jnp.zeros(meta_ref.shape, jnp.int32)
    meta_ref[0:1, :] = jnp.minimum(be, N_EXPERTS - 1.0).astype(jnp.int32)
    meta_ref[1:2, :] = jnp.broadcast_to(nblk, (1, LANES)).astype(jnp.int32)
    meta_ref[2:3, :] = last_blk.astype(jnp.int32)
    meta_ref[3:4, :] = used.astype(jnp.int32)


def _route(logits_t):
    n = logits_t.shape[1]
    nblk_max = n * TOP_K // MOE_TM + N_EXPERTS
    assert nblk_max <= LANES and n % ROUTE_CHUNK == 0
    tri = jnp.asarray(np.triu(np.ones((ROUTE_CHUNK, ROUTE_CHUNK)), 1), BF16)
    whole = lambda shape: pl.BlockSpec(shape, lambda: (0,) * len(shape))
    dest, wts, meta = pl.pallas_call(
        functools.partial(_route_kernel, n),
        out_shape=[jax.ShapeDtypeStruct((TOP_K, n), jnp.int32),
                   jax.ShapeDtypeStruct((TOP_K, n), F32),
                   jax.ShapeDtypeStruct((8, LANES), jnp.int32)],
        in_specs=[whole(logits_t.shape), whole(tri.shape)],
        out_specs=[whole((TOP_K, n)), whole((TOP_K, n)), whole((8, LANES))],
        scratch_shapes=[pltpu.VMEM((N_EXPERTS, n), F32), pltpu.VMEM((N_EXPERTS, n), F32)],
        compiler_params=pltpu.CompilerParams(vmem_limit_bytes=VMEM_LIMIT),
        name="moe_route",
    )(logits_t, tri)
    return dest, wts, meta, nblk_max


def _dispatch_kernel(tg, nblk_max, dest_ref, meta_ref, h_ref, xbuf_hbm, sem, zero_ref, zero_sem):
    @pl.when(pl.program_id(0) == 0)
    def _():
        zero_ref[...] = jnp.zeros_like(zero_ref)

        def fill(blk):
            cp = pltpu.make_async_copy(
                zero_ref, xbuf_hbm.at[pl.ds(pl.multiple_of(blk * MOE_TM, MOE_TM), MOE_TM)], zero_sem)
            cp.start()
            cp.wait()

        for e in range(N_EXPERTS):
            @pl.when(meta_ref[3, e] != 0)
            def _():
                fill(meta_ref[2, e])

        def tail(b, carry):
            fill(b)
            return carry

        lax.fori_loop(meta_ref[1, 0], nblk_max, tail, 0)

    def copies(r):
        src = h_ref.at[pl.ds(r, 1)]
        return (pltpu.make_async_copy(src, xbuf_hbm.at[pl.ds(dest_ref[0, 0, r], 1)], sem),
                pltpu.make_async_copy(src, xbuf_hbm.at[pl.ds(dest_ref[0, 0, tg + r], 1)], sem))

    def start(r, carry):
        for k, cp in enumerate(copies(r)):
            cp.start(priority=k)
        return carry

    def wait(r, carry):
        for cp in copies(r):
            cp.wait()
        return carry

    lax.fori_loop(0, tg, start, 0, unroll=DMA_UNROLL)
    lax.fori_loop(0, tg, wait, 0, unroll=DMA_UNROLL)


def _tile_dest(dest, t):
    n = dest.shape[1]
    return jnp.transpose(dest.reshape(TOP_K, n // t, t), (1, 0, 2)).reshape(n // t, 1, TOP_K * t)


def _dispatch(h, dest, meta, nblk_max):
    n = h.shape[0]
    tg = min(1024, n)
    nt = n // tg
    dest_t = _tile_dest(dest, tg)
    return pl.pallas_call(
        functools.partial(_dispatch_kernel, tg, nblk_max),
        out_shape=jax.ShapeDtypeStruct((nblk_max * MOE_TM, D_MODEL), F32),
        grid=(nt,),
        in_specs=[pl.BlockSpec((1, 1, TOP_K * tg), lambda i: (i, 0, 0), memory_space=pltpu.SMEM),
                  pl.BlockSpec(memory_space=pltpu.SMEM),
                  pl.BlockSpec((tg, D_MODEL), lambda i: (i, 0))],
        out_specs=pl.BlockSpec(memory_space=pl.ANY),
        scratch_shapes=[pltpu.SemaphoreType.DMA(()), pltpu.VMEM((MOE_TM, D_MODEL), F32),
                        pltpu.SemaphoreType.DMA(())],
        compiler_params=_cparams(("arbitrary",)),
        name="moe_dispatch",
    )(dest_t, meta, h)


def _moe_kernel(nf, blk_e_ref, nblk_ref, x_ref, w1_ref, w3_ref, w2_ref, o_ref, xb_ref, hid_ref):
    del blk_e_ref
    i = pl.program_id(0)
    j = pl.program_id(1)

    @pl.when(i < nblk_ref[0])
    def _():
        @pl.when(j == 0)
        def _():
            xb_ref[...] = x_ref[...].astype(BF16)

        xb = xb_ref[...]
        for c in range(0, MOE_TF, FFN_TF):
            a = _dot(xb, w1_ref[0, :, c:c + FFN_TF])
            u = _dot(xb, w3_ref[0, :, c:c + FFN_TF])
            hid_ref[:, c:c + FFN_TF] = (a * _sigmoid(a) * u).astype(BF16)
        y = _dot(hid_ref[...], w2_ref[0])

        @pl.when(j == 0)
        def _():
            o_ref[...] = y

        @pl.when(j > 0)
        def _():
            o_ref[...] += y

    @pl.when((i >= nblk_ref[0]) & (j == nf - 1))
    def _():
        o_ref[...] = jnp.zeros_like(o_ref)


def _moe_ffn(xbuf, blk_e, nblk, nblk_max, w1, w3, w2):
    nf = EXPERT_FF // MOE_TF

    def rows(i, j, be, nb):
        return (i, 0)

    def jj(i, j, nb):
        return jnp.where(i < nb[0], j, nf - 1)

    grid_spec = pltpu.PrefetchScalarGridSpec(
        num_scalar_prefetch=2, grid=(nblk_max, nf),
        in_specs=[pl.BlockSpec((MOE_TM, D_MODEL), rows),
                  pl.BlockSpec((1, D_MODEL, MOE_TF), lambda i, j, be, nb: (be[i], 0, jj(i, j, nb))),
                  pl.BlockSpec((1, D_MODEL, MOE_TF), lambda i, j, be, nb: (be[i], 0, jj(i, j, nb))),
                  pl.BlockSpec((1, MOE_TF, D_MODEL), lambda i, j, be, nb: (be[i], jj(i, j, nb), 0))],
        out_specs=pl.BlockSpec((MOE_TM, D_MODEL), rows),
        scratch_shapes=[pltpu.VMEM((MOE_TM, D_MODEL), BF16), pltpu.VMEM((MOE_TM, MOE_TF), BF16)])
    return pl.pallas_call(
        functools.partial(_moe_kernel, nf),
        out_shape=jax.ShapeDtypeStruct(xbuf.shape, F32),
        grid_spec=grid_spec,
        compiler_params=_cparams(("arbitrary", "arbitrary")),
        name="moe_experts",
    )(blk_e, nblk, xbuf, w1, w3, w2)


def _combine_kernel(tc, nt, dest_ref, dest_next_ref, wts_ref, x_ref, mod_ref, y_hbm, o_ref, yg_ref, sems):
    i = pl.program_id(0)
    slot = lax.rem(i, 2)

    def copies(d_ref, s, r):
        return (pltpu.make_async_copy(y_hbm.at[pl.ds(d_ref[0, 0, r], 1)],
                                      yg_ref.at[s, 0, pl.ds(r, 1)], sems.at[s]),
                pltpu.make_async_copy(y_hbm.at[pl.ds(d_ref[0, 0, tc + r], 1)],
                                      yg_ref.at[s, 1, pl.ds(r, 1)], sems.at[s]))

    def request(d_ref, s):
        def start(r, carry):
            for k, cp in enumerate(copies(d_ref, s, r)):
                cp.start(priority=k)
            return carry
        lax.fori_loop(0, tc, start, 0, unroll=DMA_UNROLL)

    @pl.when(i == 0)
    def _():
        request(dest_ref, slot)

    @pl.when(i + 1 < nt)
    def _():
        request(dest_next_ref, 1 - slot)

    def wait(r, carry):
        for cp in copies(dest_ref, slot, r):
            cp.wait()
        return carry

    lax.fori_loop(0, tc, wait, 0, unroll=DMA_UNROLL)
    m = mod_ref[0]
    w = wts_ref[...]
    y = w[:, 0:1] * yg_ref[slot, 0] + w[:, 1:2] * yg_ref[slot, 1]
    o_ref[...] = x_ref[...] + m[5:6] * y


def _combine(ybuf, dest, wts, x, mod, *, seq):
    n = x.shape[0]
    tc = 256
    nt = n // tc
    tps = seq // tc
    dest_t = _tile_dest(dest, tc)
    wts = wts.T
    tok = lambda i: (i, 0)
    slots = lambda step: pl.BlockSpec((1, 1, TOP_K * tc), lambda i: (jnp.minimum(i + step, nt - 1), 0, 0),
                                      memory_space=pltpu.SMEM)
    return pl.pallas_call(
        functools.partial(_combine_kernel, tc, nt),
        out_shape=jax.ShapeDtypeStruct((n, D_MODEL), F32),
        grid=(nt,),
        in_specs=[slots(0), slots(1),
                  pl.BlockSpec((tc, TOP_K), tok),
                  pl.BlockSpec((tc, D_MODEL), tok),
                  pl.BlockSpec((1, 6, D_MODEL), lambda i: (i // tps, 0, 0)),
                  pl.BlockSpec(memory_space=pl.ANY)],
        out_specs=pl.BlockSpec((tc, D_MODEL), tok),
        scratch_shapes=[pltpu.VMEM((2, TOP_K, tc, D_MODEL), F32), pltpu.SemaphoreType.DMA((2,))],
        compiler_params=_cparams(("arbitrary",)),
        name="moe_combine",
    )(dest_t, dest_t, wts, x, mod, ybuf)


def _head_index():
    return np.concatenate([np.arange(h * HD, (h + 1) * HD) for h in HEAD_PERM])


def _prep_w_in(w):
    hp = _head_index()
    kv_lo = Q_COLS
    kr_lo = kv_lo + KV_PAD_COLS - LANES
    parts = [w[:, 0:512][:, hp], w[:, 512:1024][:, hp], w[:, 1024:Q_COLS],
             w[:, kv_lo:kr_lo],
             jnp.pad(w[:, kr_lo:kr_lo + QK_ROPE], ((0, 0), (QK_NOPE, LANES - QK_NOPE - QK_ROPE))),
             w[:, kr_lo + QK_ROPE:]]
    return jnp.concatenate(parts, axis=1).astype(BF16)


def _prep_layer(l, p):
    zeros = lambda k: jnp.zeros((k,), F32)
    sc = LOG2E / math.sqrt(HD)
    sc_d = LOG2E / math.sqrt(QK_NOPE + QK_ROPE)
    wuq = p["d_w_uq"][l].reshape(Q_LORA, D_HEADS, QK_NOPE + QK_ROPE)
    wuq = jnp.pad(wuq, ((0, 0), (0, 0), (0, LANES - QK_NOPE - QK_ROPE))).reshape(Q_LORA, D_HEADS * LANES)
    wukv = p["d_w_ukv"][l].reshape(KV_LORA, D_HEADS, QK_NOPE + V_HEAD)
    wuk = jnp.pad(wukv[:, :, :QK_NOPE], ((0, 0), (0, 0), (0, LANES - QK_NOPE))).reshape(KV_LORA, D_HEADS * LANES)
    wuv = wukv[:, :, QK_NOPE:].reshape(KV_LORA, D_HEADS * V_HEAD)
    gains = jnp.stack([
        jnp.tile(p["a_qn"][l], 2) * sc, jnp.tile(p["b_qn"][l], 2) * sc, jnp.tile(p["c_qn"][l], 2) * sc,
        jnp.tile(p["a_kn"][l], 2), jnp.tile(p["b_kn"][l], 2), jnp.tile(p["c_kn"][l], 2),
        jnp.concatenate([p["d_qn_nope"][l], p["d_qn_rope"][l], zeros(32)]) * sc_d,
        jnp.concatenate([p["d_kn_nope"][l], zeros(64)]),
        jnp.concatenate([zeros(64), p["d_kn_rope"][l], zeros(32)]),
    ] + [zeros(LANES)] * 7).astype(F32)
    vec = jnp.stack([p["mix_norm"][l],
                     jnp.concatenate([p["d_q_norm"][l], p["d_kv_norm"][l], zeros(D_MODEL - Q_LORA - KV_LORA)])]
                    + [zeros(D_MODEL)] * 6).astype(F32)
    wbr = p["w_br"][l]
    hp = _head_index()
    wbr = jnp.stack([wbr[0][hp], wbr[1][hp], wbr[2], wbr[3]]).astype(BF16)

    amax = lambda v: jnp.max(jnp.abs(v))
    bound64 = lambda gq, gk: 1.02 * HD * amax(gq) * amax(gk)
    nq_d = jnp.sqrt(QK_NOPE * amax(gains[6, :QK_NOPE]) ** 2 + QK_ROPE * amax(gains[6, QK_NOPE:]) ** 2)
    nk_d = jnp.sqrt(QK_NOPE * amax(gains[7]) ** 2 + QK_ROPE * amax(gains[8]) ** 2)
    bounds = jnp.stack([
        jnp.maximum(jnp.maximum(bound64(gains[0], gains[3]), bound64(gains[1], gains[4])),
                    LOG2E * amax(p["a_sink"][l])),
        bound64(gains[2], gains[5]),
        1.02 * nq_d * nk_d])
    safe = (bounds <= LOGIT_SAFE).astype(jnp.int32)
    return {
        "safe": safe,
        "w_in": _prep_w_in(p["w_in"][l]),
        "wuq": wuq.astype(BF16),
        "wukv": jnp.concatenate([wuk, wuv], axis=1).astype(BF16),
        "gains": gains, "vec": vec, "wbr": wbr, "wout": p["w_out"][l].astype(BF16),
        "ffn_norm": p["ffn_norm"][l].reshape(1, D_MODEL),
        "sink": p["a_sink"][l][np.asarray(HEAD_PERM)].astype(F32),
        "lq1": p["c_lq1"][l].reshape(1, HD), "lk1": p["c_lk1"][l].reshape(1, HD),
        "lq2": p["c_lq2"][l].reshape(1, HD), "lk2": p["c_lk2"][l].reshape(1, HD),
        "subln": p["c_subln"][l].reshape(1, 2 * HD),
        "lam_init": 0.8 - 0.6 * math.exp(-0.3 * l),
    }


def _group_avg_mats():
    m64 = np.kron(np.eye(4), np.full((HD, HD), 1.0 / HD))
    one = np.zeros((LANES, LANES))
    one[:QK_NOPE, :QK_NOPE] = 1.0 / QK_NOPE
    one[QK_NOPE:QK_NOPE + QK_ROPE, QK_NOPE:QK_NOPE + QK_ROPE] = 1.0 / QK_ROPE
    md = np.kron(np.eye(2), one)
    return jnp.asarray(np.stack([m64, md]), BF16)


def _rope_tables(seq, rotary):
    if not rotary:
        one = jnp.ones((seq, LANES), F32)
        zero = jnp.zeros((seq, LANES), F32)
        return jnp.stack([one, zero, one, zero])
    t = jnp.arange(seq)

    def angles(rot_dim):
        nfreq = rot_dim // 4
        inv = jnp.power(ROPE_THETA, -jnp.arange(nfreq, dtype=F32) / nfreq)
        return jnp.concatenate([(t // GRID_W).astype(F32)[:, None] * inv,
                                (t % GRID_W).astype(F32)[:, None] * inv], axis=-1)

    ah = angles(HD)
    c64 = jnp.tile(jnp.cos(ah), (1, 4))
    s64 = jnp.tile(jnp.concatenate([-jnp.sin(ah), jnp.sin(ah)], axis=1), (1, 2))
    ar = angles(QK_ROPE)
    one = jnp.ones((seq, 1), F32)
    cd = jnp.concatenate([one * jnp.ones((1, QK_NOPE), F32), jnp.cos(ar), jnp.cos(ar),
                          one * jnp.ones((1, LANES - QK_NOPE - QK_ROPE), F32)], axis=1)
    sd = jnp.concatenate([jnp.zeros((seq, QK_NOPE), F32), -jnp.sin(ar), jnp.sin(ar),
                          jnp.zeros((seq, LANES - QK_NOPE - QK_ROPE), F32)], axis=1)
    return jnp.stack([c64, s64, cd, sd])


def kernel(x, c, ctx, c_ctx, w_mod, b_mod, mix_norm, ffn_norm, w_in, a_qn, a_kn, a_sink, b_qn, b_kn, c_qn, c_kn, c_lq1, c_lk1, c_lq2, c_lk2, c_subln, d_q_norm, d_kv_norm, d_w_uq, d_w_ukv, d_qn_nope, d_kn_nope, d_qn_rope, d_kn_rope, w_br, w_out, ff_w_gate, ff_w_up, ff_w_down, moe_router, moe_w1, moe_w3, moe_w2):
    p = dict(mix_norm=mix_norm, ffn_norm=ffn_norm, w_in=w_in, a_qn=a_qn, a_kn=a_kn, a_sink=a_sink,
             b_qn=b_qn, b_kn=b_kn, c_qn=c_qn, c_kn=c_kn, c_lq1=c_lq1, c_lk1=c_lk1, c_lq2=c_lq2,
             c_lk2=c_lk2, c_subln=c_subln, d_q_norm=d_q_norm, d_kv_norm=d_kv_norm, d_w_uq=d_w_uq,
             d_w_ukv=d_w_ukv, d_qn_nope=d_qn_nope, d_kn_nope=d_kn_nope, d_qn_rope=d_qn_rope,
             d_kn_rope=d_kn_rope, w_br=w_br, w_out=w_out)
    nb, seq, _ = x.shape
    assert seq % TM == 0 and ctx.shape[1] == CTX_LEN and nb <= 15
    depth = w_mod.shape[0]
    lat = x.reshape(nb * seq, D_MODEL)
    cx = ctx.reshape(nb * CTX_LEN, D_MODEL)

    cond = jnp.zeros((16, D_MODEL), F32).at[:nb].set(c).at[nb].set(c_ctx)
    mod_all = _modulation(cond, w_mod, b_mod).reshape(depth, 16, 6, D_MODEL)
    rope_lat = _rope_tables(seq, True)
    rope_ctx = _rope_tables(CTX_LEN, False)
    mavg = _group_avg_mats()

    for l in range(depth):
        last = l == depth - 1
        lp = _prep_layer(l, p)
        mod_lat = mod_all[l, :nb]
        mod_ctx = jnp.broadcast_to(mod_all[l, nb:nb + 1], (nb, 6, D_MODEL))
        common = (lp["vec"], lp["gains"])
        wts = (lp["wuq"], lp["wukv"])
        if last:
            kv_ctx = _project(cx, mod_ctx, *common, rope_ctx, mavg,
                              lp["w_in"][:, Q_COLS:Q_COLS + KV_PAD_COLS], *wts, seq=CTX_LEN,
                              tm=CTX_LEN, with_q=False)
        else:
            outs = _project(cx, mod_ctx, *common, rope_ctx, mavg, lp["w_in"], *wts, seq=CTX_LEN,
                            tm=CTX_LEN, with_q=True)
            kv_ctx, q_ctx, g_ctx = outs[:8], outs[8:12], outs[12]
        outs = _project(lat, mod_lat, *common, rope_lat, mavg, lp["w_in"], *wts, seq=seq,
                        tm=TM, with_q=True)
        kv_lat, q_lat, g_lat = outs[:8], outs[8:12], outs[12]
        br_lat = _mixers(q_lat, [kv_lat, kv_ctx], lp, seq_q=seq, tag="lat")
        moe = l % 2 == 1
        i = l // 2
        router_t = None
        if moe:
            wr = moe_router[i].T
            wr_hi = wr.astype(BF16)
            router_t = jnp.stack([wr_hi, (wr - wr_hi.astype(F32)).astype(BF16)])
        res = _merge(br_lat, g_lat, lat, mod_lat, lp["ffn_norm"], lp["wbr"], lp["wout"], router_t, seq=seq)
        if not last:
            br_ctx = _mixers(q_ctx, [kv_ctx], lp, seq_q=CTX_LEN, tag="ctx")
            res_ctx = _merge(br_ctx, g_ctx, cx, mod_ctx, lp["ffn_norm"], lp["wbr"], lp["wout"],
                             router_t, seq=CTX_LEN)
        if not moe:
            wg, wu, wd = (ff_w_gate[i].astype(BF16), ff_w_up[i].astype(BF16), ff_w_down[i].astype(BF16))
            lat = _dense_ffn(res[1], res[0], mod_lat, wg, wu, wd, seq=seq)
            if not last:
                cx = _dense_ffn(res_ctx[1], res_ctx[0], mod_ctx, wg, wu, wd, seq=CTX_LEN)
        else:
            w1, w3, w2 = moe_w1[i].astype(BF16), moe_w3[i].astype(BF16), moe_w2[i].astype(BF16)

            def routed(res_t, mod_t, seq_t):
                xr, hr, lg = res_t
                dest, wts_r, meta, nblk_max = _route(lg)
                xbuf = _dispatch(hr, dest, meta, nblk_max)
                ybuf = _moe_ffn(xbuf, meta[0, :nblk_max], meta[1, :1], nblk_max, w1, w3, w2)
                return _combine(ybuf, dest, wts_r, xr, mod_t, seq=seq_t)

            lat = routed(res, mod_lat, seq)
            if not last:
                cx = routed(res_ctx, mod_ctx, CTX_LEN)
    return lat.reshape(nb, seq, D_MODEL)
```

```python
import functools
import math

import numpy as np
import jax
import jax.numpy as jnp
from jax import lax
from jax.experimental import pallas as pl
from jax.experimental.pallas import tpu as pltpu

F32 = jnp.float32
BF16 = jnp.bfloat16

D_MODEL = 1024
DEPTH = 2
CTX_LEN = 256
GRID_W = 64
HD = 64
BLOCK = 128
WINDOW = 128
ROPE_THETA = 10000.0
EPS = 1e-6
NEG_INF = -1e30
A_HEADS = 8
C_HEADS = 4
D_HEADS = 8
Q_LORA = 512
KV_LORA = 256
QK_NOPE = 64
QK_ROPE = 32
V_HEAD = 64
BRANCH_W = 512
D_FF = 2816
N_EXPERTS = 8
TOP_K = 2
EXPERT_FF = 3584

LANES = 128
VMEM_LIMIT = 56 * 1024 * 1024
LOG2E = math.log2(math.e)

Q_COLS = 2048
KV_PAD_COLS = 1920
GATE_COLS = 4 * D_MODEL
W_COLS = Q_COLS + KV_PAD_COLS + GATE_COLS
HEAD_PERM = (0, 4, 1, 5, 2, 6, 3, 7)

TM = 512
MERGE_TM = 512
MOE_TM = 512
MOE_TF = 1792
FFN_TM = 512
FFN_TF = 256


def _cparams(sem):
    return pltpu.CompilerParams(dimension_semantics=sem, vmem_limit_bytes=VMEM_LIMIT)


def _resident(shape):
    nd = len(shape)
    return pl.BlockSpec(shape, lambda *_: (0,) * nd, pipeline_mode=pl.Buffered(1))


def _sigmoid(x):
    return 1.0 / (1.0 + jnp.exp(-x))


def _dot(a, b):
    return jnp.dot(a, b, preferred_element_type=F32)


def _dot_nt(a, b):
    return lax.dot_general(a, b, (((1,), (1,)), ((), ())), preferred_element_type=F32)


def _mod_kernel(c_ref, w_ref, b_ref, o_ref):
    cond = c_ref[...]
    s = cond * _sigmoid(cond)
    o_ref[0] = jnp.dot(s, w_ref[0], precision=lax.Precision.HIGHEST,
                       preferred_element_type=F32) + b_ref[0]


def _modulation(cond, w_mod, b_mod):
    depth = w_mod.shape[0]
    nct = 6 * D_MODEL // 1024
    return pl.pallas_call(
        _mod_kernel,
        out_shape=jax.ShapeDtypeStruct((depth, 16, 6 * D_MODEL), F32),
        grid=(depth, nct),
        in_specs=[pl.BlockSpec((16, D_MODEL), lambda l, j: (0, 0)),
                  pl.BlockSpec((1, D_MODEL, 1024), lambda l, j: (l, 0, j)),
                  pl.BlockSpec((1, 1, 1024), lambda l, j: (l, 0, j))],
        out_specs=pl.BlockSpec((1, 16, 1024), lambda l, j: (l, 0, j)),
        compiler_params=_cparams(("arbitrary", "arbitrary")),
        name="modulation",
    )(cond, w_mod, b_mod.reshape(depth, 1, 6 * D_MODEL))


def _adaln(x, g, shift, scale):
    ms = jnp.mean(x * x, axis=-1, keepdims=True)
    y = x * lax.rsqrt(ms + EPS) * g
    return y * (1.0 + scale) + shift


def _group_mean_sq(r, mavg):
    sq = (r * r).astype(BF16)
    w = r.shape[1]
    parts = []
    for c in range(0, w, 2 * LANES):
        cw = min(2 * LANES, w - c)
        parts.append(_dot(sq[:, c:c + cw], mavg[:cw, :cw]))
    return parts[0] if len(parts) == 1 else jnp.concatenate(parts, axis=1)


def _rope_slab(y, cos, sin, half):
    lane = lax.broadcasted_iota(jnp.int32, y.shape, 1)
    first = (lane % (2 * half)) < half
    sw = jnp.where(first, pltpu.roll(y, LANES - half, 1), pltpu.roll(y, half, 1))
    return y * cos + sw * sin


def _norm_rope(r, ms, gain, cos, sin, half):
    outs = []
    for c in range(0, r.shape[1], LANES):
        y = r[:, c:c + LANES] * lax.rsqrt(ms[:, c:c + LANES] + EPS) * gain
        if cos is not None:
            y = _rope_slab(y, cos, sin, half)
        outs.append(y)
    return outs[0] if len(outs) == 1 else jnp.concatenate(outs, axis=1)


def _proj_kernel(with_q, x_ref, mod_ref, vec_ref, gains_ref, rope_ref, mavg_ref,
                 w_ref, wuq_ref, wukv_ref, *refs):
    if with_q:
        (qa_ref, qb_ref, qc_ref, qd_ref, g_ref) = refs[-5:]
        kv_refs = refs[-13:-5]
        kvo, go = Q_COLS, Q_COLS + KV_PAD_COLS
    else:
        kv_refs = refs[-8:]
        kvo = 0
    ka_ref, va_ref, kb_ref, vb_ref, kc_ref, vc_ref, kd_ref, vd_ref = kv_refs

    m = mod_ref[0]
    hb = _adaln(x_ref[...], vec_ref[0:1, :], m[0:1], m[1:2]).astype(BF16)

    def mm(lo, hi):
        return _dot(hb, w_ref[:, lo:hi])

    m64 = mavg_ref[0]
    md = mavg_ref[1]
    c64, s64, cd, sd = rope_ref[0], rope_ref[1], rope_ref[2], rope_ref[3]
    gains = gains_ref[...]

    def gain(i):
        return gains[i:i + 1, :]

    def gates(n):
        gl = mm(go + n * D_MODEL, go + (n + 1) * D_MODEL)
        g_ref[:, n * D_MODEL:(n + 1) * D_MODEL] = _sigmoid(gl).astype(BF16)

    def full_rms(r, g):
        return (r * lax.rsqrt(jnp.mean(r * r, axis=-1, keepdims=True) + EPS) * g).astype(BF16)

    if with_q:
        r_qa, r_qb, r_qc, cq = mm(0, 512), mm(512, 1024), mm(1024, 1536), mm(1536, 2048)
    kvab = mm(kvo, kvo + 512)
    r_kc = mm(kvo + 512, kvo + 1024)
    r_vc = mm(kvo + 1024, kvo + 1536)
    ckv = mm(kvo + 1536, kvo + 1792)
    r_kr = mm(kvo + 1792, kvo + 1920)
    if with_q:
        gates(0)
        ms_qa, ms_qb, ms_qc = (_group_mean_sq(r, m64) for r in (r_qa, r_qb, r_qc))
        qd = _dot(full_rms(cq, vec_ref[1:2, 0:Q_LORA]), wuq_ref[...])
    r_ka, r_kb = kvab[:, 0:128], kvab[:, 256:384]
    ms_ka, ms_kb, ms_kc = (_group_mean_sq(r, m64) for r in (r_ka, r_kb, r_kc))
    ms_kr = _group_mean_sq(r_kr, md)
    kvd = _dot(full_rms(ckv, vec_ref[1:2, Q_LORA:Q_LORA + KV_LORA]), wukv_ref[...])
    if with_q:
        gates(1)
        qa_ref[...] = _norm_rope(r_qa, ms_qa, gain(0), c64, s64, 32).astype(BF16)
        qb_ref[...] = _norm_rope(r_qb, ms_qb, gain(1), c64, s64, 32).astype(BF16)
        qc_ref[...] = _norm_rope(r_qc, ms_qc, gain(2), c64, s64, 32).astype(BF16)
        ms_qd = _group_mean_sq(qd, md)
    r_kn = kvd[:, 0:D_HEADS * LANES]
    ms_kn = _group_mean_sq(r_kn, md)
    if with_q:
        gates(2)
    ka_ref[...] = _norm_rope(r_ka, ms_ka, gain(3), c64, s64, 32).astype(BF16)
    va_ref[...] = kvab[:, 128:256].astype(BF16)
    kb_ref[...] = _norm_rope(r_kb, ms_kb, gain(4), c64, s64, 32).astype(BF16)
    vb_ref[...] = kvab[:, 384:512].astype(BF16)
    kc_ref[...] = _norm_rope(r_kc, ms_kc, gain(5), c64, s64, 32).astype(BF16)
    vc_ref[...] = r_vc.astype(BF16)
    if with_q:
        qd_ref[...] = _norm_rope(qd, ms_qd, gain(6), cd, sd, 16).astype(BF16)
        gates(3)
    kr = _norm_rope(r_kr, ms_kr, gain(8), cd, sd, 16)
    kn = _norm_rope(r_kn, ms_kn, gain(7), None, None, 0)
    kd_ref[...] = (kn + jnp.concatenate([kr] * D_HEADS, axis=1)).astype(BF16)
    vd_ref[...] = kvd[:, D_HEADS * LANES:].astype(BF16)


_KV_WIDTHS = (128, 128, 128, 128, 512, 512, D_HEADS * LANES, D_HEADS * V_HEAD)


def _project(x, mod, vec, gains, rope, mavg, w, wuq, wukv, *, seq, tm, with_q):
    n = x.shape[0]
    tps = seq // tm
    grid = (n // tm,)

    def tok(i):
        return (i, 0)

    in_specs = [
        pl.BlockSpec((tm, D_MODEL), tok),
        pl.BlockSpec((1, 6, D_MODEL), lambda i: (i // tps, 0, 0)),
        _resident(vec.shape),
        _resident(gains.shape),
        pl.BlockSpec((4, tm, LANES), lambda i: (0, i % tps, 0)),
        _resident(mavg.shape),
        _resident(w.shape),
        _resident(wuq.shape),
        _resident(wukv.shape),
    ]
    args = [x, mod, vec, gains, rope, mavg, w, wuq, wukv]
    out_shapes = [jax.ShapeDtypeStruct((n, wd), BF16) for wd in _KV_WIDTHS]
    out_specs = [pl.BlockSpec((tm, wd), tok) for wd in _KV_WIDTHS]
    if with_q:
        for wd in (512, 512, 512, D_HEADS * LANES, GATE_COLS):
            out_shapes.append(jax.ShapeDtypeStruct((n, wd), BF16))
            out_specs.append(pl.BlockSpec((tm, wd), tok))
    return pl.pallas_call(
        functools.partial(_proj_kernel, with_q),
        out_shape=out_shapes, grid=grid, in_specs=in_specs, out_specs=out_specs,
        compiler_params=_cparams(("arbitrary",)),
        name="project_q" if with_q else "project_kv",
    )(*args)


LOGIT_SAFE = 40.0


def _scores(q, kv, mask=None):
    scores = [_dot_nt(q, k) for k, _ in kv]
    if mask is not None:
        scores[0] = jnp.where(mask, scores[0], NEG_INF)
    return scores


def _attend(shift, scores, kv, extra_logit=None):
    if shift:
        m = functools.reduce(jnp.maximum, [jnp.max(s, axis=-1, keepdims=True) for s in scores])
        if extra_logit is not None:
            m = jnp.maximum(m, extra_logit)
            extra_logit = extra_logit - m
        scores = [s - m for s in scores]
    probs = [jnp.exp2(s) for s in scores]
    l = functools.reduce(jnp.add, [jnp.sum(p, axis=-1, keepdims=True) for p in probs])
    if extra_logit is not None:
        l = l + jnp.exp2(extra_logit)
    o = functools.reduce(jnp.add, [_dot(p.astype(BF16), v) for p, (_, v) in zip(probs, kv)])
    return o / l


def _run_skewed(units):
    pending = None
    for start, finish in units:
        state = start()
        if pending is not None:
            pending[0](pending[1])
        pending = (finish, state)
    pending[0](pending[1])


def _guarded(body, idx, safe_ref, *refs):
    @pl.when(safe_ref[idx] != 0)
    def _():
        body(False, *refs)

    @pl.when(safe_ref[idx] == 0)
    def _():
        body(True, *refs)


def _split_heads(qs):
    lane = lax.broadcasted_iota(jnp.int32, qs.shape, 1)
    zero = jnp.zeros_like(qs)
    return jnp.concatenate([jnp.where(lane < HD, qs, zero), jnp.where(lane >= HD, qs, zero)], axis=0)


def _merge_halves(o, tq):
    lane = lax.broadcasted_iota(jnp.int32, (tq, LANES), 1)
    return jnp.where(lane < HD, o[:tq], o[tq:])


def _kv_cols(kv_refs, kc, vc):
    return [(k_ref[:, kc * LANES:(kc + 1) * LANES], v_ref[:, vc * LANES:(vc + 1) * LANES])
            for k_ref, v_ref in kv_refs]


def _one_group(groups):
    return [(jnp.concatenate([k for k, _ in groups], axis=0), jnp.concatenate([v for _, v in groups], axis=0))]


def _mix_b(shift, q_ref, kv_refs, o_ref):
    tq = q_ref.shape[0]
    kv = _kv_cols(kv_refs, 0, 0)

    def start(j):
        return _scores(_split_heads(q_ref[:, j * LANES:(j + 1) * LANES]), kv)

    def finish(j, scores):
        o = _attend(shift, scores, kv)
        o_ref[:, j * LANES:(j + 1) * LANES] = _merge_halves(o, tq).astype(BF16)

    return [(functools.partial(start, j), functools.partial(finish, j)) for j in range(4)]


def _mix_a(shift, sink_ref, q_ref, kv_refs, o_ref):
    tq = q_ref.shape[0]
    if len(kv_refs) == 2:
        (kl_ref, vl_ref), (kc_ref, vc_ref) = kv_refs
        span = tq + 2 * WINDOW
        q0 = pl.program_id(1) * tq
        start = pl.multiple_of(jnp.clip(q0 - WINDOW, 0, kl_ref.shape[0] - span), BLOCK)
        kv = [(kl_ref[pl.ds(start, span), :], vl_ref[pl.ds(start, span), :]), (kc_ref[...], vc_ref[...])]
        row = lax.broadcasted_iota(jnp.int32, (2 * tq, span), 0)
        col = lax.broadcasted_iota(jnp.int32, (2 * tq, span), 1)
        qpos = q0 + jnp.where(row >= tq, row - tq, row)
        dist = (start + col) - qpos
        valid = (dist >= -WINDOW) & (dist <= WINDOW)
    else:
        kv = _kv_cols(kv_refs, 0, 0)
        valid = None
    rows = lax.broadcasted_iota(jnp.int32, (2 * tq, 1), 0)

    def start(j):
        return _scores(_split_heads(q_ref[:, j * LANES:(j + 1) * LANES]), kv, valid)

    def finish(j, scores):
        sink = jnp.where(rows < tq, sink_ref[2 * j], sink_ref[2 * j + 1]) * LOG2E
        o = _attend(shift, scores, kv, extra_logit=sink)
        o_ref[:, j * LANES:(j + 1) * LANES] = _merge_halves(o, tq).astype(BF16)

    return [(functools.partial(start, j), functools.partial(finish, j)) for j in range(4)]


def _mix_c(lam_init, shift, lq1_ref, lk1_ref, lq2_ref, lk2_ref, subln_ref, q_ref, kv_refs, o_ref):
    tq = q_ref.shape[0]
    lam = (jnp.exp(jnp.sum(lq1_ref[...] * lk1_ref[...], axis=-1, keepdims=True))
           - jnp.exp(jnp.sum(lq2_ref[...] * lk2_ref[...], axis=-1, keepdims=True)) + lam_init)

    def start(c):
        kv = _one_group(_kv_cols(kv_refs, c, c))
        return kv, _scores(_split_heads(q_ref[:, c * LANES:(c + 1) * LANES]), kv)

    def finish(c, state):
        kv, scores = state
        o = _attend(shift, scores, kv)
        oc = o[:tq] - lam * o[tq:]
        oc = oc * lax.rsqrt(jnp.mean(oc * oc, axis=-1, keepdims=True) + EPS) * subln_ref[...]
        o_ref[:, c * LANES:(c + 1) * LANES] = (oc * (1.0 - lam_init)).astype(BF16)

    _run_skewed([(functools.partial(start, c), functools.partial(finish, c)) for c in range(C_HEADS)])


def _mix_d(shift, q_ref, kv_refs, o_ref):
    tq = q_ref.shape[0]

    def start(h):
        return _scores(q_ref[:, h * LANES:(h + 1) * LANES], _kv_cols(kv_refs, h, h // 2))

    done = {}

    def finish(h, scores):
        done[h] = _attend(shift, scores, _kv_cols(kv_refs, h, h // 2))
        if h % 2 == 1:
            o = jnp.concatenate([done.pop(h - 1), done.pop(h)], axis=0)
            o_ref[:, (h // 2) * LANES:(h // 2 + 1) * LANES] = _merge_halves(o, tq).astype(BF16)

    _run_skewed([(functools.partial(start, h), functools.partial(finish, h)) for h in range(D_HEADS)])


def _mix_ab(shift, sink_ref, qa_ref, kva_refs, oa_ref, qb_ref, kvb_refs, ob_ref):
    units = []
    for unit_b, unit_a in zip(_mix_b(shift, qb_ref, kvb_refs, ob_ref),
                              _mix_a(shift, sink_ref, qa_ref, kva_refs, oa_ref)):
        units += [unit_b, unit_a]
    _run_skewed(units)


ATTN_TQ = 256


def _attention(body, safe_idx, safe, extra, extra_specs, units, *, seq_q, name):
    n = units[0][0].shape[0]
    nb = n // seq_q
    tq = ATTN_TQ
    nq = seq_q // tq
    tok = lambda b, i: (b * nq + i, 0)
    per_batch = lambda a: pl.BlockSpec((a.shape[0] // nb, a.shape[1]), lambda b, i: (b, 0))
    args, in_specs, counts = [], [], []
    for q, kv in units:
        args.append(q)
        in_specs.append(pl.BlockSpec((tq, q.shape[1]), tok))
        for pair in kv:
            args.extend(pair)
            in_specs.extend(per_batch(a) for a in pair)
        counts.append(len(kv))
    n_extra = len(extra)

    def kernel(shift, *refs):
        ins, outs = refs[n_extra:len(refs) - len(units)], refs[len(refs) - len(units):]
        packed, pos = [], 0
        for u, nkv in enumerate(counts):
            kv_refs = [(ins[pos + 1 + 2 * g], ins[pos + 2 + 2 * g]) for g in range(nkv)]
            packed.extend([ins[pos], kv_refs, outs[u]])
            pos += 1 + 2 * nkv
        body(shift, *refs[:n_extra], *packed)

    out = pl.pallas_call(
        functools.partial(_guarded, kernel, safe_idx),
        out_shape=[jax.ShapeDtypeStruct((n, BRANCH_W), BF16)] * len(units),
        grid=(nb, nq),
        in_specs=[pl.BlockSpec(memory_space=pltpu.SMEM)] + list(extra_specs) + in_specs,
        out_specs=[pl.BlockSpec((tq, BRANCH_W), tok)] * len(units),
        compiler_params=_cparams(("arbitrary", "arbitrary")),
        name=name,
    )(safe, *extra, *args)
    return out


def _mixers(qs, kv_groups, lp, *, seq_q, tag):
    qa, qb, qc, qd = qs
    pairs = lambda j: [(g[2 * j], g[2 * j + 1]) for g in kv_groups]
    safe = lp["safe"]
    row = lambda w: pl.BlockSpec((1, w), lambda b, i: (0, 0))
    oa, ob = _attention(_mix_ab, 0, safe, [lp["sink"]], [pl.BlockSpec(memory_space=pltpu.SMEM)],
                        [(qa, pairs(0)), (qb, pairs(1))], seq_q=seq_q, name="attn_ab_" + tag)
    oc, = _attention(functools.partial(_mix_c, lp["lam_init"]), 1, safe,
                     [lp["lq1"], lp["lk1"], lp["lq2"], lp["lk2"], lp["subln"]],
                     [row(HD)] * 4 + [row(2 * HD)], [(qc, pairs(2))], seq_q=seq_q, name="attn_c_" + tag)
    od, = _attention(_mix_d, 2, safe, [], [], [(qd, pairs(3))], seq_q=seq_q, name="attn_d_" + tag)
    return oa, ob, oc, od


def _merge_kernel(moe, oa_ref, ob_ref, oc_ref, od_ref, g_ref, x_ref, mod_ref, norm_ref,
                  wbr_ref, wout_ref, *refs):
    if moe:
        wr_ref, xo_ref, h_ref, lg_ref = refs
    else:
        xo_ref, h_ref = refs
    m = mod_ref[0]
    y = None
    for n, o_ref in enumerate((oa_ref, ob_ref, oc_ref, od_ref)):
        yn = g_ref[:, n * D_MODEL:(n + 1) * D_MODEL].astype(F32) * _dot(o_ref[...], wbr_ref[n])
        y = yn if y is None else y + yn
    z = _dot(y.astype(BF16), wout_ref[...])
    xn = x_ref[...] + m[2:3] * z
    xo_ref[...] = xn
    h = _adaln(xn, norm_ref[...], m[3:4], m[4:5])
    if moe:
        h_ref[...] = h
        h_hi = h.astype(BF16)
        h_lo = (h - h_hi.astype(F32)).astype(BF16)
        lg_ref[...] = (_dot_nt(wr_ref[0], h_hi) + _dot_nt(wr_ref[1], h_hi)) + _dot_nt(wr_ref[0], h_lo)
    else:
        h_ref[...] = h.astype(BF16)


def _merge(branches, g, x, mod, norm, wbr, wout, router_t, *, seq):
    n = x.shape[0]
    tm = min(MERGE_TM, seq)
    tps = seq // tm
    moe = router_t is not None
    tok = lambda i: (i, 0)
    in_specs = [pl.BlockSpec((tm, BRANCH_W), tok)] * 4 + [
        pl.BlockSpec((tm, GATE_COLS), tok),
        pl.BlockSpec((tm, D_MODEL), tok),
        pl.BlockSpec((1, 6, D_MODEL), lambda i: (i // tps, 0, 0)),
        _resident(norm.shape), _resident(wbr.shape), _resident(wout.shape)]
    args = list(branches) + [g, x, mod, norm, wbr, wout]
    out_shape = [jax.ShapeDtypeStruct((n, D_MODEL), F32),
                 jax.ShapeDtypeStruct((n, D_MODEL), F32 if moe else BF16)]
    out_specs = [pl.BlockSpec((tm, D_MODEL), tok), pl.BlockSpec((tm, D_MODEL), tok)]
    if moe:
        in_specs.append(_resident(router_t.shape))
        args.append(router_t)
        out_shape.append(jax.ShapeDtypeStruct((N_EXPERTS, n), F32))
        out_specs.append(pl.BlockSpec((N_EXPERTS, tm), lambda i: (0, i)))
    return pl.pallas_call(
        functools.partial(_merge_kernel, moe),
        out_shape=out_shape, grid=(n // tm,), in_specs=in_specs, out_specs=out_specs,
        compiler_params=_cparams(("arbitrary",)),
        name="merge_moe" if moe else "merge_dense",
    )(*args)


def _ffn_kernel(h_ref, x_ref, mod_ref, wg_ref, wu_ref, wd_ref, o_ref, hid_ref):
    h = h_ref[...]
    for c in range(0, D_FF, FFN_TF):
        a = _dot(h, wg_ref[:, c:c + FFN_TF])
        u = _dot(h, wu_ref[:, c:c + FFN_TF])
        hid_ref[:, c:c + FFN_TF] = (a * _sigmoid(a) * u).astype(BF16)
    m = mod_ref[0]
    o_ref[...] = x_ref[...] + m[5:6] * _dot(hid_ref[...], wd_ref[...])


def _dense_ffn(h, x, mod, wg, wu, wd, *, seq):
    n = x.shape[0]
    tm = min(FFN_TM, seq)
    tps = seq // tm
    tok = lambda i: (i, 0)
    return pl.pallas_call(
        _ffn_kernel,
        out_shape=jax.ShapeDtypeStruct((n, D_MODEL), F32),
        grid=(n // tm,),
        in_specs=[pl.BlockSpec((tm, D_MODEL), tok), pl.BlockSpec((tm, D_MODEL), tok),
                  pl.BlockSpec((1, 6, D_MODEL), lambda i: (i // tps, 0, 0)),
                  _resident(wg.shape), _resident(wu.shape), _resident(wd.shape)],
        out_specs=pl.BlockSpec((tm, D_MODEL), tok),
        scratch_shapes=[pltpu.VMEM((tm, D_FF), BF16)],
        compiler_params=_cparams(("arbitrary",)),
        name="dense_ffn",
    )(h, x, mod, wg, wu, wd)


ROUTE_CHUNK = 512
DMA_UNROLL = 8


def _route_kernel(n, lg_ref, tri_ref, dest_ref, wts_ref, meta_ref, mem_ref, pos_ref):
    lg = lg_ref[...]
    eidx = lax.broadcasted_iota(jnp.int32, lg.shape, 0).astype(F32)
    none = float(N_EXPERTS)
    m1 = jnp.max(lg, axis=0, keepdims=True)
    i1 = jnp.min(jnp.where(lg == m1, eidx, none), axis=0, keepdims=True)
    lg2 = jnp.where(eidx == i1, -jnp.inf, lg)
    m2 = jnp.max(lg2, axis=0, keepdims=True)
    i2 = jnp.min(jnp.where(lg2 == m2, eidx, none), axis=0, keepdims=True)
    e = jnp.exp(m2 - m1)
    w1 = 1.0 / (1.0 + e)
    wts_ref[0:1, :] = w1
    wts_ref[1:2, :] = e * w1
    mem_ref[...] = jnp.where(eidx == i1, 1.0, 0.0) + jnp.where(eidx == i2, 1.0, 0.0)

    carry = jnp.zeros((N_EXPERTS, 1), F32)
    tri = tri_ref[...]
    for c in range(0, n, ROUTE_CHUNK):
        mc = mem_ref[:, c:c + ROUTE_CHUNK]
        pos_ref[:, c:c + ROUTE_CHUNK] = carry + _dot(mc.astype(BF16), tri)
        carry = carry + jnp.sum(mc, axis=1, keepdims=True)
    padded = jnp.floor((carry + (MOE_TM - 1.0)) * (1.0 / MOE_TM)) * MOE_TM
    esub = lax.broadcasted_iota(jnp.int32, (N_EXPERTS, 1), 0)
    pstart = jnp.zeros((N_EXPERTS, 1), F32)
    for k in range(N_EXPERTS - 1):
        pstart = pstart + jnp.where(esub > k, padded[k:k + 1, :], 0.0)
    pend = pstart + padded
    slot = pstart + pos_ref[...]
    dest_ref[0:1, :] = jnp.sum(jnp.where(eidx == i1, slot, 0.0), axis=0, keepdims=True).astype(jnp.int32)
    dest_ref[1:2, :] = jnp.sum(jnp.where(eidx == i2, slot, 0.0), axis=0, keepdims=True).astype(jnp.int32)

    nblk = pend[N_EXPERTS - 1:N_EXPERTS, :] * (1.0 / MOE_TM)
    brow = jnp.minimum(lax.broadcasted_iota(jnp.int32, (1, LANES), 1).astype(F32), nblk - 1.0) * MOE_TM
    be = jnp.zeros((1, LANES), F32)
    for k in range(N_EXPERTS):
        be = be + jnp.where(pend[k:k + 1, :] <= brow, 1.0, 0.0)
    lane = lax.broadcasted_iota(jnp.int32, (1, LANES), 1)
    last_blk = jnp.zeros((1, LANES), F32)
    used = jnp.zeros((1, LANES), F32)
    for k in range(N_EXPERTS):
        last_blk = jnp.where(lane == k, pend[k:k + 1, :] * (1.0 / MOE_TM) - 1.0, last_blk)
        used = jnp.where(lane == k, jnp.where(padded[k:k + 1, :] > 0.0, 1.0, 0.0), used)
    meta_ref[...] = jnp.zeros(meta_ref.shape, jnp.int32)
    meta_ref[0:1, :] = jnp.minimum(be, N_EXPERTS - 1.0).astype(jnp.int32)
    meta_ref[1:2, :] = jnp.broadcast_to(nblk, (1, LANES)).astype(jnp.int32)
    meta_ref[2:3, :] = last_blk.astype(jnp.int32)
    meta_ref[3:4, :] = used.astype(jnp.int32)


def _route(logits_t):
    n = logits_t.shape[1]
    nblk_max = n * TOP_K // MOE_TM + N_EXPERTS
    assert nblk_max <= LANES and n % ROUTE_CHUNK == 0
    tri = jnp.asarray(np.triu(np.ones((ROUTE_CHUNK, ROUTE_CHUNK)), 1), BF16)
    whole = lambda shape: pl.BlockSpec(shape, lambda: (0,) * len(shape))
    dest, wts, meta = pl.pallas_call(
        functools.partial(_route_kernel, n),
        out_shape=[jax.ShapeDtypeStruct((TOP_K, n), jnp.int32),
                   jax.ShapeDtypeStruct((TOP_K, n), F32),
                   jax.ShapeDtypeStruct((8, LANES), jnp.int32)],
        in_specs=[whole(logits_t.shape), whole(tri.shape)],
        out_specs=[whole((TOP_K, n)), whole((TOP_K, n)), whole((8, LANES))],
        scratch_shapes=[pltpu.VMEM((N_EXPERTS, n), F32), pltpu.VMEM((N_EXPERTS, n), F32)],
        compiler_params=pltpu.CompilerParams(vmem_limit_bytes=VMEM_LIMIT),
        name="moe_route",
    )(logits_t, tri)
    return dest, wts, meta, nblk_max


def _dispatch_kernel(tg, nblk_max, dest_ref, meta_ref, h_ref, xbuf_hbm, sem, zero_ref, zero_sem):
    @pl.when(pl.program_id(0) == 0)
    def _():
        zero_ref[...] = jnp.zeros_like(zero_ref)

        def fill(blk):
            return pltpu.make_async_copy(
                zero_ref, xbuf_hbm.at[pl.ds(pl.multiple_of(blk * MOE_TM, MOE_TM), MOE_TM)], zero_sem)

        def each_fill(act):
            for e in range(N_EXPERTS):
                @pl.when(meta_ref[3, e] != 0)
                def _():
                    act(fill(meta_ref[2, e]))

            def tail(b, carry):
                act(fill(b))
                return carry

            lax.fori_loop(meta_ref[1, 0], nblk_max, tail, 0)

        each_fill(lambda cp: cp.start())
        each_fill(lambda cp: cp.wait())

    def copies(r):
        src = h_ref.at[pl.ds(r, 1)]
        return (pltpu.make_async_copy(src, xbuf_hbm.at[pl.ds(dest_ref[0, 0, r], 1)], sem),
                pltpu.make_async_copy(src, xbuf_hbm.at[pl.ds(dest_ref[0, 0, tg + r], 1)], sem))

    def start(r, carry):
        for k, cp in enumerate(copies(r)):
            cp.start(priority=k)
        return carry

    def wait(r, carry):
        for cp in copies(r):
            cp.wait()
        return carry

    lax.fori_loop(0, tg, start, 0, unroll=DMA_UNROLL)
    lax.fori_loop(0, tg, wait, 0, unroll=DMA_UNROLL)


def _tile_dest(dest, t):
    n = dest.shape[1]
    return jnp.transpose(dest.reshape(TOP_K, n // t, t), (1, 0, 2)).reshape(n // t, 1, TOP_K * t)


def _dispatch(h, dest, meta, nblk_max):
    n = h.shape[0]
    tg = min(1024, n)
    nt = n // tg
    dest_t = _tile_dest(dest, tg)
    return pl.pallas_call(
        functools.partial(_dispatch_kernel, tg, nblk_max),
        out_shape=jax.ShapeDtypeStruct((nblk_max * MOE_TM, D_MODEL), F32),
        grid=(nt,),
        in_specs=[pl.BlockSpec((1, 1, TOP_K * tg), lambda i: (i, 0, 0), memory_space=pltpu.SMEM),
                  pl.BlockSpec(memory_space=pltpu.SMEM),
                  pl.BlockSpec((tg, D_MODEL), lambda i: (i, 0))],
        out_specs=pl.BlockSpec(memory_space=pl.ANY),
        scratch_shapes=[pltpu.SemaphoreType.DMA(()), pltpu.VMEM((MOE_TM, D_MODEL), F32),
                        pltpu.SemaphoreType.DMA(())],
        compiler_params=_cparams(("arbitrary",)),
        name="moe_dispatch",
    )(dest_t, meta, h)


def _moe_kernel(nf, blk_e_ref, nblk_ref, x_ref, w1_ref, w3_ref, w2_ref, o_ref, xb_ref, hid_ref):
    del blk_e_ref
    i = pl.program_id(0)
    j = pl.program_id(1)

    @pl.when(i < nblk_ref[0])
    def _():
        @pl.when(j == 0)
        def _():
            xb_ref[...] = x_ref[...].astype(BF16)

        xb = xb_ref[...]
        for c in range(0, MOE_TF, FFN_TF):
            a = _dot(xb, w1_ref[0, :, c:c + FFN_TF])
            u = _dot(xb, w3_ref[0, :, c:c + FFN_TF])
            hid_ref[:, c:c + FFN_TF] = (a * _sigmoid(a) * u).astype(BF16)
        y = _dot(hid_ref[...], w2_ref[0])

        @pl.when(j == 0)
        def _():
            o_ref[...] = y

        @pl.when(j > 0)
        def _():
            o_ref[...] += y

    @pl.when((i >= nblk_ref[0]) & (j == nf - 1))
    def _():
        o_ref[...] = jnp.zeros_like(o_ref)


def _moe_ffn(xbuf, blk_e, nblk, nblk_max, w1, w3, w2):
    nf = EXPERT_FF // MOE_TF

    def rows(i, j, be, nb):
        return (i, 0)

    def jj(i, j, nb):
        return jnp.where(i < nb[0], j, nf - 1)

    grid_spec = pltpu.PrefetchScalarGridSpec(
        num_scalar_prefetch=2, grid=(nblk_max, nf),
        in_specs=[pl.BlockSpec((MOE_TM, D_MODEL), rows),
                  pl.BlockSpec((1, D_MODEL, MOE_TF), lambda i, j, be, nb: (be[i], 0, jj(i, j, nb))),
                  pl.BlockSpec((1, D_MODEL, MOE_TF), lambda i, j, be, nb: (be[i], 0, jj(i, j, nb))),
                  pl.BlockSpec((1, MOE_TF, D_MODEL), lambda i, j, be, nb: (be[i], jj(i, j, nb), 0))],
        out_specs=pl.BlockSpec((MOE_TM, D_MODEL), rows),
        scratch_shapes=[pltpu.VMEM((MOE_TM, D_MODEL), BF16), pltpu.VMEM((MOE_TM, MOE_TF), BF16)])
    return pl.pallas_call(
        functools.partial(_moe_kernel, nf),
        out_shape=jax.ShapeDtypeStruct(xbuf.shape, F32),
        grid_spec=grid_spec,
        compiler_params=_cparams(("arbitrary", "arbitrary")),
        name="moe_experts",
    )(blk_e, nblk, xbuf, w1, w3, w2)


def _combine_kernel(tc, nt, dest_ref, dest_next_ref, wts_ref, x_ref, mod_ref, y_hbm, o_ref, yg_ref, sems):
    i = pl.program_id(0)
    slot = lax.rem(i, 2)

    def copies(d_ref, s, r):
        return (pltpu.make_async_copy(y_hbm.at[pl.ds(d_ref[0, 0, r], 1)],
                                      yg_ref.at[s, 0, pl.ds(r, 1)], sems.at[s]),
                pltpu.make_async_copy(y_hbm.at[pl.ds(d_ref[0, 0, tc + r], 1)],
                                      yg_ref.at[s, 1, pl.ds(r, 1)], sems.at[s]))

    def request(d_ref, s):
        def start(r, carry):
            for k, cp in enumerate(copies(d_ref, s, r)):
                cp.start(priority=k)
            return carry
        lax.fori_loop(0, tc, start, 0, unroll=DMA_UNROLL)

    @pl.when(i == 0)
    def _():
        request(dest_ref, slot)

    @pl.when(i + 1 < nt)
    def _():
        request(dest_next_ref, 1 - slot)

    def wait(r, carry):
        for cp in copies(dest_ref, slot, r):
            cp.wait()
        return carry

    lax.fori_loop(0, tc, wait, 0, unroll=DMA_UNROLL)
    m = mod_ref[0]
    w = wts_ref[...]
    y = w[:, 0:1] * yg_ref[slot, 0] + w[:, 1:2] * yg_ref[slot, 1]
    o_ref[...] = x_ref[...] + m[5:6] * y


def _combine(ybuf, dest, wts, x, mod, *, seq):
    n = x.shape[0]
    tc = min(512, seq)
    nt = n // tc
    tps = seq // tc
    dest_t = _tile_dest(dest, tc)
    wts = wts.T
    tok = lambda i: (i, 0)
    slots = lambda step: pl.BlockSpec((1, 1, TOP_K * tc), lambda i: (jnp.minimum(i + step, nt - 1), 0, 0),
                                      memory_space=pltpu.SMEM)
    return pl.pallas_call(
        functools.partial(_combine_kernel, tc, nt),
        out_shape=jax.ShapeDtypeStruct((n, D_MODEL), F32),
        grid=(nt,),
        in_specs=[slots(0), slots(1),
                  pl.BlockSpec((tc, TOP_K), tok),
                  pl.BlockSpec((tc, D_MODEL), tok),
                  pl.BlockSpec((1, 6, D_MODEL), lambda i: (i // tps, 0, 0)),
                  pl.BlockSpec(memory_space=pl.ANY)],
        out_specs=pl.BlockSpec((tc, D_MODEL), tok),
        scratch_shapes=[pltpu.VMEM((2, TOP_K, tc, D_MODEL), F32), pltpu.SemaphoreType.DMA((2,))],
        compiler_params=_cparams(("arbitrary",)),
        name="moe_combine",
    )(dest_t, dest_t, wts, x, mod, ybuf)


def _head_index():
    return np.concatenate([np.arange(h * HD, (h + 1) * HD) for h in HEAD_PERM])


def _prep_w_in(w):
    hp = _head_index()
    kv_lo = Q_COLS
    kr_lo = kv_lo + KV_PAD_COLS - LANES
    parts = [w[:, 0:512][:, hp], w[:, 512:1024][:, hp], w[:, 1024:Q_COLS],
             w[:, kv_lo:kr_lo],
             jnp.pad(w[:, kr_lo:kr_lo + QK_ROPE], ((0, 0), (QK_NOPE, LANES - QK_NOPE - QK_ROPE))),
             w[:, kr_lo + QK_ROPE:]]
    return jnp.concatenate(parts, axis=1).astype(BF16)


def _prep_layer(l, p):
    zeros = lambda k: jnp.zeros((k,), F32)
    sc = LOG2E / math.sqrt(HD)
    sc_d = LOG2E / math.sqrt(QK_NOPE + QK_ROPE)
    wuq = p["d_w_uq"][l].reshape(Q_LORA, D_HEADS, QK_NOPE + QK_ROPE)
    wuq = jnp.pad(wuq, ((0, 0), (0, 0), (0, LANES - QK_NOPE - QK_ROPE))).reshape(Q_LORA, D_HEADS * LANES)
    wukv = p["d_w_ukv"][l].reshape(KV_LORA, D_HEADS, QK_NOPE + V_HEAD)
    wuk = jnp.pad(wukv[:, :, :QK_NOPE], ((0, 0), (0, 0), (0, LANES - QK_NOPE))).reshape(KV_LORA, D_HEADS * LANES)
    wuv = wukv[:, :, QK_NOPE:].reshape(KV_LORA, D_HEADS * V_HEAD)
    gains = jnp.stack([
        jnp.tile(p["a_qn"][l], 2) * sc, jnp.tile(p["b_qn"][l], 2) * sc, jnp.tile(p["c_qn"][l], 2) * sc,
        jnp.tile(p["a_kn"][l], 2), jnp.tile(p["b_kn"][l], 2), jnp.tile(p["c_kn"][l], 2),
        jnp.concatenate([p["d_qn_nope"][l], p["d_qn_rope"][l], zeros(32)]) * sc_d,
        jnp.concatenate([p["d_kn_nope"][l], zeros(64)]),
        jnp.concatenate([zeros(64), p["d_kn_rope"][l], zeros(32)]),
    ] + [zeros(LANES)] * 7).astype(F32)
    vec = jnp.stack([p["mix_norm"][l],
                     jnp.concatenate([p["d_q_norm"][l], p["d_kv_norm"][l], zeros(D_MODEL - Q_LORA - KV_LORA)])]
                    + [zeros(D_MODEL)] * 6).astype(F32)
    wbr = p["w_br"][l]
    hp = _head_index()
    wbr = jnp.stack([wbr[0][hp], wbr[1][hp], wbr[2], wbr[3]]).astype(BF16)

    amax = lambda v: jnp.max(jnp.abs(v))
    bound64 = lambda gq, gk: 1.02 * HD * amax(gq) * amax(gk)
    nq_d = jnp.sqrt(QK_NOPE * amax(gains[6, :QK_NOPE]) ** 2 + QK_ROPE * amax(gains[6, QK_NOPE:]) ** 2)
    nk_d = jnp.sqrt(QK_NOPE * amax(gains[7]) ** 2 + QK_ROPE * amax(gains[8]) ** 2)
    bounds = jnp.stack([
        jnp.maximum(jnp.maximum(bound64(gains[0], gains[3]), bound64(gains[1], gains[4])),
                    LOG2E * amax(p["a_sink"][l])),
        bound64(gains[2], gains[5]),
        1.02 * nq_d * nk_d])
    safe = (bounds <= LOGIT_SAFE).astype(jnp.int32)
    return {
        "safe": safe,
        "w_in": _prep_w_in(p["w_in"][l]),
        "wuq": wuq.astype(BF16),
        "wukv": jnp.concatenate([wuk, wuv], axis=1).astype(BF16),
        "gains": gains, "vec": vec, "wbr": wbr, "wout": p["w_out"][l].astype(BF16),
        "ffn_norm": p["ffn_norm"][l].reshape(1, D_MODEL),
        "sink": p["a_sink"][l][np.asarray(HEAD_PERM)].astype(F32),
        "lq1": p["c_lq1"][l].reshape(1, HD), "lk1": p["c_lk1"][l].reshape(1, HD),
        "lq2": p["c_lq2"][l].reshape(1, HD), "lk2": p["c_lk2"][l].reshape(1, HD),
        "subln": p["c_subln"][l].reshape(1, 2 * HD),
        "lam_init": 0.8 - 0.6 * math.exp(-0.3 * l),
    }


def _group_avg_mats():
    m64 = np.kron(np.eye(4), np.full((HD, HD), 1.0 / HD))
    one = np.zeros((LANES, LANES))
    one[:QK_NOPE, :QK_NOPE] = 1.0 / QK_NOPE
    one[QK_NOPE:QK_NOPE + QK_ROPE, QK_NOPE:QK_NOPE + QK_ROPE] = 1.0 / QK_ROPE
    md = np.kron(np.eye(2), one)
    return jnp.asarray(np.stack([m64, md]), BF16)


def _rope_tables(seq, rotary):
    if not rotary:
        one = jnp.ones((seq, LANES), F32)
        zero = jnp.zeros((seq, LANES), F32)
        return jnp.stack([one, zero, one, zero])
    t = jnp.arange(seq)

    def angles(rot_dim):
        nfreq = rot_dim // 4
        inv = jnp.power(ROPE_THETA, -jnp.arange(nfreq, dtype=F32) / nfreq)
        return jnp.concatenate([(t // GRID_W).astype(F32)[:, None] * inv,
                                (t % GRID_W).astype(F32)[:, None] * inv], axis=-1)

    ah = angles(HD)
    c64 = jnp.tile(jnp.cos(ah), (1, 4))
    s64 = jnp.tile(jnp.concatenate([-jnp.sin(ah), jnp.sin(ah)], axis=1), (1, 2))
    ar = angles(QK_ROPE)
    one = jnp.ones((seq, 1), F32)
    cd = jnp.concatenate([one * jnp.ones((1, QK_NOPE), F32), jnp.cos(ar), jnp.cos(ar),
                          one * jnp.ones((1, LANES - QK_NOPE - QK_ROPE), F32)], axis=1)
    sd = jnp.concatenate([jnp.zeros((seq, QK_NOPE), F32), -jnp.sin(ar), jnp.sin(ar),
                          jnp.zeros((seq, LANES - QK_NOPE - QK_ROPE), F32)], axis=1)
    return jnp.stack([c64, s64, cd, sd])


def kernel(x, c, ctx, c_ctx, w_mod, b_mod, mix_norm, ffn_norm, w_in, a_qn, a_kn, a_sink, b_qn, b_kn, c_qn, c_kn, c_lq1, c_lk1, c_lq2, c_lk2, c_subln, d_q_norm, d_kv_norm, d_w_uq, d_w_ukv, d_qn_nope, d_kn_nope, d_qn_rope, d_kn_rope, w_br, w_out, ff_w_gate, ff_w_up, ff_w_down, moe_router, moe_w1, moe_w3, moe_w2):
    p = dict(mix_norm=mix_norm, ffn_norm=ffn_norm, w_in=w_in, a_qn=a_qn, a_kn=a_kn, a_sink=a_sink,
             b_qn=b_qn, b_kn=b_kn, c_qn=c_qn, c_kn=c_kn, c_lq1=c_lq1, c_lk1=c_lk1, c_lq2=c_lq2,
             c_lk2=c_lk2, c_subln=c_subln, d_q_norm=d_q_norm, d_kv_norm=d_kv_norm, d_w_uq=d_w_uq,
             d_w_ukv=d_w_ukv, d_qn_nope=d_qn_nope, d_kn_nope=d_kn_nope, d_qn_rope=d_qn_rope,
             d_kn_rope=d_kn_rope, w_br=w_br, w_out=w_out)
    nb, seq, _ = x.shape
    assert seq % TM == 0 and ctx.shape[1] == CTX_LEN and nb <= 15
    depth = w_mod.shape[0]
    lat = x.reshape(nb * seq, D_MODEL)
    cx = ctx.reshape(nb * CTX_LEN, D_MODEL)

    cond = jnp.zeros((16, D_MODEL), F32).at[:nb].set(c).at[nb].set(c_ctx)
    mod_all = _modulation(cond, w_mod, b_mod).reshape(depth, 16, 6, D_MODEL)
    rope_lat = _rope_tables(seq, True)
    rope_ctx = _rope_tables(CTX_LEN, False)
    mavg = _group_avg_mats()

    for l in range(depth):
        last = l == depth - 1
        lp = _prep_layer(l, p)
        mod_lat = mod_all[l, :nb]
        mod_ctx = jnp.broadcast_to(mod_all[l, nb:nb + 1], (nb, 6, D_MODEL))
        common = (lp["vec"], lp["gains"])
        wts = (lp["wuq"], lp["wukv"])
        if last:
            kv_ctx = _project(cx, mod_ctx, *common, rope_ctx, mavg,
                              lp["w_in"][:, Q_COLS:Q_COLS + KV_PAD_COLS], *wts, seq=CTX_LEN,
                              tm=CTX_LEN, with_q=False)
        else:
            outs = _project(cx, mod_ctx, *common, rope_ctx, mavg, lp["w_in"], *wts, seq=CTX_LEN,
                            tm=CTX_LEN, with_q=True)
            kv_ctx, q_ctx, g_ctx = outs[:8], outs[8:12], outs[12]
        outs = _project(lat, mod_lat, *common, rope_lat, mavg, lp["w_in"], *wts, seq=seq,
                        tm=TM, with_q=True)
        kv_lat, q_lat, g_lat = outs[:8], outs[8:12], outs[12]
        br_lat = _mixers(q_lat, [kv_lat, kv_ctx], lp, seq_q=seq, tag="lat")
        moe = l % 2 == 1
        i = l // 2
        router_t = None
        if moe:
            wr = moe_router[i].T
            wr_hi = wr.astype(BF16)
            router_t = jnp.stack([wr_hi, (wr - wr_hi.astype(F32)).astype(BF16)])
        res = _merge(br_lat, g_lat, lat, mod_lat, lp["ffn_norm"], lp["wbr"], lp["wout"], router_t, seq=seq)
        if not last:
            br_ctx = _mixers(q_ctx, [kv_ctx], lp, seq_q=CTX_LEN, tag="ctx")
            res_ctx = _merge(br_ctx, g_ctx, cx, mod_ctx, lp["ffn_norm"], lp["wbr"], lp["wout"],
                             router_t, seq=CTX_LEN)
        if not moe:
            wg, wu, wd = (ff_w_gate[i].astype(BF16), ff_w_up[i].astype(BF16), ff_w_down[i].astype(BF16))
            lat = _dense_ffn(res[1], res[0], mod_lat, wg, wu, wd, seq=seq)
            if not last:
                cx = _dense_ffn(res_ctx[1], res_ctx[0], mod_ctx, wg, wu, wd, seq=CTX_LEN)
        else:
            w1, w3, w2 = moe_w1[i].astype(BF16), moe_w3[i].astype(BF16), moe_w2[i].astype(BF16)

            def routed(res_t, mod_t, seq_t):
                xr, hr, lg = res_t
                dest, wts_r, meta, nblk_max = _route(lg)
                xbuf = _dispatch(hr, dest, meta, nblk_max)
                ybuf = _moe_ffn(xbuf, meta[0, :nblk_max], meta[1, :1], nblk_max, w1, w3, w2)
                return _combine(ybuf, dest, wts_r, xr, mod_t, seq=seq_t)

            lat = routed(res, mod_lat, seq)
            if not last:
                cx = routed(res_ctx, mod_ctx, CTX_LEN)
    return lat.reshape(nb, seq, D_MODEL)
```
